```python
import jax, jax.numpy as jnp
from jax import lax
import numpy as np

D_MODEL = 1024
BATCH = 32
SEQ = 2048
DEPTH = 4

HEAD_DIM = 64
MIX_WIDTH = D_MODEL
N_HEADS_A = (MIX_WIDTH // 4) // HEAD_DIM
N_HEADS_B = (MIX_WIDTH // 2) // HEAD_DIM
N_KV_GROUPS_B = 2
N_POOL_GROUPS = 4
POOL_GROUP_DIM = (MIX_WIDTH // 4) // N_POOL_GROUPS
POOL_WINDOWS = (2, 4, 8, 16)
KV_RANK_A = 2 * HEAD_DIM
N_IDX_HEADS = 4
IDX_DIM = 32
TOPK_MAX = 256
CMP_BLOCK = 32
CMP_STRIDE = 16
SLC_BLOCK = 64
N_SLC = 16
WINDOW = 512
ROPE_THETA = 500000.0
ROT_DIM_HEAD = HEAD_DIM // 4
ROT_DIM_IDX = IDX_DIM // 4
D_FF = 4 * D_MODEL
Q_BLOCK = 32
EPS = 1e-6
NEG = -1e30

SPLIT_WIDTHS = (
    N_HEADS_A * HEAD_DIM,
    KV_RANK_A,
    N_IDX_HEADS * IDX_DIM,
    IDX_DIM,
    N_IDX_HEADS,
    N_HEADS_B * HEAD_DIM,
    N_KV_GROUPS_B * HEAD_DIM,
    N_KV_GROUPS_B * HEAD_DIM,
    N_KV_GROUPS_B * HEAD_DIM,
    N_KV_GROUPS_B * HEAD_DIM,
    N_KV_GROUPS_B * HEAD_DIM,
    N_KV_GROUPS_B * HEAD_DIM,
    N_HEADS_B * 3,
    N_POOL_GROUPS * POOL_GROUP_DIM,
)
D_IN = sum(SPLIT_WIDTHS)
SPLIT_OFFSETS = tuple(sum(SPLIT_WIDTHS[:i + 1]) for i in range(len(SPLIT_WIDTHS) - 1))

kernel_name = "hymba_dsa_nsa_pool_hybrid"

f32 = jnp.float32


def rms_norm(x, gain=None):
    xf = x.astype(f32)
    y = xf * lax.rsqrt(jnp.mean(xf * xf, axis=-1, keepdims=True) + EPS)
    if gain is not None:
        y = y * gain.astype(f32)
    return y.astype(x.dtype)


def rope_tables(positions, rot_dim):
    inv_freq = ROPE_THETA ** (-jnp.arange(0, rot_dim, 2, dtype=f32) / rot_dim)
    ang = positions.astype(f32)[..., None] * inv_freq
    return jnp.cos(ang), jnp.sin(ang)


def apply_partial_rope(x, cos, sin):
    half = cos.shape[-1]
    bshape = cos.shape[:2] + (1,) * (x.ndim - 3) + (half,)
    c, s = cos.reshape(bshape), sin.reshape(bshape)
    xf = x.astype(f32)
    x1, x2, rest = xf[..., :half], xf[..., half:2 * half], xf[..., 2 * half:]
    return jnp.concatenate([x1 * c - x2 * s, x1 * s + x2 * c, rest], axis=-1).astype(x.dtype)


def sweep_query_blocks(block_fn, seq_len):
    starts = jnp.arange(seq_len // Q_BLOCK, dtype=jnp.int32) * Q_BLOCK
    out = lax.map(block_fn, starts)
    out = jnp.swapaxes(out, 0, 1)
    return out.reshape((out.shape[0], seq_len) + out.shape[3:])


def take_rows(src, idx):
    return jax.vmap(lambda s, i: s[i])(src, idx)


def dsa_attention(q, k, v, q_idx, k_idx, w_idx):
    S = q.shape[1]
    n_keep = min(TOPK_MAX, S // 4)
    scale = HEAD_DIM ** -0.5
    key_pos = jnp.arange(S)

    def block(q0):
        t = q0 + jnp.arange(Q_BLOCK)
        qi = lax.dynamic_slice_in_dim(q_idx, q0, Q_BLOCK, 1)
        wi = lax.dynamic_slice_in_dim(w_idx, q0, Q_BLOCK, 1)
        qb = lax.dynamic_slice_in_dim(q, q0, Q_BLOCK, 1)
        rel = jax.nn.relu(jnp.einsum('bqhd,bsd->bqhs', qi, k_idx).astype(f32))
        score = jnp.einsum('bqhs,bqh->bqs', rel, wi.astype(f32))
        causal = key_pos[None, :] <= t[:, None]
        score = jnp.where(causal[None], score, -jnp.inf)
        _, idx = lax.top_k(score, n_keep)
        ks = take_rows(k, idx)
        vs = take_rows(v, idx)
        logits = jnp.einsum('bqhd,bqkd->bqhk', qb, ks).astype(f32) * scale
        valid = (idx <= t[None, :, None])[:, :, None, :]
        p = jax.nn.softmax(jnp.where(valid, logits, NEG), axis=-1)
        return jnp.einsum('bqhk,bqkd->bqhd', p.astype(vs.dtype), vs)

    return sweep_query_blocks(block, S)


def nsa_compress(kv, pe, w_phi):
    B, S, G, dh = kv.shape
    chunks = kv.reshape(B, S // CMP_STRIDE, CMP_STRIDE, G, dh)
    blocks = jnp.concatenate([chunks[:, :-1], chunks[:, 1:]], axis=2)
    blocks = blocks + pe[None, None, :, None, :]
    return jnp.einsum('bclgd,lde->bcge', blocks, w_phi)


def nsa_attention(q_nope, q_rope, k_cmp, v_cmp, k_slc, v_slc, k_win, v_win, gates):
    B, S, H, dh = q_nope.shape
    G = k_slc.shape[2]
    R = H // G
    NC = k_cmp.shape[1]
    NS = S // SLC_BLOCK
    n_sel = min(N_SLC, NS)
    scale = dh ** -0.5
    cmp_start = jnp.arange(NC) * CMP_STRIDE
    cmp_end = cmp_start + CMP_BLOCK - 1
    slc_start = jnp.arange(NS) * SLC_BLOCK
    overlap = ((cmp_start[:, None] <= slc_start[None, :] + SLC_BLOCK - 1)
               & (cmp_end[:, None] >= slc_start[None, :])).astype(f32)
    kblk = k_slc.reshape(B, NS, SLC_BLOCK, G, dh).transpose(0, 3, 1, 2, 4)
    vblk = v_slc.reshape(B, NS, SLC_BLOCK, G, dh).transpose(0, 3, 1, 2, 4)
    k_pad = jnp.pad(k_win, ((0, 0), (WINDOW, 0), (0, 0), (0, 0)))
    v_pad = jnp.pad(v_win, ((0, 0), (WINDOW, 0), (0, 0), (0, 0)))
    blk_ids = jnp.arange(NS)
    gather_blocks = jax.vmap(jax.vmap(lambda blocks, i: blocks[i]))

    def block(q0):
        t = q0 + jnp.arange(Q_BLOCK)
        qn = lax.dynamic_slice_in_dim(q_nope, q0, Q_BLOCK, 1).reshape(B, Q_BLOCK, G, R, dh)
        qr = lax.dynamic_slice_in_dim(q_rope, q0, Q_BLOCK, 1).reshape(B, Q_BLOCK, G, R, dh)
        g = jax.nn.sigmoid(lax.dynamic_slice_in_dim(gates, q0, Q_BLOCK, 1).astype(f32))
        g = g.reshape(B, Q_BLOCK, G, R, 3)
        lc = jnp.einsum('bqgrd,bcgd->bqgrc', qn, k_cmp).astype(f32) * scale
        cmask = (cmp_end[None, :] <= t[:, None])[None, :, None, None, :]
        p_cmp = jax.nn.softmax(jnp.where(cmask, lc, NEG), axis=-1) * cmask
        o_cmp = jnp.einsum('bqgrc,bcgd->bqgrd', p_cmp.astype(v_cmp.dtype), v_cmp)
        s_slc = jnp.einsum('bqgrc,cj->bqgj', p_cmp, overlap)
        cur = t // SLC_BLOCK
        forced = ((blk_ids[None, :] == 0) | (blk_ids[None, :] == cur[:, None])
                  | (blk_ids[None, :] == cur[:, None] - 1))
        admissible = slc_start[None, :] <= t[:, None]
        s_slc = jnp.where(forced[None, :, None, :], jnp.inf,
                          jnp.where(admissible[None, :, None, :], s_slc, -jnp.inf))
        _, sel = lax.top_k(s_slc, n_sel)
        sel_t = sel.transpose(0, 2, 1, 3)
        ks = gather_blocks(kblk, sel_t)
        vs = gather_blocks(vblk, sel_t)
        ls = jnp.einsum('bqgrd,bgqnld->bqgrnl', qr, ks).astype(f32) * scale
        tok = sel_t[..., None] * SLC_BLOCK + jnp.arange(SLC_BLOCK)
        smask = (tok <= t[None, None, :, None, None]).transpose(0, 2, 1, 3, 4)[:, :, :, None]
        ls = jnp.where(smask, ls, NEG).reshape(B, Q_BLOCK, G, R, n_sel * SLC_BLOCK)
        p_s = jax.nn.softmax(ls, axis=-1).reshape(B, Q_BLOCK, G, R, n_sel, SLC_BLOCK)
        o_slc = jnp.einsum('bqgrnl,bgqnld->bqgrd', p_s.astype(vs.dtype), vs)
        kw = lax.dynamic_slice_in_dim(k_pad, q0, WINDOW + Q_BLOCK, 1)
        vw = lax.dynamic_slice_in_dim(v_pad, q0, WINDOW + Q_BLOCK, 1)
        kpos = q0 - WINDOW + jnp.arange(WINDOW + Q_BLOCK)
        wmask = ((kpos[None, :] >= 0) & (kpos[None, :] <= t[:, None])
                 & (kpos[None, :] > t[:, None] - WINDOW))
        lw = jnp.einsum('bqgrd,bkgd->bqgrk', qr, kw).astype(f32) * scale
        p_w = jax.nn.softmax(jnp.where(wmask[None, :, None, None, :], lw, NEG), axis=-1)
        o_win = jnp.einsum('bqgrk,bkgd->bqgrd', p_w.astype(vw.dtype), vw)
        o = (g[..., 0:1] * o_cmp.astype(f32) + g[..., 1:2] * o_slc.astype(f32)
             + g[..., 2:3] * o_win.astype(f32))
        return o.astype(q_nope.dtype).reshape(B, Q_BLOCK, H, dh)

    return sweep_query_blocks(block, S)


def multiscale_pool(u, w_pool):
    S = u.shape[1]
    uf = u.astype(f32)
    csum = jnp.cumsum(uf, axis=1)
    count = jnp.arange(1, S + 1, dtype=f32)
    outs = []
    for gi, w in enumerate(POOL_WINDOWS):
        cs = csum[:, :, gi]
        prev = jnp.pad(cs[:, :S - w], ((0, 0), (w, 0), (0, 0)))
        mean = (cs - prev) / jnp.minimum(count, w)[None, :, None]
        outs.append(mean - uf[:, :, gi])
    pooled = jnp.stack(outs, axis=2).astype(u.dtype)
    return jnp.einsum('bsgc,gcd->bsgd', pooled, w_pool)


def hybrid_layer(x, cos_h, sin_h, cos_i, sin_i, w_in, g_kv_a, w_kv_up_a, w_cmp_k, w_cmp_v,
                 pe_cmp_k, pe_cmp_v, w_pool, pool_scale, g_out_a, g_out_b, w_out,
                 g_pre_mix, g_post_mix, g_pre_mlp, g_post_mlp, w_up, w_down):
    B, S, _ = x.shape
    G = N_KV_GROUPS_B
    h = rms_norm(x, g_pre_mix)
    proj = h @ w_in
    (q_a, ckv_a, q_idx, k_idx, w_idx, q_b, k_cmp, v_cmp, k_slc, v_slc,
     k_win, v_win, gate_b, u_c) = jnp.split(proj, SPLIT_OFFSETS, axis=-1)
    q_a = apply_partial_rope(q_a.reshape(B, S, N_HEADS_A, HEAD_DIM), cos_h, sin_h)
    kv_a = rms_norm(ckv_a, g_kv_a) @ w_kv_up_a
    k_a = apply_partial_rope(kv_a[..., :HEAD_DIM], cos_h, sin_h)
    v_a = kv_a[..., HEAD_DIM:]
    q_idx = apply_partial_rope(q_idx.reshape(B, S, N_IDX_HEADS, IDX_DIM), cos_i, sin_i)
    k_idx = apply_partial_rope(k_idx, cos_i, sin_i)
    o_a = dsa_attention(q_a, k_a, v_a, q_idx, k_idx, w_idx).reshape(B, S, -1)
    q_b = q_b.reshape(B, S, N_HEADS_B, HEAD_DIM)
    kc = nsa_compress(k_cmp.reshape(B, S, G, HEAD_DIM), pe_cmp_k, w_cmp_k)
    vc = nsa_compress(v_cmp.reshape(B, S, G, HEAD_DIM), pe_cmp_v, w_cmp_v)
    o_b = nsa_attention(q_b, apply_partial_rope(q_b, cos_h, sin_h), kc, vc,
                        apply_partial_rope(k_slc.reshape(B, S, G, HEAD_DIM), cos_h, sin_h),
                        v_slc.reshape(B, S, G, HEAD_DIM),
                        apply_partial_rope(k_win.reshape(B, S, G, HEAD_DIM), cos_h, sin_h),
                        v_win.reshape(B, S, G, HEAD_DIM),
                        gate_b.reshape(B, S, N_HEADS_B, 3)).reshape(B, S, -1)
    o_c = multiscale_pool(u_c.reshape(B, S, N_POOL_GROUPS, POOL_GROUP_DIM), w_pool).reshape(B, S, -1)
    mixed = jnp.concatenate([rms_norm(o_a, g_out_a), rms_norm(o_b, g_out_b),
                             rms_norm(o_c) * pool_scale.reshape(-1)], axis=-1)
    x = x + rms_norm(mixed @ w_out, g_post_mix)
    h = rms_norm(x, g_pre_mlp)
    f = jnp.square(jax.nn.relu(h @ w_up)) @ w_down
    return x + rms_norm(f, g_post_mlp)


def setup_inputs(seed: int = 0) -> dict:
    key = jax.random.key(seed)
    ks = jax.random.split(key, 24)
    G = N_KV_GROUPS_B

    def nrm(k, shape, fan_in):
        return jax.random.normal(k, shape, f32) * (fan_in ** -0.5)

    def gain(k, shape):
        return 1.0 + 0.1 * jax.random.normal(k, shape, f32)

    x = jax.random.normal(ks[0], (BATCH, SEQ, D_MODEL), f32)
    offset = jax.random.randint(ks[1], (BATCH, 1), 0, 1024, dtype=jnp.int32)
    positions = jnp.arange(SEQ, dtype=jnp.int32)[None, :] + offset
    return {
        "x": x,
        "positions": positions,
        "w_in": nrm(ks[2], (DEPTH, D_MODEL, D_IN), D_MODEL),
        "g_kv_a": gain(ks[3], (DEPTH, KV_RANK_A)),
        "w_kv_up_a": nrm(ks[4], (DEPTH, KV_RANK_A, 2 * HEAD_DIM), KV_RANK_A),
        "w_cmp_k": nrm(ks[5], (DEPTH, CMP_BLOCK, HEAD_DIM, HEAD_DIM), CMP_BLOCK * HEAD_DIM),
        "w_cmp_v": nrm(ks[6], (DEPTH, CMP_BLOCK, HEAD_DIM, HEAD_DIM), CMP_BLOCK * HEAD_DIM),
        "pe_cmp_k": 0.02 * jax.random.normal(ks[7], (DEPTH, CMP_BLOCK, HEAD_DIM), f32),
        "pe_cmp_v": 0.02 * jax.random.normal(ks[8], (DEPTH, CMP_BLOCK, HEAD_DIM), f32),
        "w_pool": nrm(ks[9], (DEPTH, N_POOL_GROUPS, POOL_GROUP_DIM, POOL_GROUP_DIM), POOL_GROUP_DIM),
        "pool_scale": gain(ks[10], (DEPTH, N_POOL_GROUPS, POOL_GROUP_DIM)),
        "g_out_a": gain(ks[11], (DEPTH, N_HEADS_A * HEAD_DIM)),
        "g_out_b": gain(ks[12], (DEPTH, N_HEADS_B * HEAD_DIM)),
        "w_out": nrm(ks[13], (DEPTH, MIX_WIDTH, D_MODEL), MIX_WIDTH),
        "g_pre_mix": gain(ks[14], (DEPTH, D_MODEL)),
        "g_post_mix": gain(ks[15], (DEPTH, D_MODEL)),
        "g_pre_mlp": gain(ks[16], (DEPTH, D_MODEL)),
        "g_post_mlp": gain(ks[17], (DEPTH, D_MODEL)),
        "w_up": nrm(ks[18], (DEPTH, D_MODEL, D_FF), D_MODEL),
        "w_down": nrm(ks[19], (DEPTH, D_FF, D_MODEL), D_FF),
    }


def reference(x, positions, w_in, g_kv_a, w_kv_up_a, w_cmp_k, w_cmp_v, pe_cmp_k, pe_cmp_v,
              w_pool, pool_scale, g_out_a, g_out_b, w_out, g_pre_mix, g_post_mix,
              g_pre_mlp, g_post_mlp, w_up, w_down):
    cos_h, sin_h = rope_tables(positions, ROT_DIM_HEAD)
    cos_i, sin_i = rope_tables(positions, ROT_DIM_IDX)
    for l in range(DEPTH):
        x = hybrid_layer(x, cos_h, sin_h, cos_i, sin_i, w_in[l], g_kv_a[l], w_kv_up_a[l],
                         w_cmp_k[l], w_cmp_v[l], pe_cmp_k[l], pe_cmp_v[l], w_pool[l],
                         pool_scale[l], g_out_a[l], g_out_b[l], w_out[l], g_pre_mix[l],
                         g_post_mix[l], g_pre_mlp[l], g_post_mlp[l], w_up[l], w_down[l])
    return x
```

```python
import functools

import jax
import jax.numpy as jnp
from jax import lax
from jax.experimental import pallas as pl
from jax.experimental.pallas import tpu as pltpu

f32 = jnp.float32
bf16 = jnp.bfloat16

HEAD_DIM = 64
N_HEADS_A = 4
N_HEADS_B = 8
N_KV_GROUPS_B = 2
N_POOL_GROUPS = 4
POOL_GROUP_DIM = 64
POOL_WINDOWS = (2, 4, 8, 16)
KV_RANK_A = 128
N_IDX_HEADS = 4
IDX_DIM = 32
TOPK_MAX = 256
CMP_BLOCK = 32
CMP_STRIDE = 16
SLC_BLOCK = 64
N_SLC = 16
WINDOW = 512
ROPE_THETA = 500000.0
ROT_DIM_HEAD = HEAD_DIM // 4
ROT_DIM_IDX = IDX_DIM // 4
EPS = 1e-6
NEG = -1e30
SCALE = HEAD_DIM ** -0.5

LANES = 128
TOKEN_TILE = 512
QUERY_TILE = 256
KEY_CHUNK = 256
VMEM_LIMIT = 56 * 1024 * 1024

_COLS = {}
_off = 0
for _name, _w in (("qa", 256), ("ckv", 128), ("qidx", 128), ("kidx", 128), ("qb", 512),
                  ("kvcmp", 256), ("kslc", 256), ("vslc", 256), ("kwin", 256), ("vwin", 256),
                  ("misc", 128), ("uc", 256)):
    _COLS[_name] = (_off, _off + _w)
    _off += _w
N_PROJ = _off
GATE_LANE0 = N_IDX_HEADS


def _rms(v, gain=None):
    y = v * lax.rsqrt(jnp.mean(v * v, axis=-1, keepdims=True) + EPS)
    return y if gain is None else y * gain


def _rope128(v, c, sa, sb, half):
    return v * c + pltpu.roll(v, LANES - half, 1) * sa + pltpu.roll(v, half, 1) * sb


def _nt_dot(a, b):
    return lax.dot_general(a, b, (((1,), (1,)), ((), ())), preferred_element_type=f32)


def _proj_kernel(x_ref, g_ref, w_ref, gkv_ref, wkv_ref, rh_ref, ri_ref,
                 qa_ref, ka_ref, va_ref, qidx_ref, kidx_ref, qbn_ref, qbr_ref,
                 kvcmp_ref, kslc_ref, vslc_ref, kwin_ref, vwin_ref, misc_ref, uc_ref):
    h = _rms(x_ref[...], g_ref[...])
    p = jnp.dot(h.astype(bf16), w_ref[...], preferred_element_type=f32)
    hc, hsa, hsb = rh_ref[:, 0:128], rh_ref[:, 128:256], rh_ref[:, 256:384]
    ic, isa, isb = ri_ref[:, 0:128], ri_ref[:, 128:256], ri_ref[:, 256:384]
    hh, ih = ROT_DIM_HEAD // 2, ROT_DIM_IDX // 2

    def col(name, j=0):
        a = _COLS[name][0] + j * LANES
        return p[:, a:a + LANES]

    for j in range(2):
        qa_ref[:, j * LANES:(j + 1) * LANES] = (_rope128(col("qa", j), hc, hsa, hsb, hh) * SCALE).astype(bf16)
    kv = jnp.dot(_rms(col("ckv"), gkv_ref[...]).astype(bf16), wkv_ref[...], preferred_element_type=f32)
    ka_ref[...] = _rope128(kv[:, 0:LANES], hc, hsa, hsb, hh).astype(bf16)
    va_ref[...] = kv[:, LANES:2 * LANES].astype(bf16)
    qidx_ref[...] = _rope128(col("qidx"), ic, isa, isb, ih).astype(bf16)
    kidx_ref[...] = _rope128(col("kidx"), ic, isa, isb, ih).astype(bf16)
    for j in range(4):
        q = col("qb", j)
        qbn_ref[:, j * LANES:(j + 1) * LANES] = (q * SCALE).astype(bf16)
        qbr_ref[:, j * LANES:(j + 1) * LANES] = (_rope128(q, hc, hsa, hsb, hh) * SCALE).astype(bf16)
    for j in range(2):
        sl = slice(j * LANES, (j + 1) * LANES)
        kvcmp_ref[:, sl] = col("kvcmp", j)
        kslc_ref[:, sl] = _rope128(col("kslc", j), hc, hsa, hsb, hh).astype(bf16)
        vslc_ref[:, sl] = col("vslc", j).astype(bf16)
        kwin_ref[:, sl] = _rope128(col("kwin", j), hc, hsa, hsb, hh).astype(bf16)
        vwin_ref[:, sl] = col("vwin", j).astype(bf16)
        uc_ref[:, sl] = col("uc", j)
    misc_ref[...] = col("misc")


def _proj_call(x, g, w, gkv, wkv, rope_h, rope_i):
    T, D = x.shape
    TM = TOKEN_TILE
    row = lambda w_: pl.BlockSpec((TM, w_), lambda i: (i, 0))
    full = lambda a: pl.BlockSpec(a.shape, lambda i: (0, 0))
    outs = [("qa", 256, bf16), ("ka", 128, bf16), ("va", 128, bf16), ("qidx", 128, bf16),
            ("kidx", 128, bf16), ("qbn", 512, bf16), ("qbr", 512, bf16), ("kvcmp", 256, f32),
            ("kslc", 256, bf16), ("vslc", 256, bf16), ("kwin", 256, bf16), ("vwin", 256, bf16),
            ("misc", 128, f32), ("uc", 256, f32)]
    res = pl.pallas_call(
        _proj_kernel,
        grid=(T // TM,),
        in_specs=[row(D), full(g), full(w), full(gkv), full(wkv), row(384), row(384)],
        out_specs=[row(w_) for _, w_, _ in outs],
        out_shape=[jax.ShapeDtypeStruct((T, w_), dt) for _, w_, dt in outs],
        compiler_params=pltpu.CompilerParams(dimension_semantics=("arbitrary",),
                                             vmem_limit_bytes=VMEM_LIMIT),
        name="proj",
    )(x, g, w, gkv, wkv, rope_h, rope_i)
    return dict(zip([n for n, _, _ in outs], res))


def _count_ge(sc, v):
    return jnp.sum(jnp.where(sc >= v, 1.0, 0.0), axis=1, keepdims=True)


def _topk_threshold(sc, need_search, n_keep):
    k = float(n_keep)
    big = jnp.float32(3.0e38)
    lo = jnp.min(jnp.where(sc == -jnp.inf, big, sc), axis=1, keepdims=True)
    hi = jnp.max(sc, axis=1, keepdims=True)
    c_hi = _count_ge(sc, hi)
    thr = jnp.where(need_search, hi, -big)
    active = need_search & (c_hi < k)
    n_active = jnp.sum(active.astype(jnp.int32))

    def cond(st):
        return st[4] > 0

    def body(st):
        lo, hi, thr, act, _ = st
        act = act > 0.5
        mid = 0.5 * lo + 0.5 * hi
        collapsed = (mid <= lo) | (mid >= hi)
        c = _count_ge(sc, mid)
        found = (c == k) & ~collapsed
        thr = jnp.where(act & found, mid, jnp.where(act & collapsed, lo, thr))
        go = act & ~found & ~collapsed
        lo = jnp.where(go & (c > k), mid, lo)
        hi = jnp.where(go & (c < k), mid, hi)
        return lo, hi, thr, go.astype(f32), jnp.sum(go.astype(jnp.int32))

    _, _, thr, _, _ = lax.while_loop(cond, body, (lo, hi, thr, active.astype(f32), n_active))
    return thr


def _dsa_kernel(qa_ref, qidx_ref, misc_ref, kidx_ref, ka_ref, va_ref, o_ref, *, n_keep):
    TQ = qa_ref.shape[0]
    S = ka_ref.shape[0]
    q0 = pl.program_id(1) * TQ
    tpos = q0 + lax.broadcasted_iota(jnp.int32, (TQ, 1), 0)
    kpos = lax.broadcasted_iota(jnp.int32, (1, S), 1)
    causal = kpos <= tpos
    lane = lax.broadcasted_iota(jnp.int32, (1, LANES), 1)

    qidx = qidx_ref[...]
    kidx = kidx_ref[...]
    misc = misc_ref[...]
    score = jnp.zeros((TQ, S), f32)
    for h in range(N_IDX_HEADS):
        qh = jnp.where(lane // IDX_DIM == h, qidx, jnp.zeros_like(qidx))
        score = score + jnp.maximum(_nt_dot(qh, kidx), 0.0) * misc[:, h:h + 1]
    sc = jnp.where(causal, score, -jnp.inf)

    thr = _topk_threshold(sc, (tpos + 1) > n_keep, n_keep)
    above = sc > thr
    tie = sc == thr
    needed = float(n_keep) - jnp.sum(jnp.where(above, 1.0, 0.0), axis=1, keepdims=True)
    ci = lax.broadcasted_iota(jnp.int32, (KEY_CHUNK, KEY_CHUNK), 0)
    cj = lax.broadcasted_iota(jnp.int32, (KEY_CHUNK, KEY_CHUNK), 1)
    strict_upper = jnp.where(ci < cj, 1.0, 0.0).astype(bf16)
    tie_f = jnp.where(tie, 1.0, 0.0)
    carry = jnp.zeros((TQ, 1), f32)
    ranks = []
    for c in range(S // KEY_CHUNK):
        t_c = tie_f[:, c * KEY_CHUNK:(c + 1) * KEY_CHUNK]
        ranks.append(jnp.dot(t_c.astype(bf16), strict_upper, preferred_element_type=f32) + carry)
        carry = carry + jnp.sum(t_c, axis=1, keepdims=True)
    sel = above | (tie & (jnp.concatenate(ranks, axis=1) < needed))

    ka = ka_ref[...]
    va = va_ref[...]
    for j in range(N_HEADS_A * HEAD_DIM // LANES):
        qc = qa_ref[:, j * LANES:(j + 1) * LANES]
        out = jnp.zeros((TQ, LANES), f32)
        for half in range(LANES // HEAD_DIM):
            in_head = lane // HEAD_DIM == half
            s = _nt_dot(jnp.where(in_head, qc, jnp.zeros_like(qc)), ka)
            s = jnp.where(sel, s, NEG)
            e = jnp.exp(s - jnp.max(s, axis=1, keepdims=True))
            o = jnp.dot(e.astype(bf16), va, preferred_element_type=f32)
            out = out + jnp.where(in_head, o / jnp.sum(e, axis=1, keepdims=True), 0.0)
        o_ref[:, j * LANES:(j + 1) * LANES] = out


def _dsa_call(pr, B, S):
    TQ = min(QUERY_TILE, S)
    n_keep = min(TOPK_MAX, S // 4)
    qspec = lambda w_: pl.BlockSpec((TQ, w_), lambda b, i: (b * (S // TQ) + i, 0))
    kspec = lambda w_: pl.BlockSpec((S, w_), lambda b, i: (b, 0))
    return pl.pallas_call(
        functools.partial(_dsa_kernel, n_keep=n_keep),
        grid=(B, S // TQ),
        in_specs=[qspec(256), qspec(128), qspec(128), kspec(128), kspec(128), kspec(128)],
        out_specs=qspec(256),
        out_shape=jax.ShapeDtypeStruct((B * S, 256), f32),
        compiler_params=pltpu.CompilerParams(dimension_semantics=("arbitrary", "arbitrary"),
                                             vmem_limit_bytes=VMEM_LIMIT),
        name="dsa",
    )(pr["qa"], pr["qidx"], pr["misc"], pr["kidx"], pr["ka"], pr["va"])


def _compress_kernel(ch_ref, pe_ref, w_ref, kc_ref, vc_ref):
    nch = ch_ref.shape[1]
    rowid = lax.broadcasted_iota(jnp.int32, (nch, 1), 0)
    for kv, o_ref in ((0, kc_ref), (1, vc_ref)):
        for g in range(N_KV_GROUPS_B):
            ch = ch_ref[kv * N_KV_GROUPS_B + g]
            top = jnp.dot((ch + pe_ref[kv, 0:1, :]).astype(bf16), w_ref[kv, 0], preferred_element_type=f32)
            bot = jnp.dot((ch + pe_ref[kv, 1:2, :]).astype(bf16), w_ref[kv, 1], preferred_element_type=f32)
            blk = top + pltpu.roll(bot, nch - 1, 0)
            o_ref[:, g * LANES:(g + 1) * LANES] = jnp.where(rowid < nch - 1, blk, 0.0).astype(bf16)


def _compress_call(kvcmp, pe_k, pe_v, w_k, w_v, B, S):
    nch = S // CMP_STRIDE
    G, dh = N_KV_GROUPS_B, HEAD_DIM
    ch = kvcmp.reshape(B, nch, CMP_STRIDE, 2 * G, dh).transpose(0, 3, 1, 2, 4)
    ch = ch.reshape(B, 2 * G, nch, CMP_STRIDE * dh)
    pe = jnp.stack([pe_k, pe_v]).reshape(2, 2, CMP_STRIDE * dh)
    w = jnp.stack([w_k, w_v]).reshape(2, 2, CMP_STRIDE * dh, dh)
    w = jnp.concatenate([w, w], axis=-1).astype(bf16)
    out = jax.ShapeDtypeStruct((B * nch, G * LANES), bf16)
    ospec = pl.BlockSpec((nch, G * LANES), lambda b: (b, 0))
    return pl.pallas_call(
        _compress_kernel,
        grid=(B,),
        in_specs=[pl.BlockSpec((None, 2 * G, nch, CMP_STRIDE * dh), lambda b: (b, 0, 0, 0)),
                  pl.BlockSpec(pe.shape, lambda b: (0, 0, 0)),
                  pl.BlockSpec(w.shape, lambda b: (0, 0, 0, 0))],
        out_specs=[ospec, ospec],
        out_shape=[out, out],
        compiler_params=pltpu.CompilerParams(dimension_semantics=("arbitrary",),
                                             vmem_limit_bytes=VMEM_LIMIT),
        name="compress",
    )(ch, pe, w)


def _split3(v):
    a = v.astype(bf16)
    r = v - a.astype(f32)
    b = r.astype(bf16)
    c = (r - b.astype(f32)).astype(bf16)
    return a, b, c


def _masked_attention(q, k, v, mask):
    s = jnp.where(mask, _nt_dot(q, k), NEG)
    e = jnp.exp(s - jnp.max(s, axis=1, keepdims=True))
    o = jnp.dot(e.astype(bf16), v, preferred_element_type=f32)
    return o / jnp.sum(e, axis=1, keepdims=True)


def _nsa_kernel(qn_ref, qr_ref, misc_ref, kc_ref, vc_ref, kslc_ref, vslc_ref, kwin_ref, vwin_ref,
                o_ref, *, n_sel, win_len):
    TQ = qn_ref.shape[0]
    S = kslc_ref.shape[0]
    nch = kc_ref.shape[0]
    n_cmp = nch - 1
    n_blk = S // SLC_BLOCK
    G = N_KV_GROUPS_B
    R = N_HEADS_B // G
    q0 = pl.program_id(1) * TQ
    tpos = q0 + lax.broadcasted_iota(jnp.int32, (TQ, 1), 0)
    lane = lax.broadcasted_iota(jnp.int32, (1, LANES), 1)
    kpos = lax.broadcasted_iota(jnp.int32, (1, S), 1)
    causal = kpos <= tpos

    gate = jax.nn.sigmoid(misc_ref[...])

    def gate_of(h, branch):
        a = GATE_LANE0 + 3 * h + branch
        return gate[:, a:a + 1]

    cpos = lax.broadcasted_iota(jnp.int32, (1, nch), 1)
    cmask = (cpos * CMP_STRIDE + CMP_BLOCK - 1 <= tpos) & (cpos < n_cmp)
    oc = lax.broadcasted_iota(jnp.int32, (nch, LANES), 0)
    oj = lax.broadcasted_iota(jnp.int32, (nch, LANES), 1)
    overlap = ((oc * CMP_STRIDE <= oj * SLC_BLOCK + SLC_BLOCK - 1)
               & (oc * CMP_STRIDE + CMP_BLOCK - 1 >= oj * SLC_BLOCK)
               & (oc < n_cmp) & (oj < n_blk))
    overlap = jnp.where(overlap, 1.0, 0.0).astype(bf16)
    ej = lax.broadcasted_iota(jnp.int32, (LANES, S), 0)
    es = lax.broadcasted_iota(jnp.int32, (LANES, S), 1)
    expand = jnp.where(es // SLC_BLOCK == ej, 1.0, 0.0).astype(bf16)
    cur = tpos // SLC_BLOCK
    forced = (lane == 0) | (lane == cur) | (lane == cur - 1)
    admissible = lane * SLC_BLOCK <= tpos
    kstart = pl.multiple_of(jnp.clip(q0 - WINDOW, 0, S - win_len), LANES)
    wpos = kstart + lax.broadcasted_iota(jnp.int32, (1, win_len), 1)
    wmask = (wpos <= tpos) & (wpos > tpos - WINDOW)

    acc = [jnp.zeros((TQ, LANES), f32) for _ in range(N_HEADS_B * HEAD_DIM // LANES)]
    for g in range(G):
        gs = slice(g * LANES, (g + 1) * LANES)
        kc, vc = kc_ref[:, gs], vc_ref[:, gs]
        ks, vs = kslc_ref[:, gs], vslc_ref[:, gs]
        kw, vw = kwin_ref[pl.ds(kstart, win_len), gs], vwin_ref[pl.ds(kstart, win_len), gs]
        heads = []
        psum = jnp.zeros((TQ, nch), f32)
        for r in range(R):
            h = g * R + r
            j = h // 2
            in_head = lane // HEAD_DIM == h % 2
            qn = qn_ref[:, j * LANES:(j + 1) * LANES]
            qn = jnp.where(in_head, qn, jnp.zeros_like(qn))
            lc = jnp.where(cmask, _nt_dot(qn, kc), NEG)
            e = jnp.where(cmask, jnp.exp(lc - jnp.max(lc, axis=1, keepdims=True)), 0.0)
            l = jnp.sum(e, axis=1, keepdims=True)
            p = e / jnp.where(l > 0.0, l, 1.0)
            psum = psum + p
            o_cmp = jnp.dot(p.astype(bf16), vc, preferred_element_type=f32)
            heads.append((h, j, in_head, gate_of(h, 0) * o_cmp))
        s_blk = sum(jnp.dot(piece, overlap, preferred_element_type=f32) for piece in _split3(psum))
        s_blk = jnp.where(forced, jnp.inf, jnp.where(admissible, s_blk, -jnp.inf))
        s_blk = jnp.where(lane < n_blk, s_blk, -jnp.inf)
        rank = jnp.zeros((TQ, LANES), f32)
        for i in range(n_blk):
            col = s_blk[:, i:i + 1]
            beats = (col > s_blk) | ((col == s_blk) & (lane > i))
            rank = rank + jnp.where(beats, 1.0, 0.0)
        chosen = jnp.where((rank < float(n_sel)) & (lane < n_blk), 1.0, 0.0).astype(bf16)
        smask = (jnp.dot(chosen, expand, preferred_element_type=f32) > 0.5) & causal
        for h, j, in_head, o_gated in heads:
            qr = qr_ref[:, j * LANES:(j + 1) * LANES]
            qr = jnp.where(in_head, qr, jnp.zeros_like(qr))
            o_gated = o_gated + gate_of(h, 1) * _masked_attention(qr, ks, vs, smask)
            o_gated = o_gated + gate_of(h, 2) * _masked_attention(qr, kw, vw, wmask)
            acc[j] = acc[j] + jnp.where(in_head, o_gated, 0.0)
    for j, a in enumerate(acc):
        o_ref[:, j * LANES:(j + 1) * LANES] = a


def _nsa_call(pr, kc, vc, B, S):
    TQ = min(QUERY_TILE, S)
    n_sel = min(N_SLC, S // SLC_BLOCK)
    win_len = min(WINDOW + TQ, S)
    nch = S // CMP_STRIDE
    qspec = lambda w_: pl.BlockSpec((TQ, w_), lambda b, i: (b * (S // TQ) + i, 0))
    kspec = lambda w_: pl.BlockSpec((S, w_), lambda b, i: (b, 0))
    cspec = pl.BlockSpec((nch, 256), lambda b, i: (b, 0))
    return pl.pallas_call(
        functools.partial(_nsa_kernel, n_sel=n_sel, win_len=win_len),
        grid=(B, S // TQ),
        in_specs=[qspec(512), qspec(512), qspec(128), cspec, cspec,
                  kspec(256), kspec(256), kspec(256), kspec(256)],
        out_specs=qspec(512),
        out_shape=jax.ShapeDtypeStruct((B * S, 512), f32),
        compiler_params=pltpu.CompilerParams(dimension_semantics=("arbitrary", "arbitrary"),
                                             vmem_limit_bytes=VMEM_LIMIT),
        name="nsa",
    )(pr["qbn"], pr["qbr"], pr["misc"], kc, vc, pr["kslc"], pr["vslc"], pr["kwin"], pr["vwin"])


def _pool_kernel(u_ref, w_ref, o_ref):
    u = u_ref[...]
    S, C = u.shape
    row = lax.broadcasted_iota(jnp.int32, (S, 1), 0)
    grp = lax.broadcasted_iota(jnp.int32, (1, C), 1) // POOL_GROUP_DIM
    run, width = u, 1
    win_sum = jnp.zeros_like(u)
    win_len = jnp.zeros((1, C), f32)
    for gi, w in enumerate(POOL_WINDOWS):
        while width < w:
            run = run + jnp.where(row >= width, pltpu.roll(run, width, 0), 0.0)
            width *= 2
        assert width == w, "pooling windows must be increasing powers of two"
        win_sum = jnp.where(grp == gi, run, win_sum)
        win_len = jnp.where(grp == gi, float(w), win_len)
    pooled = win_sum / jnp.minimum((row + 1).astype(f32), win_len) - u
    o_ref[...] = jnp.dot(pooled.astype(bf16), w_ref[...], preferred_element_type=f32)


def _pool_call(uc, w_pool, B, S):
    C = N_POOL_GROUPS * POOL_GROUP_DIM
    wbd = jnp.zeros((C, C), f32)
    for gi in range(N_POOL_GROUPS):
        sl = slice(gi * POOL_GROUP_DIM, (gi + 1) * POOL_GROUP_DIM)
        wbd = wbd.at[sl, sl].set(w_pool[gi])
    spec = pl.BlockSpec((S, C), lambda b: (b, 0))
    return pl.pallas_call(
        _pool_kernel,
        grid=(B,),
        in_specs=[spec, pl.BlockSpec((C, C), lambda b: (0, 0))],
        out_specs=spec,
        out_shape=jax.ShapeDtypeStruct((B * S, C), f32),
        compiler_params=pltpu.CompilerParams(dimension_semantics=("arbitrary",),
                                             vmem_limit_bytes=VMEM_LIMIT),
        name="pool",
    )(uc, wbd.astype(bf16))


def _merge_mlp_kernel(x_ref, oa_ref, ob_ref, oc_ref, ga_ref, gb_ref, ps_ref, wo_ref, gpm_ref,
                      gmlp_ref, wup_ref, wdn_ref, gpost_ref, o_ref):
    mixed = jnp.concatenate([_rms(oa_ref[...], ga_ref[...]), _rms(ob_ref[...], gb_ref[...]),
                             _rms(oc_ref[...]) * ps_ref[...]], axis=1).astype(bf16)
    x = x_ref[...] + _rms(jnp.dot(mixed, wo_ref[...], preferred_element_type=f32), gpm_ref[...])
    h = _rms(x, gmlp_ref[...]).astype(bf16)
    d_ff = wup_ref.shape[1]
    f = jnp.zeros_like(x)
    for c in range(d_ff // x.shape[1]):
        sl = slice(c * x.shape[1], (c + 1) * x.shape[1])
        up = jnp.maximum(jnp.dot(h, wup_ref[:, sl], preferred_element_type=f32), 0.0)
        f = f + jnp.dot((up * up).astype(bf16), wdn_ref[sl, :], preferred_element_type=f32)
    o_ref[...] = x + _rms(f, gpost_ref[...])


def _merge_mlp_call(x, oa, ob, oc, ga, gb, ps, wo, gpm, gmlp, wup, wdn, gpost):
    T, D = x.shape
    TM = TOKEN_TILE
    row = lambda a: pl.BlockSpec((TM, a.shape[1]), lambda i: (i, 0))
    full = lambda a: pl.BlockSpec(a.shape, lambda i: (0, 0))
    args = (x, oa, ob, oc, ga, gb, ps, wo, gpm, gmlp, wup, wdn, gpost)
    return pl.pallas_call(
        _merge_mlp_kernel,
        grid=(T // TM,),
        in_specs=[row(a) for a in args[:4]] + [full(a) for a in args[4:]],
        out_specs=pl.BlockSpec((TM, D), lambda i: (i, 0)),
        out_shape=jax.ShapeDtypeStruct((T, D), f32),
        compiler_params=pltpu.CompilerParams(dimension_semantics=("arbitrary",),
                                             vmem_limit_bytes=VMEM_LIMIT),
        name="merge_mlp",
    )(*args)


def _rope_lane_tables(positions, rot_dim, period):
    half = rot_dim // 2
    inv_freq = ROPE_THETA ** (-jnp.arange(0, rot_dim, 2, dtype=f32) / rot_dim)
    ang = positions.astype(f32).reshape(-1, 1) * inv_freq
    cos, sin = jnp.cos(ang), jnp.sin(ang)
    T = ang.shape[0]
    ones, zeros, zh = jnp.ones((T, period - rot_dim), f32), jnp.zeros((T, period - rot_dim), f32), jnp.zeros((T, half), f32)
    c = jnp.concatenate([cos, cos, ones], axis=1)
    sa = jnp.concatenate([-sin, zh, zeros], axis=1)
    sb = jnp.concatenate([zh, sin, zeros], axis=1)
    reps = LANES // period
    return jnp.concatenate([jnp.tile(t, (1, reps)) for t in (c, sa, sb)], axis=1)


def _dup_halves(w):
    D = w.shape[0]
    g = w.reshape(D, -1, 1, HEAD_DIM)
    return jnp.broadcast_to(g, (D, g.shape[1], 2, HEAD_DIM)).reshape(D, -1)


def _layout_w_in(w_in):
    D = w_in.shape[0]
    widths = (256, 128, 128, 32, 4, 512, 128, 128, 128, 128, 128, 128, 24, 256)
    offs = [0]
    for w_ in widths:
        offs.append(offs[-1] + w_)
    (q_a, ckv, q_idx, k_idx, w_idx, q_b, k_cmp, v_cmp, k_slc, v_slc, k_win, v_win, gates, u_c) = [
        w_in[:, offs[i]:offs[i + 1]] for i in range(len(widths))]
    misc = jnp.concatenate([w_idx, gates, jnp.zeros((D, LANES - 28), w_in.dtype)], axis=1)
    cols = [q_a, ckv, q_idx, jnp.tile(k_idx, (1, LANES // IDX_DIM)), q_b, k_cmp, v_cmp,
            _dup_halves(k_slc), _dup_halves(v_slc), _dup_halves(k_win), _dup_halves(v_win), misc, u_c]
    w = jnp.concatenate(cols, axis=1)
    assert w.shape[1] == N_PROJ
    return w.astype(bf16)


def kernel(x, positions, w_in, g_kv_a, w_kv_up_a, w_cmp_k, w_cmp_v, pe_cmp_k, pe_cmp_v, w_pool,
           pool_scale, g_out_a, g_out_b, w_out, g_pre_mix, g_post_mix, g_pre_mlp, g_post_mlp,
           w_up, w_down):
    B, S, D = x.shape
    depth = w_in.shape[0]
    assert S % QUERY_TILE == 0 or S < QUERY_TILE
    assert (B * S) % TOKEN_TILE == 0
    rope_h = _rope_lane_tables(positions, ROT_DIM_HEAD, HEAD_DIM)
    rope_i = _rope_lane_tables(positions, ROT_DIM_IDX, IDX_DIM)
    row = lambda v: v.reshape(1, -1)
    xt = x.reshape(B * S, D)
    for l in range(depth):
        wkv = jnp.concatenate([_dup_halves(w_kv_up_a[l][:, :HEAD_DIM]),
                               _dup_halves(w_kv_up_a[l][:, HEAD_DIM:])], axis=1).astype(bf16)
        pr = _proj_call(xt, row(g_pre_mix[l]), _layout_w_in(w_in[l]), row(g_kv_a[l]), wkv, rope_h, rope_i)
        o_a = _dsa_call(pr, B, S)
        kc, vc = _compress_call(pr["kvcmp"], pe_cmp_k[l], pe_cmp_v[l], w_cmp_k[l], w_cmp_v[l], B, S)
        o_b = _nsa_call(pr, kc, vc, B, S)
        o_c = _pool_call(pr["uc"], w_pool[l], B, S)
        xt = _merge_mlp_call(xt, o_a, o_b, o_c, row(g_out_a[l]), row(g_out_b[l]), row(pool_scale[l]),
                             w_out[l].astype(bf16), row(g_post_mix[l]), row(g_pre_mlp[l]),
                             w_up[l].astype(bf16), w_down[l].astype(bf16), row(g_post_mlp[l]))
    return xt.reshape(B, S, D)
```

```python
import functools

import jax
import jax.numpy as jnp
from jax import lax
from jax.experimental import pallas as pl
from jax.experimental.pallas import tpu as pltpu

f32 = jnp.float32
bf16 = jnp.bfloat16

HEAD_DIM = 64
N_HEADS_A = 4
N_HEADS_B = 8
N_KV_GROUPS_B = 2
N_POOL_GROUPS = 4
POOL_GROUP_DIM = 64
POOL_WINDOWS = (2, 4, 8, 16)
KV_RANK_A = 128
N_IDX_HEADS = 4
IDX_DIM = 32
TOPK_MAX = 256
CMP_BLOCK = 32
CMP_STRIDE = 16
SLC_BLOCK = 64
N_SLC = 16
WINDOW = 512
ROPE_THETA = 500000.0
ROT_DIM_HEAD = HEAD_DIM // 4
ROT_DIM_IDX = IDX_DIM // 4
EPS = 1e-6
NEG = -1e30
SCALE = HEAD_DIM ** -0.5

LANES = 128
TOKEN_TILE = 512
QUERY_TILE = 256
KEY_CHUNK = 256
VMEM_LIMIT = 56 * 1024 * 1024

_COLS = {}
_off = 0
for _name, _w in (("qa", 256), ("ckv", 128), ("qidx", 128), ("kidx", 128), ("qb", 512),
                  ("kvcmp", 256), ("kslc", 256), ("vslc", 256), ("kwin", 256), ("vwin", 256),
                  ("misc", 128), ("uc", 256)):
    _COLS[_name] = (_off, _off + _w)
    _off += _w
N_PROJ = _off
GATE_LANE0 = N_IDX_HEADS


def _rms(v, gain=None):
    y = v * lax.rsqrt(jnp.mean(v * v, axis=-1, keepdims=True) + EPS)
    return y if gain is None else y * gain


def _rope128(v, c, sa, sb, half):
    return v * c + pltpu.roll(v, LANES - half, 1) * sa + pltpu.roll(v, half, 1) * sb


def _nt_dot(a, b):
    return lax.dot_general(a, b, (((1,), (1,)), ((), ())), preferred_element_type=f32)


def _proj_kernel(x_ref, g_ref, w_ref, gkv_ref, wkv_ref, rh_ref, ri_ref,
                 qa_ref, ka_ref, vat_ref, qidx_ref, kidx_ref, qbn_ref, qbr_ref,
                 kvcmp_ref, kslc_ref, vslc_ref, kwin_ref, vwin_ref, misc_ref, uc_ref):
    h = _rms(x_ref[...], g_ref[...])
    p = jnp.dot(h.astype(bf16), w_ref[...], preferred_element_type=f32)
    hc, hsa, hsb = rh_ref[:, 0:128], rh_ref[:, 128:256], rh_ref[:, 256:384]
    ic, isa, isb = ri_ref[:, 0:128], ri_ref[:, 128:256], ri_ref[:, 256:384]
    hh, ih = ROT_DIM_HEAD // 2, ROT_DIM_IDX // 2

    def col(name, j=0):
        a = _COLS[name][0] + j * LANES
        return p[:, a:a + LANES]

    for j in range(2):
        qa_ref[:, j * LANES:(j + 1) * LANES] = (_rope128(col("qa", j), hc, hsa, hsb, hh) * SCALE).astype(bf16)
    kv = jnp.dot(_rms(col("ckv"), gkv_ref[...]).astype(bf16), wkv_ref[...], preferred_element_type=f32)
    ka_ref[...] = _rope128(kv[:, 0:LANES], hc, hsa, hsb, hh).astype(bf16)
    for cc in range(vat_ref.shape[0]):
        vt = kv[cc * KEY_CHUNK:(cc + 1) * KEY_CHUNK, LANES:2 * LANES].T
        vat_ref[cc] = vt[0:HEAD_DIM, :].astype(bf16)
    qidx_ref[...] = _rope128(col("qidx"), ic, isa, isb, ih).astype(bf16)
    kidx_ref[...] = _rope128(col("kidx"), ic, isa, isb, ih).astype(bf16)
    for j in range(4):
        q = col("qb", j)
        qbn_ref[:, j * LANES:(j + 1) * LANES] = (q * SCALE).astype(bf16)
        qbr_ref[:, j * LANES:(j + 1) * LANES] = (_rope128(q, hc, hsa, hsb, hh) * SCALE).astype(bf16)
    for j in range(2):
        sl = slice(j * LANES, (j + 1) * LANES)
        kvcmp_ref[:, sl] = col("kvcmp", j)
        kslc_ref[:, sl] = _rope128(col("kslc", j), hc, hsa, hsb, hh).astype(bf16)
        vslc_ref[:, sl] = col("vslc", j).astype(bf16)
        kwin_ref[:, sl] = _rope128(col("kwin", j), hc, hsa, hsb, hh).astype(bf16)
        vwin_ref[:, sl] = col("vwin", j).astype(bf16)
        uc_ref[:, sl] = col("uc", j)
    misc_ref[...] = col("misc")


def _proj_call(x, g, w, gkv, wkv, rope_h, rope_i):
    T, D = x.shape
    TM = TOKEN_TILE
    row = lambda w_: pl.BlockSpec((TM, w_), lambda i: (i, 0))
    full = lambda a: pl.BlockSpec(a.shape, lambda i: (0, 0))
    outs = [("qa", 256, bf16), ("ka", 128, bf16), ("vat", None, bf16), ("qidx", 128, bf16),
            ("kidx", 128, bf16), ("qbn", 512, bf16), ("qbr", 512, bf16), ("kvcmp", 256, f32),
            ("kslc", 256, bf16), ("vslc", 256, bf16), ("kwin", 256, bf16), ("vwin", 256, bf16),
            ("misc", 128, f32), ("uc", 256, f32)]
    vat_spec = pl.BlockSpec((TM // KEY_CHUNK, HEAD_DIM, KEY_CHUNK), lambda i: (i, 0, 0))
    vat_shape = jax.ShapeDtypeStruct((T // KEY_CHUNK, HEAD_DIM, KEY_CHUNK), bf16)
    res = pl.pallas_call(
        _proj_kernel,
        grid=(T // TM,),
        in_specs=[row(D), full(g), full(w), full(gkv), full(wkv), row(384), row(384)],
        out_specs=[vat_spec if w_ is None else row(w_) for _, w_, _ in outs],
        out_shape=[vat_shape if w_ is None else jax.ShapeDtypeStruct((T, w_), dt) for _, w_, dt in outs],
        compiler_params=pltpu.CompilerParams(dimension_semantics=("arbitrary",),
                                             vmem_limit_bytes=VMEM_LIMIT),
        name="proj",
    )(x, g, w, gkv, wkv, rope_h, rope_i)
    return dict(zip([n for n, _, _ in outs], res))


KEY_OF_ONE_ULP = 1
KEY_OF_INF = 0x7F800000
KEY_OF_NEG_MAX = -(2 ** 31 - 2 ** 23)
COUNT_SLAB = 32
SEARCH_STEPS_PER_CHECK = 4


def _ordered_key(v):
    bits = lax.bitcast_convert_type(v, jnp.int32)
    return bits ^ ((bits >> 31) & 0x7FFFFFFF)


def _dsa_kernel(qa_ref, qidx_ref, misc_ref, kidx_ref, ka_ref, vat_ref, o_ref,
                key_ref, s_ref, acc_ref, *, n_keep):
    TQ = qa_ref.shape[0]
    CH = KEY_CHUNK
    q0 = pl.program_id(1) * TQ
    n_ch = (q0 + TQ + CH - 1) // CH
    tpos = q0 + lax.broadcasted_iota(jnp.int32, (1, TQ), 1)
    lane = lax.broadcasted_iota(jnp.int32, (1, LANES), 1)
    k = float(n_keep)

    SUB = LANES
    n_sub = n_ch * (CH // SUB)

    def rows(c):
        return pl.ds(pl.multiple_of(c * CH, CH), CH)

    def sub_rows(i):
        return pl.ds(pl.multiple_of(i * SUB, SUB), SUB)

    w_t = misc_ref[...].T
    qidx = qidx_ref[...]
    q_heads = [jnp.where(lane // IDX_DIM == h, qidx, jnp.zeros_like(qidx)) for h in range(N_IDX_HEADS)]

    def score_chunk(c, carry):
        for i in (2 * c, 2 * c + 1):
            kb = kidx_ref[sub_rows(i), :]
            sc = jnp.zeros((SUB, TQ), f32)
            for h in range(N_IDX_HEADS):
                sc = sc + jnp.maximum(_nt_dot(kb, q_heads[h]), 0.0) * w_t[h:h + 1, :]
            kpos = i * SUB + lax.broadcasted_iota(jnp.int32, (SUB, 1), 0)
            sc = jnp.where(sc == 0.0, 0.0, sc)
            key_ref[sub_rows(i), :] = _ordered_key(jnp.where(kpos <= tpos, sc, -jnp.inf))
        return carry

    lax.fori_loop(0, n_ch, score_chunk, 0)

    def count_ge(v):
        vs = jnp.broadcast_to(v, (COUNT_SLAB, TQ))

        def chunk(c, acc):
            base = pl.multiple_of(c * CH, CH)
            for r in range(CH // COUNT_SLAB):
                ks = key_ref[pl.ds(base + r * COUNT_SLAB, COUNT_SLAB), :]
                acc = acc + jnp.where(ks >= vs, 1, 0)
            return acc

        acc = lax.fori_loop(0, n_ch, chunk, jnp.zeros((COUNT_SLAB, TQ), jnp.int32))
        return jnp.sum(acc.astype(f32), axis=0, keepdims=True)

    need = (tpos + 1) > n_keep
    c0 = count_ge(jnp.zeros((1, TQ), jnp.int32))
    c1 = count_ge(jnp.full((1, TQ), KEY_OF_ONE_ULP, jnp.int32))
    positive = c1 >= k
    settled = ~need | ((c0 >= k) & (c1 < k)) | (c1 == k)
    thr = jnp.where(need, jnp.where(c1 == k, KEY_OF_ONE_ULP, 0), KEY_OF_NEG_MAX)
    lo = jnp.where(positive, KEY_OF_ONE_ULP, KEY_OF_NEG_MAX)
    hi = jnp.where(positive, KEY_OF_INF, 0)
    active = jnp.where(settled, 0, 1)

    def search_cond(st):
        return st[4] > 0

    def search_body(st):
        lo, hi, thr, active, _ = st
        for _ in range(SEARCH_STEPS_PER_CHECK):
            act = active > 0
            mid = lo + ((hi - lo) >> 1)
            collapsed = (hi - lo) <= 1
            c = count_ge(mid)
            found = c == k
            thr = jnp.where(act & found, mid, jnp.where(act & collapsed, lo, thr))
            go = act & ~found & ~collapsed
            lo = jnp.where(go & (c > k), mid, lo)
            hi = jnp.where(go & (c < k), mid, hi)
            active = jnp.where(go, 1, 0)
        return lo, hi, thr, active, jnp.sum(active)

    _, _, thr, _, _ = lax.while_loop(search_cond, search_body, (lo, hi, thr, active, jnp.sum(active)))
    needed = k - count_ge(thr + 1)

    ri = lax.broadcasted_iota(jnp.int32, (SUB, SUB), 0)
    rj = lax.broadcasted_iota(jnp.int32, (SUB, SUB), 1)
    strict_lower = jnp.where(rj < ri, 1.0, 0.0).astype(bf16)
    q_att = []
    for h in range(N_HEADS_A):
        qc = qa_ref[:, (h // 2) * LANES:(h // 2 + 1) * LANES]
        q_att.append(jnp.where(lane // HEAD_DIM == h % 2, qc, jnp.zeros_like(qc)))

    def logits_chunk(c, st):
        ties_before, maxes = st
        maxes = list(maxes)
        for i in (2 * c, 2 * c + 1):
            keys = key_ref[sub_rows(i), :]
            tie = keys == thr
            tie_f = jnp.where(tie, 1.0, 0.0)
            rank = jnp.dot(strict_lower, tie_f.astype(bf16), preferred_element_type=f32) + ties_before
            sel = (keys > thr) | (tie & (rank < needed))
            bias = jnp.where(sel, 0.0, NEG)
            kb = ka_ref[sub_rows(i), :]
            for h in range(N_HEADS_A):
                s = _nt_dot(kb, q_att[h]) + bias
                s_ref[h, sub_rows(i), :] = s
                maxes[h] = jnp.maximum(maxes[h], jnp.max(s, axis=0, keepdims=True))
            ties_before = ties_before + jnp.sum(tie_f, axis=0, keepdims=True)
        return ties_before, tuple(maxes)

    _, maxes = lax.fori_loop(0, n_ch, logits_chunk,
                             (jnp.zeros((1, TQ), f32), (jnp.full((1, TQ), -3.0e38, f32),) * N_HEADS_A))

    acc_ref[...] = jnp.zeros_like(acc_ref)

    def value_chunk(c, sums):
        new_sums = []
        for h in range(N_HEADS_A):
            e = jnp.exp(s_ref[h, rows(c), :] - maxes[h])
            acc_ref[h] += jnp.dot(vat_ref[c], e.astype(bf16), preferred_element_type=f32)
            new_sums.append(sums[h] + jnp.sum(e, axis=0, keepdims=True))
        return tuple(new_sums)

    sums = lax.fori_loop(0, n_ch, value_chunk, (jnp.zeros((1, TQ), f32),) * N_HEADS_A)
    outs = [acc_ref[h] / sums[h] for h in range(N_HEADS_A)]
    o_ref[...] = jnp.concatenate(outs, axis=0).T


def _dsa_call(pr, B, S):
    TQ = min(QUERY_TILE, S)
    n_keep = min(TOPK_MAX, S // 4)
    n_chunks = S // KEY_CHUNK
    qspec = lambda w_: pl.BlockSpec((TQ, w_), lambda b, i: (b * (S // TQ) + i, 0))
    kspec = lambda w_: pl.BlockSpec((S, w_), lambda b, i: (b, 0))
    vspec = pl.BlockSpec((n_chunks, HEAD_DIM, KEY_CHUNK), lambda b, i: (b, 0, 0))
    return pl.pallas_call(
        functools.partial(_dsa_kernel, n_keep=n_keep),
        grid=(B, S // TQ),
        in_specs=[qspec(256), qspec(128), qspec(128), kspec(128), kspec(128), vspec],
        out_specs=qspec(256),
        out_shape=jax.ShapeDtypeStruct((B * S, 256), f32),
        scratch_shapes=[pltpu.VMEM((S, TQ), jnp.int32), pltpu.VMEM((N_HEADS_A, S, TQ), f32),
                        pltpu.VMEM((N_HEADS_A, HEAD_DIM, TQ), f32)],
        compiler_params=pltpu.CompilerParams(dimension_semantics=("arbitrary", "arbitrary"),
                                             vmem_limit_bytes=VMEM_LIMIT),
        name="dsa",
    )(pr["qa"], pr["qidx"], pr["misc"], pr["kidx"], pr["ka"], pr["vat"])


def _compress_kernel(ch_ref, pe_ref, w_ref, kc_ref, vc_ref):
    nch = ch_ref.shape[1]
    rowid = lax.broadcasted_iota(jnp.int32, (nch, 1), 0)
    for kv, o_ref in ((0, kc_ref), (1, vc_ref)):
        for g in range(N_KV_GROUPS_B):
            ch = ch_ref[kv * N_KV_GROUPS_B + g]
            top = jnp.dot((ch + pe_ref[kv, 0:1, :]).astype(bf16), w_ref[kv, 0], preferred_element_type=f32)
            bot = jnp.dot((ch + pe_ref[kv, 1:2, :]).astype(bf16), w_ref[kv, 1], preferred_element_type=f32)
            blk = top + pltpu.roll(bot, nch - 1, 0)
            o_ref[:, g * LANES:(g + 1) * LANES] = jnp.where(rowid < nch - 1, blk, 0.0).astype(bf16)


def _compress_call(kvcmp, pe_k, pe_v, w_k, w_v, B, S):
    nch = S // CMP_STRIDE
    G, dh = N_KV_GROUPS_B, HEAD_DIM
    ch = kvcmp.reshape(B, nch, CMP_STRIDE, 2 * G, dh).transpose(0, 3, 1, 2, 4)
    ch = ch.reshape(B, 2 * G, nch, CMP_STRIDE * dh)
    pe = jnp.stack([pe_k, pe_v]).reshape(2, 2, CMP_STRIDE * dh)
    w = jnp.stack([w_k, w_v]).reshape(2, 2, CMP_STRIDE * dh, dh)
    w = jnp.concatenate([w, w], axis=-1).astype(bf16)
    out = jax.ShapeDtypeStruct((B * nch, G * LANES), bf16)
    ospec = pl.BlockSpec((nch, G * LANES), lambda b: (b, 0))
    return pl.pallas_call(
        _compress_kernel,
        grid=(B,),
        in_specs=[pl.BlockSpec((None, 2 * G, nch, CMP_STRIDE * dh), lambda b: (b, 0, 0, 0)),
                  pl.BlockSpec(pe.shape, lambda b: (0, 0, 0)),
                  pl.BlockSpec(w.shape, lambda b: (0, 0, 0, 0))],
        out_specs=[ospec, ospec],
        out_shape=[out, out],
        compiler_params=pltpu.CompilerParams(dimension_semantics=("arbitrary",),
                                             vmem_limit_bytes=VMEM_LIMIT),
        name="compress",
    )(ch, pe, w)


def _split3(v):
    a = v.astype(bf16)
    r = v - a.astype(f32)
    b = r.astype(bf16)
    c = (r - b.astype(f32)).astype(bf16)
    return a, b, c


def _masked_attention(q, k, v, mask):
    s = jnp.where(mask, _nt_dot(q, k), NEG)
    e = jnp.exp(s - jnp.max(s, axis=1, keepdims=True))
    o = jnp.dot(e.astype(bf16), v, preferred_element_type=f32)
    return o / jnp.sum(e, axis=1, keepdims=True)


def _nsa_kernel(qn_ref, qr_ref, misc_ref, kc_ref, vc_ref, kslc_ref, vslc_ref, kwin_ref, vwin_ref,
                o_ref, *, n_sel, win_len):
    TQ = qn_ref.shape[0]
    S = kslc_ref.shape[0]
    nch = kc_ref.shape[0]
    n_cmp = nch - 1
    n_blk = S // SLC_BLOCK
    G = N_KV_GROUPS_B
    R = N_HEADS_B // G
    q0 = pl.program_id(1) * TQ
    tpos = q0 + lax.broadcasted_iota(jnp.int32, (TQ, 1), 0)
    lane = lax.broadcasted_iota(jnp.int32, (1, LANES), 1)
    kpos = lax.broadcasted_iota(jnp.int32, (1, S), 1)
    causal = kpos <= tpos

    gate = jax.nn.sigmoid(misc_ref[...])

    def gate_of(h, branch):
        a = GATE_LANE0 + 3 * h + branch
        return gate[:, a:a + 1]

    cpos = lax.broadcasted_iota(jnp.int32, (1, nch), 1)
    cmask = (cpos * CMP_STRIDE + CMP_BLOCK - 1 <= tpos) & (cpos < n_cmp)
    oc = lax.broadcasted_iota(jnp.int32, (nch, LANES), 0)
    oj = lax.broadcasted_iota(jnp.int32, (nch, LANES), 1)
    overlap = ((oc * CMP_STRIDE <= oj * SLC_BLOCK + SLC_BLOCK - 1)
               & (oc * CMP_STRIDE + CMP_BLOCK - 1 >= oj * SLC_BLOCK)
               & (oc < n_cmp) & (oj < n_blk))
    overlap = jnp.where(overlap, 1.0, 0.0).astype(bf16)
    ej = lax.broadcasted_iota(jnp.int32, (LANES, S), 0)
    es = lax.broadcasted_iota(jnp.int32, (LANES, S), 1)
    expand = jnp.where(es // SLC_BLOCK == ej, 1.0, 0.0).astype(bf16)
    cur = tpos // SLC_BLOCK
    forced = (lane == 0) | (lane == cur) | (lane == cur - 1)
    admissible = lane * SLC_BLOCK <= tpos
    kstart = pl.multiple_of(jnp.clip(q0 - WINDOW, 0, S - win_len), LANES)
    wpos = kstart + lax.broadcasted_iota(jnp.int32, (1, win_len), 1)
    wmask = (wpos <= tpos) & (wpos > tpos - WINDOW)

    acc = [jnp.zeros((TQ, LANES), f32) for _ in range(N_HEADS_B * HEAD_DIM // LANES)]
    for g in range(G):
        gs = slice(g * LANES, (g + 1) * LANES)
        kc, vc = kc_ref[:, gs], vc_ref[:, gs]
        ks, vs = kslc_ref[:, gs], vslc_ref[:, gs]
        kw, vw = kwin_ref[pl.ds(kstart, win_len), gs], vwin_ref[pl.ds(kstart, win_len), gs]
        heads = []
        psum = jnp.zeros((TQ, nch), f32)
        for r in range(R):
            h = g * R + r
            j = h // 2
            in_head = lane // HEAD_DIM == h % 2
            qn = qn_ref[:, j * LANES:(j + 1) * LANES]
            qn = jnp.where(in_head, qn, jnp.zeros_like(qn))
            lc = jnp.where(cmask, _nt_dot(qn, kc), NEG)
            e = jnp.where(cmask, jnp.exp(lc - jnp.max(lc, axis=1, keepdims=True)), 0.0)
            l = jnp.sum(e, axis=1, keepdims=True)
            p = e / jnp.where(l > 0.0, l, 1.0)
            psum = psum + p
            o_cmp = jnp.dot(p.astype(bf16), vc, preferred_element_type=f32)
            heads.append((h, j, in_head, gate_of(h, 0) * o_cmp))
        s_blk = sum(jnp.dot(piece, overlap, preferred_element_type=f32) for piece in _split3(psum))
        s_blk = jnp.where(forced, jnp.inf, jnp.where(admissible, s_blk, -jnp.inf))
        s_blk = jnp.where(lane < n_blk, s_blk, -jnp.inf)
        rank = jnp.zeros((TQ, LANES), f32)
        for i in range(n_blk):
            col = s_blk[:, i:i + 1]
            beats = (col > s_blk) | ((col == s_blk) & (lane > i))
            rank = rank + jnp.where(beats, 1.0, 0.0)
        chosen = jnp.where((rank < float(n_sel)) & (lane < n_blk), 1.0, 0.0).astype(bf16)
        smask = (jnp.dot(chosen, expand, preferred_element_type=f32) > 0.5) & causal
        for h, j, in_head, o_gated in heads:
            qr = qr_ref[:, j * LANES:(j + 1) * LANES]
            qr = jnp.where(in_head, qr, jnp.zeros_like(qr))
            o_gated = o_gated + gate_of(h, 1) * _masked_attention(qr, ks, vs, smask)
            o_gated = o_gated + gate_of(h, 2) * _masked_attention(qr, kw, vw, wmask)
            acc[j] = acc[j] + jnp.where(in_head, o_gated, 0.0)
    for j, a in enumerate(acc):
        o_ref[:, j * LANES:(j + 1) * LANES] = a


def _nsa_call(pr, kc, vc, B, S):
    TQ = min(QUERY_TILE, S)
    n_sel = min(N_SLC, S // SLC_BLOCK)
    win_len = min(WINDOW + TQ, S)
    nch = S // CMP_STRIDE
    qspec = lambda w_: pl.BlockSpec((TQ, w_), lambda b, i: (b * (S // TQ) + i, 0))
    kspec = lambda w_: pl.BlockSpec((S, w_), lambda b, i: (b, 0))
    cspec = pl.BlockSpec((nch, 256), lambda b, i: (b, 0))
    return pl.pallas_call(
        functools.partial(_nsa_kernel, n_sel=n_sel, win_len=win_len),
        grid=(B, S // TQ),
        in_specs=[qspec(512), qspec(512), qspec(128), cspec, cspec,
                  kspec(256), kspec(256), kspec(256), kspec(256)],
        out_specs=qspec(512),
        out_shape=jax.ShapeDtypeStruct((B * S, 512), f32),
        compiler_params=pltpu.CompilerParams(dimension_semantics=("arbitrary", "arbitrary"),
                                             vmem_limit_bytes=VMEM_LIMIT),
        name="nsa",
    )(pr["qbn"], pr["qbr"], pr["misc"], kc, vc, pr["kslc"], pr["vslc"], pr["kwin"], pr["vwin"])


def _pool_kernel(u_ref, w_ref, o_ref):
    u = u_ref[...]
    S, C = u.shape
    row = lax.broadcasted_iota(jnp.int32, (S, 1), 0)
    grp = lax.broadcasted_iota(jnp.int32, (1, C), 1) // POOL_GROUP_DIM
    run, width = u, 1
    win_sum = jnp.zeros_like(u)
    win_len = jnp.zeros((1, C), f32)
    for gi, w in enumerate(POOL_WINDOWS):
        while width < w:
            run = run + jnp.where(row >= width, pltpu.roll(run, width, 0), 0.0)
            width *= 2
        assert width == w, "pooling windows must be increasing powers of two"
        win_sum = jnp.where(grp == gi, run, win_sum)
        win_len = jnp.where(grp == gi, float(w), win_len)
    pooled = win_sum / jnp.minimum((row + 1).astype(f32), win_len) - u
    o_ref[...] = jnp.dot(pooled.astype(bf16), w_ref[...], preferred_element_type=f32)


def _pool_call(uc, w_pool, B, S):
    C = N_POOL_GROUPS * POOL_GROUP_DIM
    wbd = jnp.zeros((C, C), f32)
    for gi in range(N_POOL_GROUPS):
        sl = slice(gi * POOL_GROUP_DIM, (gi + 1) * POOL_GROUP_DIM)
        wbd = wbd.at[sl, sl].set(w_pool[gi])
    spec = pl.BlockSpec((S, C), lambda b: (b, 0))
    return pl.pallas_call(
        _pool_kernel,
        grid=(B,),
        in_specs=[spec, pl.BlockSpec((C, C), lambda b: (0, 0))],
        out_specs=spec,
        out_shape=jax.ShapeDtypeStruct((B * S, C), f32),
        compiler_params=pltpu.CompilerParams(dimension_semantics=("arbitrary",),
                                             vmem_limit_bytes=VMEM_LIMIT),
        name="pool",
    )(uc, wbd.astype(bf16))


def _merge_mlp_kernel(x_ref, oa_ref, ob_ref, oc_ref, ga_ref, gb_ref, ps_ref, wo_ref, gpm_ref,
                      gmlp_ref, wup_ref, wdn_ref, gpost_ref, o_ref):
    mixed = jnp.concatenate([_rms(oa_ref[...], ga_ref[...]), _rms(ob_ref[...], gb_ref[...]),
                             _rms(oc_ref[...]) * ps_ref[...]], axis=1).astype(bf16)
    x = x_ref[...] + _rms(jnp.dot(mixed, wo_ref[...], preferred_element_type=f32), gpm_ref[...])
    h = _rms(x, gmlp_ref[...]).astype(bf16)
    d_ff = wup_ref.shape[1]
    f = jnp.zeros_like(x)
    for c in range(d_ff // x.shape[1]):
        sl = slice(c * x.shape[1], (c + 1) * x.shape[1])
        up = jnp.maximum(jnp.dot(h, wup_ref[:, sl], preferred_element_type=f32), 0.0)
        f = f + jnp.dot((up * up).astype(bf16), wdn_ref[sl, :], preferred_element_type=f32)
    o_ref[...] = x + _rms(f, gpost_ref[...])


def _merge_mlp_call(x, oa, ob, oc, ga, gb, ps, wo, gpm, gmlp, wup, wdn, gpost):
    T, D = x.shape
    TM = TOKEN_TILE
    row = lambda a: pl.BlockSpec((TM, a.shape[1]), lambda i: (i, 0))
    full = lambda a: pl.BlockSpec(a.shape, lambda i: (0, 0))
    args = (x, oa, ob, oc, ga, gb, ps, wo, gpm, gmlp, wup, wdn, gpost)
    return pl.pallas_call(
        _merge_mlp_kernel,
        grid=(T // TM,),
        in_specs=[row(a) for a in args[:4]] + [full(a) for a in args[4:]],
        out_specs=pl.BlockSpec((TM, D), lambda i: (i, 0)),
        out_shape=jax.ShapeDtypeStruct((T, D), f32),
        compiler_params=pltpu.CompilerParams(dimension_semantics=("arbitrary",),
                                             vmem_limit_bytes=VMEM_LIMIT),
        name="merge_mlp",
    )(*args)


def _rope_lane_tables(positions, rot_dim, period):
    half = rot_dim // 2
    inv_freq = ROPE_THETA ** (-jnp.arange(0, rot_dim, 2, dtype=f32) / rot_dim)
    ang = positions.astype(f32).reshape(-1, 1) * inv_freq
    cos, sin = jnp.cos(ang), jnp.sin(ang)
    T = ang.shape[0]
    ones, zeros, zh = jnp.ones((T, period - rot_dim), f32), jnp.zeros((T, period - rot_dim), f32), jnp.zeros((T, half), f32)
    c = jnp.concatenate([cos, cos, ones], axis=1)
    sa = jnp.concatenate([-sin, zh, zeros], axis=1)
    sb = jnp.concatenate([zh, sin, zeros], axis=1)
    reps = LANES // period
    return jnp.concatenate([jnp.tile(t, (1, reps)) for t in (c, sa, sb)], axis=1)


def _dup_halves(w):
    D = w.shape[0]
    g = w.reshape(D, -1, 1, HEAD_DIM)
    return jnp.broadcast_to(g, (D, g.shape[1], 2, HEAD_DIM)).reshape(D, -1)


def _layout_w_in(w_in):
    D = w_in.shape[0]
    widths = (256, 128, 128, 32, 4, 512, 128, 128, 128, 128, 128, 128, 24, 256)
    offs = [0]
    for w_ in widths:
        offs.append(offs[-1] + w_)
    (q_a, ckv, q_idx, k_idx, w_idx, q_b, k_cmp, v_cmp, k_slc, v_slc, k_win, v_win, gates, u_c) = [
        w_in[:, offs[i]:offs[i + 1]] for i in range(len(widths))]
    misc = jnp.concatenate([w_idx, gates, jnp.zeros((D, LANES - 28), w_in.dtype)], axis=1)
    cols = [q_a, ckv, q_idx, jnp.tile(k_idx, (1, LANES // IDX_DIM)), q_b, k_cmp, v_cmp,
            _dup_halves(k_slc), _dup_halves(v_slc), _dup_halves(k_win), _dup_halves(v_win), misc, u_c]
    w = jnp.concatenate(cols, axis=1)
    assert w.shape[1] == N_PROJ
    return w.astype(bf16)


def kernel(x, positions, w_in, g_kv_a, w_kv_up_a, w_cmp_k, w_cmp_v, pe_cmp_k, pe_cmp_v, w_pool,
           pool_scale, g_out_a, g_out_b, w_out, g_pre_mix, g_post_mix, g_pre_mlp, g_post_mlp,
           w_up, w_down):
    B, S, D = x.shape
    depth = w_in.shape[0]
    assert S % QUERY_TILE == 0 or S < QUERY_TILE
    assert (B * S) % TOKEN_TILE == 0
    rope_h = _rope_lane_tables(positions, ROT_DIM_HEAD, HEAD_DIM)
    rope_i = _rope_lane_tables(positions, ROT_DIM_IDX, IDX_DIM)
    row = lambda v: v.reshape(1, -1)
    xt = x.reshape(B * S, D)
    for l in range(depth):
        wkv = jnp.concatenate([_dup_halves(w_kv_up_a[l][:, :HEAD_DIM]), w_kv_up_a[l][:, HEAD_DIM:],
                               jnp.zeros((KV_RANK_A, LANES - HEAD_DIM), f32)], axis=1).astype(bf16)
        pr = _proj_call(xt, row(g_pre_mix[l]), _layout_w_in(w_in[l]), row(g_kv_a[l]), wkv, rope_h, rope_i)
        o_a = _dsa_call(pr, B, S)
        kc, vc = _compress_call(pr["kvcmp"], pe_cmp_k[l], pe_cmp_v[l], w_cmp_k[l], w_cmp_v[l], B, S)
        o_b = _nsa_call(pr, kc, vc, B, S)
        o_c = _pool_call(pr["uc"], w_pool[l], B, S)
        xt = _merge_mlp_call(xt, o_a, o_b, o_c, row(g_out_a[l]), row(g_out_b[l]), row(pool_scale[l]),
                             w_out[l].astype(bf16), row(g_post_mix[l]), row(g_pre_mlp[l]),
                             w_up[l].astype(bf16), w_down[l].astype(bf16), row(g_post_mlp[l]))
    return xt.reshape(B, S, D)
```

```python
import functools

import jax
import jax.numpy as jnp
from jax import lax
from jax.experimental import pallas as pl
from jax.experimental.pallas import tpu as pltpu

f32 = jnp.float32
bf16 = jnp.bfloat16

HEAD_DIM = 64
N_HEADS_A = 4
N_HEADS_B = 8
N_KV_GROUPS_B = 2
N_POOL_GROUPS = 4
POOL_GROUP_DIM = 64
POOL_WINDOWS = (2, 4, 8, 16)
KV_RANK_A = 128
N_IDX_HEADS = 4
IDX_DIM = 32
TOPK_MAX = 256
CMP_BLOCK = 32
CMP_STRIDE = 16
SLC_BLOCK = 64
N_SLC = 16
WINDOW = 512
ROPE_THETA = 500000.0
ROT_DIM_HEAD = HEAD_DIM // 4
ROT_DIM_IDX = IDX_DIM // 4
EPS = 1e-6
NEG = -1e30
SCALE = HEAD_DIM ** -0.5

LANES = 128
TOKEN_TILE = 512
QUERY_TILE = 256
KEY_CHUNK = 256
VMEM_LIMIT = 56 * 1024 * 1024

_COLS = {}
_off = 0
for _name, _w in (("qa", 256), ("ckv", 128), ("qidx", 128), ("kidx", 128), ("qb", 512),
                  ("kvcmp", 256), ("kslc", 256), ("vslc", 128), ("kwin", 256), ("vwin", 128),
                  ("misc", 128), ("uc", 256)):
    _COLS[_name] = (_off, _off + _w)
    _off += _w
N_PROJ = _off
GATE_LANE0 = N_IDX_HEADS


def _rms(v, gain=None):
    y = v * lax.rsqrt(jnp.mean(v * v, axis=-1, keepdims=True) + EPS)
    return y if gain is None else y * gain


def _rope128(v, c, sa, sb, half):
    return v * c + pltpu.roll(v, LANES - half, 1) * sa + pltpu.roll(v, half, 1) * sb


def _nt_dot(a, b):
    return lax.dot_general(a, b, (((1,), (1,)), ((), ())), preferred_element_type=f32)


def _proj_kernel(x_ref, g_ref, w_ref, gkv_ref, wkv_ref, rh_ref, ri_ref,
                 qa_ref, ka_ref, vat_ref, qidx_ref, kidx_ref, qbn_ref, qbr_ref,
                 kvcmp_ref, kslc_ref, vslct_ref, kwin_ref, vwint_ref, misc_ref, uc_ref):
    h = _rms(x_ref[...], g_ref[...])
    p = jnp.dot(h.astype(bf16), w_ref[...], preferred_element_type=f32)
    hc, hsa, hsb = rh_ref[:, 0:128], rh_ref[:, 128:256], rh_ref[:, 256:384]
    ic, isa, isb = ri_ref[:, 0:128], ri_ref[:, 128:256], ri_ref[:, 256:384]
    hh, ih = ROT_DIM_HEAD // 2, ROT_DIM_IDX // 2

    def col(name, j=0):
        a = _COLS[name][0] + j * LANES
        return p[:, a:a + LANES]

    for j in range(2):
        qa_ref[:, j * LANES:(j + 1) * LANES] = (_rope128(col("qa", j), hc, hsa, hsb, hh) * SCALE).astype(bf16)
    kv = jnp.dot(_rms(col("ckv"), gkv_ref[...]).astype(bf16), wkv_ref[...], preferred_element_type=f32)
    ka_ref[...] = _rope128(kv[:, 0:LANES], hc, hsa, hsb, hh).astype(bf16)
    for cc in range(vat_ref.shape[0]):
        tok = slice(cc * KEY_CHUNK, (cc + 1) * KEY_CHUNK)
        vat_ref[cc] = kv[tok, LANES:2 * LANES].T[0:HEAD_DIM, :].astype(bf16)
        vslct_ref[cc] = col("vslc")[tok, :].T.astype(bf16)
        vwint_ref[cc] = col("vwin")[tok, :].T.astype(bf16)
    qidx_ref[...] = _rope128(col("qidx"), ic, isa, isb, ih).astype(bf16)
    kidx_ref[...] = _rope128(col("kidx"), ic, isa, isb, ih).astype(bf16)
    for j in range(4):
        q = col("qb", j)
        qbn_ref[:, j * LANES:(j + 1) * LANES] = (q * SCALE).astype(bf16)
        qbr_ref[:, j * LANES:(j + 1) * LANES] = (_rope128(q, hc, hsa, hsb, hh) * SCALE).astype(bf16)
    for j in range(2):
        sl = slice(j * LANES, (j + 1) * LANES)
        kvcmp_ref[:, sl] = col("kvcmp", j)
        kslc_ref[:, sl] = _rope128(col("kslc", j), hc, hsa, hsb, hh).astype(bf16)
        kwin_ref[:, sl] = _rope128(col("kwin", j), hc, hsa, hsb, hh).astype(bf16)
        uc_ref[:, sl] = col("uc", j)
    misc_ref[...] = col("misc")


def _proj_call(x, g, w, gkv, wkv, rope_h, rope_i):
    T, D = x.shape
    TM = TOKEN_TILE
    row = lambda w_: pl.BlockSpec((TM, w_), lambda i: (i, 0))
    full = lambda a: pl.BlockSpec(a.shape, lambda i: (0, 0))
    outs = [("qa", 256, bf16), ("ka", 128, bf16), ("vat", -HEAD_DIM, bf16), ("qidx", 128, bf16),
            ("kidx", 128, bf16), ("qbn", 512, bf16), ("qbr", 512, bf16), ("kvcmp", 256, f32),
            ("kslc", 256, bf16), ("vslct", -LANES, bf16), ("kwin", 256, bf16), ("vwint", -LANES, bf16),
            ("misc", 128, f32), ("uc", 256, f32)]
    tspec = lambda d: pl.BlockSpec((TM // KEY_CHUNK, d, KEY_CHUNK), lambda i: (i, 0, 0))
    tshape = lambda d: jax.ShapeDtypeStruct((T // KEY_CHUNK, d, KEY_CHUNK), bf16)
    res = pl.pallas_call(
        _proj_kernel,
        grid=(T // TM,),
        in_specs=[row(D), full(g), full(w), full(gkv), full(wkv), row(384), row(384)],
        out_specs=[tspec(-w_) if w_ < 0 else row(w_) for _, w_, _ in outs],
        out_shape=[tshape(-w_) if w_ < 0 else jax.ShapeDtypeStruct((T, w_), dt) for _, w_, dt in outs],
        compiler_params=pltpu.CompilerParams(dimension_semantics=("arbitrary",),
                                             vmem_limit_bytes=VMEM_LIMIT),
        name="proj",
    )(x, g, w, gkv, wkv, rope_h, rope_i)
    return dict(zip([n for n, _, _ in outs], res))


KEY_OF_ONE_ULP = 1
KEY_OF_INF = 0x7F800000
KEY_OF_NEG_MAX = -(2 ** 31 - 2 ** 23)
COUNT_SLAB = 32
SEARCH_STEPS_PER_CHECK = 4


def _ordered_key(v):
    bits = lax.bitcast_convert_type(v, jnp.int32)
    return bits ^ ((bits >> 31) & 0x7FFFFFFF)


def _dsa_kernel(qa_ref, qidx_ref, misc_ref, kidx_ref, ka_ref, vat_ref, o_ref,
                key_ref, s_ref, acc_ref, *, n_keep):
    TQ = qa_ref.shape[0]
    CH = KEY_CHUNK
    q0 = pl.program_id(1) * TQ
    n_ch = (q0 + TQ + CH - 1) // CH
    tpos = q0 + lax.broadcasted_iota(jnp.int32, (1, TQ), 1)
    lane = lax.broadcasted_iota(jnp.int32, (1, LANES), 1)
    k = float(n_keep)

    SUB = LANES
    n_sub = n_ch * (CH // SUB)

    def rows(c):
        return pl.ds(pl.multiple_of(c * CH, CH), CH)

    def sub_rows(i):
        return pl.ds(pl.multiple_of(i * SUB, SUB), SUB)

    w_t = misc_ref[...].T
    qidx = qidx_ref[...]
    q_heads = [jnp.where(lane // IDX_DIM == h, qidx, jnp.zeros_like(qidx)) for h in range(N_IDX_HEADS)]

    def score_chunk(c, carry):
        for i in (2 * c, 2 * c + 1):
            kb = kidx_ref[sub_rows(i), :]
            sc = jnp.zeros((SUB, TQ), f32)
            for h in range(N_IDX_HEADS):
                sc = sc + jnp.maximum(_nt_dot(kb, q_heads[h]), 0.0) * w_t[h:h + 1, :]
            kpos = i * SUB + lax.broadcasted_iota(jnp.int32, (SUB, 1), 0)
            sc = jnp.where(sc == 0.0, 0.0, sc)
            key_ref[sub_rows(i), :] = _ordered_key(jnp.where(kpos <= tpos, sc, -jnp.inf))
        return carry

    lax.fori_loop(0, n_ch, score_chunk, 0)

    def count_ge(v):
        vs = jnp.broadcast_to(v, (COUNT_SLAB, TQ))

        def chunk(c, acc):
            base = pl.multiple_of(c * CH, CH)
            for r in range(CH // COUNT_SLAB):
                ks = key_ref[pl.ds(base + r * COUNT_SLAB, COUNT_SLAB), :]
                acc = acc + jnp.where(ks >= vs, 1, 0)
            return acc

        acc = lax.fori_loop(0, n_ch, chunk, jnp.zeros((COUNT_SLAB, TQ), jnp.int32))
        return jnp.sum(acc.astype(f32), axis=0, keepdims=True)

    need = (tpos + 1) > n_keep
    c0 = count_ge(jnp.zeros((1, TQ), jnp.int32))
    c1 = count_ge(jnp.full((1, TQ), KEY_OF_ONE_ULP, jnp.int32))
    positive = c1 >= k
    settled = ~need | ((c0 >= k) & (c1 < k)) | (c1 == k)
    thr = jnp.where(need, jnp.where(c1 == k, KEY_OF_ONE_ULP, 0), KEY_OF_NEG_MAX)
    lo = jnp.where(positive, KEY_OF_ONE_ULP, KEY_OF_NEG_MAX)
    hi = jnp.where(positive, KEY_OF_INF, 0)
    active = jnp.where(settled, 0, 1)

    def search_cond(st):
        return st[4] > 0

    def search_body(st):
        lo, hi, thr, active, _ = st
        for _ in range(SEARCH_STEPS_PER_CHECK):
            act = active > 0
            mid = lo + ((hi - lo) >> 1)
            collapsed = (hi - lo) <= 1
            c = count_ge(mid)
            found = c == k
            thr = jnp.where(act & found, mid, jnp.where(act & collapsed, lo, thr))
            go = act & ~found & ~collapsed
            lo = jnp.where(go & (c > k), mid, lo)
            hi = jnp.where(go & (c < k), mid, hi)
            active = jnp.where(go, 1, 0)
        return lo, hi, thr, active, jnp.sum(active)

    _, _, thr, _, _ = lax.while_loop(search_cond, search_body, (lo, hi, thr, active, jnp.sum(active)))
    needed = k - count_ge(thr + 1)

    ri = lax.broadcasted_iota(jnp.int32, (SUB, SUB), 0)
    rj = lax.broadcasted_iota(jnp.int32, (SUB, SUB), 1)
    strict_lower = jnp.where(rj < ri, 1.0, 0.0).astype(bf16)
    q_att = []
    for h in range(N_HEADS_A):
        qc = qa_ref[:, (h // 2) * LANES:(h // 2 + 1) * LANES]
        q_att.append(jnp.where(lane // HEAD_DIM == h % 2, qc, jnp.zeros_like(qc)))

    def logits_chunk(c, st):
        ties_before, maxes = st
        maxes = list(maxes)
        for i in (2 * c, 2 * c + 1):
            keys = key_ref[sub_rows(i), :]
            tie = keys == thr
            tie_f = jnp.where(tie, 1.0, 0.0)
            rank = jnp.dot(strict_lower, tie_f.astype(bf16), preferred_element_type=f32) + ties_before
            sel = (keys > thr) | (tie & (rank < needed))
            bias = jnp.where(sel, 0.0, NEG)
            kb = ka_ref[sub_rows(i), :]
            for h in range(N_HEADS_A):
                s = _nt_dot(kb, q_att[h]) + bias
                s_ref[h, sub_rows(i), :] = s
                maxes[h] = jnp.maximum(maxes[h], jnp.max(s, axis=0, keepdims=True))
            ties_before = ties_before + jnp.sum(tie_f, axis=0, keepdims=True)
        return ties_before, tuple(maxes)

    _, maxes = lax.fori_loop(0, n_ch, logits_chunk,
                             (jnp.zeros((1, TQ), f32), (jnp.full((1, TQ), -3.0e38, f32),) * N_HEADS_A))

    acc_ref[...] = jnp.zeros_like(acc_ref)

    def value_chunk(c, sums):
        new_sums = []
        for h in range(N_HEADS_A):
            e = jnp.exp(s_ref[h, rows(c), :] - maxes[h])
            acc_ref[h] += jnp.dot(vat_ref[c], e.astype(bf16), preferred_element_type=f32)
            new_sums.append(sums[h] + jnp.sum(e, axis=0, keepdims=True))
        return tuple(new_sums)

    sums = lax.fori_loop(0, n_ch, value_chunk, (jnp.zeros((1, TQ), f32),) * N_HEADS_A)
    outs = [acc_ref[h] / sums[h] for h in range(N_HEADS_A)]
    o_ref[...] = jnp.concatenate(outs, axis=0).T


def _dsa_call(pr, B, S):
    TQ = min(QUERY_TILE, S)
    n_keep = min(TOPK_MAX, S // 4)
    n_chunks = S // KEY_CHUNK
    qspec = lambda w_: pl.BlockSpec((TQ, w_), lambda b, i: (b * (S // TQ) + i, 0))
    kspec = lambda w_: pl.BlockSpec((S, w_), lambda b, i: (b, 0))
    vspec = pl.BlockSpec((n_chunks, HEAD_DIM, KEY_CHUNK), lambda b, i: (b, 0, 0))
    return pl.pallas_call(
        functools.partial(_dsa_kernel, n_keep=n_keep),
        grid=(B, S // TQ),
        in_specs=[qspec(256), qspec(128), qspec(128), kspec(128), kspec(128), vspec],
        out_specs=qspec(256),
        out_shape=jax.ShapeDtypeStruct((B * S, 256), f32),
        scratch_shapes=[pltpu.VMEM((S, TQ), jnp.int32), pltpu.VMEM((N_HEADS_A, S, TQ), f32),
                        pltpu.VMEM((N_HEADS_A, HEAD_DIM, TQ), f32)],
        compiler_params=pltpu.CompilerParams(dimension_semantics=("arbitrary", "arbitrary"),
                                             vmem_limit_bytes=VMEM_LIMIT),
        name="dsa",
    )(pr["qa"], pr["qidx"], pr["misc"], pr["kidx"], pr["ka"], pr["vat"])


def _compress_kernel(ch_ref, pe_ref, w_ref, kc_ref, vct_ref):
    nch = ch_ref.shape[1]
    rowid = lax.broadcasted_iota(jnp.int32, (nch, 1), 0)
    for kv in range(2):
        for g in range(N_KV_GROUPS_B):
            ch = ch_ref[kv * N_KV_GROUPS_B + g]
            top = jnp.dot((ch + pe_ref[kv, 0:1, :]).astype(bf16), w_ref[kv, 0], preferred_element_type=f32)
            bot = jnp.dot((ch + pe_ref[kv, 1:2, :]).astype(bf16), w_ref[kv, 1], preferred_element_type=f32)
            blk = top + pltpu.roll(bot, nch - 1, 0)
            blk = jnp.where(rowid < nch - 1, blk, 0.0)
            if kv == 0:
                kc_ref[:, g * LANES:(g + 1) * LANES] = blk.astype(bf16)
            else:
                vct_ref[g] = blk.T[0:HEAD_DIM, :].astype(bf16)


def _compress_call(kvcmp, pe_k, pe_v, w_k, w_v, B, S):
    nch = S // CMP_STRIDE
    G, dh = N_KV_GROUPS_B, HEAD_DIM
    ch = kvcmp.reshape(B, nch, CMP_STRIDE, 2 * G, dh).transpose(0, 3, 1, 2, 4)
    ch = ch.reshape(B, 2 * G, nch, CMP_STRIDE * dh)
    pe = jnp.stack([pe_k, pe_v]).reshape(2, 2, CMP_STRIDE * dh)
    w = jnp.stack([w_k, w_v]).reshape(2, 2, CMP_STRIDE * dh, dh)
    w = jnp.concatenate([w, w], axis=-1).astype(bf16)
    return pl.pallas_call(
        _compress_kernel,
        grid=(B,),
        in_specs=[pl.BlockSpec((None, 2 * G, nch, CMP_STRIDE * dh), lambda b: (b, 0, 0, 0)),
                  pl.BlockSpec(pe.shape, lambda b: (0, 0, 0)),
                  pl.BlockSpec(w.shape, lambda b: (0, 0, 0, 0))],
        out_specs=[pl.BlockSpec((nch, G * LANES), lambda b: (b, 0)),
                   pl.BlockSpec((G, dh, nch), lambda b: (b, 0, 0))],
        out_shape=[jax.ShapeDtypeStruct((B * nch, G * LANES), bf16),
                   jax.ShapeDtypeStruct((B * G, dh, nch), bf16)],
        compiler_params=pltpu.CompilerParams(dimension_semantics=("arbitrary",),
                                             vmem_limit_bytes=VMEM_LIMIT),
        name="compress",
    )(ch, pe, w)


def _split3(v):
    a = v.astype(bf16)
    r = v - a.astype(f32)
    b = r.astype(bf16)
    c = (r - b.astype(f32)).astype(bf16)
    return a, b, c


def _nsa_kernel(qn_ref, qr_ref, misc_ref, kc_ref, vct_ref, kslc_ref, vslct_ref, kwin_ref, vwint_ref,
                o_ref, s_ref, acc_ref, blk_ref, *, n_sel):
    TQ = qn_ref.shape[0]
    CH = KEY_CHUNK
    SUB = LANES
    S = kslc_ref.shape[0]
    nch = kc_ref.shape[0]
    n_cmp = nch - 1
    n_blk = S // SLC_BLOCK
    G = N_KV_GROUPS_B
    R = N_HEADS_B // G
    q0 = pl.program_id(1) * TQ
    n_ch = (q0 + TQ + CH - 1) // CH
    first_win = jnp.maximum(q0 - WINDOW, 0) // CH
    tpos = q0 + lax.broadcasted_iota(jnp.int32, (1, TQ), 1)
    lane = lax.broadcasted_iota(jnp.int32, (1, LANES), 1)
    gate = jax.nn.sigmoid(misc_ref[...].T)

    def rows(c):
        return pl.ds(pl.multiple_of(c * CH, CH), CH)

    def sub_rows(i):
        return pl.ds(pl.multiple_of(i * SUB, SUB), SUB)

    def gate_of(h, branch):
        a = GATE_LANE0 + 3 * h + branch
        return gate[a:a + 1, :]

    def head_q(ref, h):
        qc = ref[:, (h // 2) * LANES:(h // 2 + 1) * LANES]
        return jnp.where(lane // HEAD_DIM == h % 2, qc, jnp.zeros_like(qc))

    def attend(k_ref, vt_ref, g, q_list, bias_of, c_lo, c_hi):
        def logits_chunk(c, maxes):
            maxes = list(maxes)
            for i in (2 * c, 2 * c + 1):
                bias = bias_of(i)
                kb = k_ref[sub_rows(i), g * LANES:(g + 1) * LANES]
                for r in range(R):
                    s = _nt_dot(kb, q_list[r]) + bias
                    s_ref[r, sub_rows(i), :] = s
                    maxes[r] = jnp.maximum(maxes[r], jnp.max(s, axis=0, keepdims=True))
            return tuple(maxes)

        maxes = lax.fori_loop(c_lo, c_hi, logits_chunk, (jnp.full((1, TQ), -3.0e38, f32),) * R)
        acc_ref[...] = jnp.zeros_like(acc_ref)

        def value_chunk(c, sums):
            new_sums = []
            for r in range(R):
                e = jnp.exp(s_ref[r, rows(c), :] - maxes[r])
                vt = vt_ref[c, g * HEAD_DIM:(g + 1) * HEAD_DIM, :]
                acc_ref[r] += jnp.dot(vt, e.astype(bf16), preferred_element_type=f32)
                new_sums.append(sums[r] + jnp.sum(e, axis=0, keepdims=True))
            return tuple(new_sums)

        sums = lax.fori_loop(c_lo, c_hi, value_chunk, (jnp.zeros((1, TQ), f32),) * R)
        return [acc_ref[r] / sums[r] for r in range(R)]

    cpos = lax.broadcasted_iota(jnp.int32, (nch, 1), 0)
    cmask = (cpos * CMP_STRIDE + CMP_BLOCK - 1 <= tpos) & (cpos < n_cmp)
    oj = lax.broadcasted_iota(jnp.int32, (n_blk, nch), 0)
    oc = lax.broadcasted_iota(jnp.int32, (n_blk, nch), 1)
    overlap_t = ((oc * CMP_STRIDE <= oj * SLC_BLOCK + SLC_BLOCK - 1)
                 & (oc * CMP_STRIDE + CMP_BLOCK - 1 >= oj * SLC_BLOCK) & (oc < n_cmp))
    overlap_t = jnp.where(overlap_t, 1.0, 0.0).astype(bf16)
    bj = lax.broadcasted_iota(jnp.int32, (n_blk, 1), 0)
    cur = tpos // SLC_BLOCK
    forced = (bj == 0) | (bj == cur) | (bj == cur - 1)
    admissible = bj * SLC_BLOCK <= tpos

    def window_bias(i):
        kpos = i * SUB + lax.broadcasted_iota(jnp.int32, (SUB, 1), 0)
        return jnp.where((kpos <= tpos) & (kpos > tpos - WINDOW), 0.0, NEG)

    outs = []
    for g in range(G):
        heads = [g * R + r for r in range(R)]
        kc = kc_ref[:, g * LANES:(g + 1) * LANES]
        vct = vct_ref[g]
        psum = jnp.zeros((nch, TQ), f32)
        o_heads = []
        for h in heads:
            lc = jnp.where(cmask, _nt_dot(kc, head_q(qn_ref, h)), NEG)
            e = jnp.where(cmask, jnp.exp(lc - jnp.max(lc, axis=0, keepdims=True)), 0.0)
            l = jnp.sum(e, axis=0, keepdims=True)
            p = e / jnp.where(l > 0.0, l, 1.0)
            psum = psum + p
            o_heads.append(gate_of(h, 0) * jnp.dot(vct, p.astype(bf16), preferred_element_type=f32))
        s_blk = sum(jnp.dot(overlap_t, piece, preferred_element_type=f32) for piece in _split3(psum))
        s_blk = jnp.where(forced, jnp.inf, jnp.where(admissible, s_blk, -jnp.inf))
        rank = jnp.zeros((n_blk, TQ), f32)
        for i in range(n_blk):
            row = s_blk[i:i + 1, :]
            beats = (row > s_blk) | ((row == s_blk) & (bj > i))
            rank = rank + jnp.where(beats, 1.0, 0.0)
        chosen_bias = jnp.where(rank < float(n_sel), 0.0, NEG)
        for j in range(n_blk):
            blk_ref[g * n_blk + j] = jnp.broadcast_to(chosen_bias[j:j + 1, :], blk_ref.shape[1:])

        def selected_bias(i):
            per_sub = SUB // SLC_BLOCK
            parts = []
            for jj in range(per_sub):
                b8 = blk_ref[g * n_blk + i * per_sub + jj]
                parts.extend([b8] * (SLC_BLOCK // b8.shape[0]))
            kpos = i * SUB + lax.broadcasted_iota(jnp.int32, (SUB, 1), 0)
            return jnp.where(kpos <= tpos, jnp.concatenate(parts, axis=0), NEG)

        q_rope = [head_q(qr_ref, h) for h in heads]
        o_slc = attend(kslc_ref, vslct_ref, g, q_rope, selected_bias, 0, n_ch)
        o_win = attend(kwin_ref, vwint_ref, g, q_rope, window_bias, first_win, n_ch)
        for r, h in enumerate(heads):
            outs.append(o_heads[r] + gate_of(h, 1) * o_slc[r] + gate_of(h, 2) * o_win[r])
    o_ref[...] = jnp.concatenate(outs, axis=0).T


def _nsa_call(pr, kc, vct, B, S):
    TQ = min(QUERY_TILE, S)
    n_sel = min(N_SLC, S // SLC_BLOCK)
    nch = S // CMP_STRIDE
    n_chunks = S // KEY_CHUNK
    G, R = N_KV_GROUPS_B, N_HEADS_B // N_KV_GROUPS_B
    qspec = lambda w_: pl.BlockSpec((TQ, w_), lambda b, i: (b * (S // TQ) + i, 0))
    kspec = lambda w_: pl.BlockSpec((S, w_), lambda b, i: (b, 0))
    vspec = pl.BlockSpec((n_chunks, G * HEAD_DIM, KEY_CHUNK), lambda b, i: (b, 0, 0))
    return pl.pallas_call(
        functools.partial(_nsa_kernel, n_sel=n_sel),
        grid=(B, S // TQ),
        in_specs=[qspec(512), qspec(512), qspec(128),
                  pl.BlockSpec((nch, G * LANES), lambda b, i: (b, 0)),
                  pl.BlockSpec((G, HEAD_DIM, nch), lambda b, i: (b, 0, 0)),
                  kspec(256), vspec, kspec(256), vspec],
        out_specs=qspec(512),
        out_shape=jax.ShapeDtypeStruct((B * S, 512), f32),
        scratch_shapes=[pltpu.VMEM((R, S, TQ), f32), pltpu.VMEM((R, HEAD_DIM, TQ), f32),
                        pltpu.VMEM((G * (S // SLC_BLOCK), 8, TQ), f32)],
        compiler_params=pltpu.CompilerParams(dimension_semantics=("arbitrary", "arbitrary"),
                                             vmem_limit_bytes=VMEM_LIMIT),
        name="nsa",
    )(pr["qbn"], pr["qbr"], pr["misc"], kc, vct, pr["kslc"], pr["vslct"], pr["kwin"], pr["vwint"])


def _pool_kernel(u_ref, w_ref, o_ref):
    u = u_ref[...]
    S, C = u.shape
    row = lax.broadcasted_iota(jnp.int32, (S, 1), 0)
    grp = lax.broadcasted_iota(jnp.int32, (1, C), 1) // POOL_GROUP_DIM
    run, width = u, 1
    win_sum = jnp.zeros_like(u)
    win_len = jnp.zeros((1, C), f32)
    for gi, w in enumerate(POOL_WINDOWS):
        while width < w:
            run = run + jnp.where(row >= width, pltpu.roll(run, width, 0), 0.0)
            width *= 2
        assert width == w, "pooling windows must be increasing powers of two"
        win_sum = jnp.where(grp == gi, run, win_sum)
        win_len = jnp.where(grp == gi, float(w), win_len)
    pooled = win_sum / jnp.minimum((row + 1).astype(f32), win_len) - u
    o_ref[...] = jnp.dot(pooled.astype(bf16), w_ref[...], preferred_element_type=f32)


def _pool_call(uc, w_pool, B, S):
    C = N_POOL_GROUPS * POOL_GROUP_DIM
    wbd = jnp.zeros((C, C), f32)
    for gi in range(N_POOL_GROUPS):
        sl = slice(gi * POOL_GROUP_DIM, (gi + 1) * POOL_GROUP_DIM)
        wbd = wbd.at[sl, sl].set(w_pool[gi])
    spec = pl.BlockSpec((S, C), lambda b: (b, 0))
    return pl.pallas_call(
        _pool_kernel,
        grid=(B,),
        in_specs=[spec, pl.BlockSpec((C, C), lambda b: (0, 0))],
        out_specs=spec,
        out_shape=jax.ShapeDtypeStruct((B * S, C), f32),
        compiler_params=pltpu.CompilerParams(dimension_semantics=("arbitrary",),
                                             vmem_limit_bytes=VMEM_LIMIT),
        name="pool",
    )(uc, wbd.astype(bf16))


def _merge_mlp_kernel(x_ref, oa_ref, ob_ref, oc_ref, ga_ref, gb_ref, ps_ref, wo_ref, gpm_ref,
                      gmlp_ref, wup_ref, wdn_ref, gpost_ref, o_ref):
    mixed = jnp.concatenate([_rms(oa_ref[...], ga_ref[...]), _rms(ob_ref[...], gb_ref[...]),
                             _rms(oc_ref[...]) * ps_ref[...]], axis=1).astype(bf16)
    x = x_ref[...] + _rms(jnp.dot(mixed, wo_ref[...], preferred_element_type=f32), gpm_ref[...])
    h = _rms(x, gmlp_ref[...]).astype(bf16)
    d_ff = wup_ref.shape[1]
    f = jnp.zeros_like(x)
    for c in range(d_ff // x.shape[1]):
        sl = slice(c * x.shape[1], (c + 1) * x.shape[1])
        up = jnp.maximum(jnp.dot(h, wup_ref[:, sl], preferred_element_type=f32), 0.0)
        f = f + jnp.dot((up * up).astype(bf16), wdn_ref[sl, :], preferred_element_type=f32)
    o_ref[...] = x + _rms(f, gpost_ref[...])


def _merge_mlp_call(x, oa, ob, oc, ga, gb, ps, wo, gpm, gmlp, wup, wdn, gpost):
    T, D = x.shape
    TM = TOKEN_TILE
    row = lambda a: pl.BlockSpec((TM, a.shape[1]), lambda i: (i, 0))
    full = lambda a: pl.BlockSpec(a.shape, lambda i: (0, 0))
    args = (x, oa, ob, oc, ga, gb, ps, wo, gpm, gmlp, wup, wdn, gpost)
    return pl.pallas_call(
        _merge_mlp_kernel,
        grid=(T // TM,),
        in_specs=[row(a) for a in args[:4]] + [full(a) for a in args[4:]],
        out_specs=pl.BlockSpec((TM, D), lambda i: (i, 0)),
        out_shape=jax.ShapeDtypeStruct((T, D), f32),
        compiler_params=pltpu.CompilerParams(dimension_semantics=("arbitrary",),
                                             vmem_limit_bytes=VMEM_LIMIT),
        name="merge_mlp",
    )(*args)


def _rope_lane_tables(positions, rot_dim, period):
    half = rot_dim // 2
    inv_freq = ROPE_THETA ** (-jnp.arange(0, rot_dim, 2, dtype=f32) / rot_dim)
    ang = positions.astype(f32).reshape(-1, 1) * inv_freq
    cos, sin = jnp.cos(ang), jnp.sin(ang)
    T = ang.shape[0]
    ones, zeros, zh = jnp.ones((T, period - rot_dim), f32), jnp.zeros((T, period - rot_dim), f32), jnp.zeros((T, half), f32)
    c = jnp.concatenate([cos, cos, ones], axis=1)
    sa = jnp.concatenate([-sin, zh, zeros], axis=1)
    sb = jnp.concatenate([zh, sin, zeros], axis=1)
    reps = LANES // period
    return jnp.concatenate([jnp.tile(t, (1, reps)) for t in (c, sa, sb)], axis=1)


def _dup_halves(w):
    D = w.shape[0]
    g = w.reshape(D, -1, 1, HEAD_DIM)
    return jnp.broadcast_to(g, (D, g.shape[1], 2, HEAD_DIM)).reshape(D, -1)


def _layout_w_in(w_in):
    D = w_in.shape[0]
    widths = (256, 128, 128, 32, 4, 512, 128, 128, 128, 128, 128, 128, 24, 256)
    offs = [0]
    for w_ in widths:
        offs.append(offs[-1] + w_)
    (q_a, ckv, q_idx, k_idx, w_idx, q_b, k_cmp, v_cmp, k_slc, v_slc, k_win, v_win, gates, u_c) = [
        w_in[:, offs[i]:offs[i + 1]] for i in range(len(widths))]
    misc = jnp.concatenate([w_idx, gates, jnp.zeros((D, LANES - 28), w_in.dtype)], axis=1)
    cols = [q_a, ckv, q_idx, jnp.tile(k_idx, (1, LANES // IDX_DIM)), q_b, k_cmp, v_cmp,
            _dup_halves(k_slc), v_slc, _dup_halves(k_win), v_win, misc, u_c]
    w = jnp.concatenate(cols, axis=1)
    assert w.shape[1] == N_PROJ
    return w.astype(bf16)


def kernel(x, positions, w_in, g_kv_a, w_kv_up_a, w_cmp_k, w_cmp_v, pe_cmp_k, pe_cmp_v, w_pool,
           pool_scale, g_out_a, g_out_b, w_out, g_pre_mix, g_post_mix, g_pre_mlp, g_post_mlp,
           w_up, w_down):
    B, S, D = x.shape
    depth = w_in.shape[0]
    assert S % QUERY_TILE == 0 or S < QUERY_TILE
    assert (B * S) % TOKEN_TILE == 0
    rope_h = _rope_lane_tables(positions, ROT_DIM_HEAD, HEAD_DIM)
    rope_i = _rope_lane_tables(positions, ROT_DIM_IDX, IDX_DIM)
    row = lambda v: v.reshape(1, -1)
    xt = x.reshape(B * S, D)
    for l in range(depth):
        wkv = jnp.concatenate([_dup_halves(w_kv_up_a[l][:, :HEAD_DIM]), w_kv_up_a[l][:, HEAD_DIM:],
                               jnp.zeros((KV_RANK_A, LANES - HEAD_DIM), f32)], axis=1).astype(bf16)
        pr = _proj_call(xt, row(g_pre_mix[l]), _layout_w_in(w_in[l]), row(g_kv_a[l]), wkv, rope_h, rope_i)
        o_a = _dsa_call(pr, B, S)
        kc, vc = _compress_call(pr["kvcmp"], pe_cmp_k[l], pe_cmp_v[l], w_cmp_k[l], w_cmp_v[l], B, S)
        o_b = _nsa_call(pr, kc, vc, B, S)
        o_c = _pool_call(pr["uc"], w_pool[l], B, S)
        xt = _merge_mlp_call(xt, o_a, o_b, o_c, row(g_out_a[l]), row(g_out_b[l]), row(pool_scale[l]),
                             w_out[l].astype(bf16), row(g_post_mix[l]), row(g_pre_mlp[l]),
                             w_up[l].astype(bf16), w_down[l].astype(bf16), row(g_post_mlp[l]))
    return xt.reshape(B, S, D)
```

```python
import functools

import jax
import jax.numpy as jnp
from jax import lax
from jax.experimental import pallas as pl
from jax.experimental.pallas import tpu as pltpu

f32 = jnp.float32
bf16 = jnp.bfloat16

HEAD_DIM = 64
N_HEADS_A = 4
N_HEADS_B = 8
N_KV_GROUPS_B = 2
N_POOL_GROUPS = 4
POOL_GROUP_DIM = 64
POOL_WINDOWS = (2, 4, 8, 16)
KV_RANK_A = 128
N_IDX_HEADS = 4
IDX_DIM = 32
TOPK_MAX = 256
CMP_BLOCK = 32
CMP_STRIDE = 16
SLC_BLOCK = 64
N_SLC = 16
WINDOW = 512
ROPE_THETA = 500000.0
ROT_DIM_HEAD = HEAD_DIM // 4
ROT_DIM_IDX = IDX_DIM // 4
EPS = 1e-6
NEG = -1e30
SCALE = HEAD_DIM ** -0.5

LANES = 128
TOKEN_TILE = 512
QUERY_TILE = 256
KEY_CHUNK = 256
DENOM_ROWS = 16
SPAN_CHUNKS = 2
VMEM_LIMIT = 56 * 1024 * 1024

_COLS = {}
_off = 0
for _name, _w in (("qa", 256), ("ckv", 128), ("qidx", 128), ("kidx", 128), ("qb", 512),
                  ("kvcmp", 256), ("kslc", 256), ("vslc", 128), ("kwin", 256), ("vwin", 128),
                  ("misc", 128), ("uc", 256)):
    _COLS[_name] = (_off, _off + _w)
    _off += _w
N_PROJ = _off
GATE_LANE0 = N_IDX_HEADS


def _rms(v, gain=None):
    y = v * lax.rsqrt(jnp.mean(v * v, axis=-1, keepdims=True) + EPS)
    return y if gain is None else y * gain


def _rope128(v, c, sa, sb, half):
    return v * c + pltpu.roll(v, LANES - half, 1) * sa + pltpu.roll(v, half, 1) * sb


def _nt_dot(a, b):
    return lax.dot_general(a, b, (((1,), (1,)), ((), ())), preferred_element_type=f32)


def _proj_kernel(x_ref, g_ref, w_ref, gkv_ref, wkv_ref, rh_ref, ri_ref,
                 qa_ref, ka_ref, vat_ref, qidx_ref, kidx_ref, qbn_ref, qbr_ref,
                 kvcmp_ref, kslc_ref, vslct_ref, kwin_ref, vwint_ref, misc_ref, uc_ref):
    h = _rms(x_ref[...], g_ref[...])
    p = jnp.dot(h.astype(bf16), w_ref[...], preferred_element_type=f32)
    hc, hsa, hsb = rh_ref[:, 0:128], rh_ref[:, 128:256], rh_ref[:, 256:384]
    ic, isa, isb = ri_ref[:, 0:128], ri_ref[:, 128:256], ri_ref[:, 256:384]
    hh, ih = ROT_DIM_HEAD // 2, ROT_DIM_IDX // 2

    def col(name, j=0):
        a = _COLS[name][0] + j * LANES
        return p[:, a:a + LANES]

    for j in range(2):
        qa_ref[:, j * LANES:(j + 1) * LANES] = (_rope128(col("qa", j), hc, hsa, hsb, hh) * SCALE).astype(bf16)
    kv = jnp.dot(_rms(col("ckv"), gkv_ref[...]).astype(bf16), wkv_ref[...], preferred_element_type=f32)
    ka_ref[...] = _rope128(kv[:, 0:LANES], hc, hsa, hsb, hh).astype(bf16)
    for cc in range(vat_ref.shape[0]):
        tok = slice(cc * KEY_CHUNK, (cc + 1) * KEY_CHUNK)
        vat_ref[cc] = kv[tok, LANES:2 * LANES].T[0:HEAD_DIM, :].astype(bf16)
        vslct_ref[cc] = col("vslc")[tok, :].T.astype(bf16)
        vwint_ref[cc] = col("vwin")[tok, :].T.astype(bf16)
    qidx_ref[...] = _rope128(col("qidx"), ic, isa, isb, ih).astype(bf16)
    kidx_ref[...] = _rope128(col("kidx"), ic, isa, isb, ih).astype(bf16)
    for j in range(4):
        q = col("qb", j)
        qbn_ref[:, j * LANES:(j + 1) * LANES] = (q * SCALE).astype(bf16)
        qbr_ref[:, j * LANES:(j + 1) * LANES] = (_rope128(q, hc, hsa, hsb, hh) * SCALE).astype(bf16)
    for j in range(2):
        sl = slice(j * LANES, (j + 1) * LANES)
        kvcmp_ref[:, sl] = col("kvcmp", j)
        kslc_ref[:, sl] = _rope128(col("kslc", j), hc, hsa, hsb, hh).astype(bf16)
        kwin_ref[:, sl] = _rope128(col("kwin", j), hc, hsa, hsb, hh).astype(bf16)
        uc_ref[:, sl] = col("uc", j)
    misc_ref[...] = col("misc")


def _proj_call(x, g, w, gkv, wkv, rope_h, rope_i):
    T, D = x.shape
    TM = TOKEN_TILE
    row = lambda w_: pl.BlockSpec((TM, w_), lambda i: (i, 0))
    full = lambda a: pl.BlockSpec(a.shape, lambda i: (0, 0))
    outs = [("qa", 256, bf16), ("ka", 128, bf16), ("vat", -HEAD_DIM, bf16), ("qidx", 128, bf16),
            ("kidx", 128, bf16), ("qbn", 512, bf16), ("qbr", 512, bf16), ("kvcmp", 256, f32),
            ("kslc", 256, bf16), ("vslct", -LANES, bf16), ("kwin", 256, bf16), ("vwint", -LANES, bf16),
            ("misc", 128, f32), ("uc", 256, f32)]
    tspec = lambda d: pl.BlockSpec((TM // KEY_CHUNK, d, KEY_CHUNK), lambda i: (i, 0, 0))
    tshape = lambda d: jax.ShapeDtypeStruct((T // KEY_CHUNK, d, KEY_CHUNK), bf16)
    res = pl.pallas_call(
        _proj_kernel,
        grid=(T // TM,),
        in_specs=[row(D), full(g), full(w), full(gkv), full(wkv), row(384), row(384)],
        out_specs=[tspec(-w_) if w_ < 0 else row(w_) for _, w_, _ in outs],
        out_shape=[tshape(-w_) if w_ < 0 else jax.ShapeDtypeStruct((T, w_), dt) for _, w_, dt in outs],
        compiler_params=pltpu.CompilerParams(dimension_semantics=("arbitrary",),
                                             vmem_limit_bytes=VMEM_LIMIT),
        name="proj",
    )(x, g, w, gkv, wkv, rope_h, rope_i)
    return dict(zip([n for n, _, _ in outs], res))


KEY_OF_ONE_ULP = 1
KEY_OF_INF = 0x7F800000
KEY_OF_NEG_MAX = -(2 ** 31 - 2 ** 23)
COUNT_SLAB = 32
SEARCH_STEPS_PER_CHECK = 4


def _attend_spans(span_logits, values_t, s_ref, acc_ref, p_lo, p_hi, extra):
    n_heads = s_ref.shape[0]
    TQ = s_ref.shape[2]
    span = SPAN_CHUNKS * KEY_CHUNK
    ones = jnp.ones((acc_ref.shape[1] - HEAD_DIM, span), bf16)
    acc_ref[...] = jnp.zeros_like(acc_ref)

    def step(p, st):
        maxes, extra = st
        span_max, extra = span_logits(p, extra)
        new_maxes = []
        for h in range(n_heads):
            vt = jnp.concatenate([values_t(h, SPAN_CHUNKS * p + j) for j in range(SPAN_CHUNKS)], axis=1)
            m_new = jnp.maximum(maxes[h], span_max[h])
            e = jnp.exp((s_ref[h] - m_new).astype(bf16))
            pv = jnp.dot(jnp.concatenate([vt, ones], axis=0), e, preferred_element_type=f32)
            acc_ref[h] = jnp.exp(maxes[h] - m_new) * acc_ref[h] + pv
            new_maxes.append(m_new)
        return tuple(new_maxes), extra

    lax.fori_loop(p_lo, p_hi, step, ((jnp.full((1, TQ), -3.0e38, f32),) * n_heads, extra))
    return [acc_ref[h, 0:HEAD_DIM, :] / acc_ref[h, HEAD_DIM:HEAD_DIM + 1, :] for h in range(n_heads)]


def _ordered_key(v):
    bits = lax.bitcast_convert_type(v, jnp.int32)
    return bits ^ ((bits >> 31) & 0x7FFFFFFF)


def _dsa_kernel(qa_ref, qidx_ref, misc_ref, kidx_ref, ka_ref, vat_ref, o_ref,
                key_ref, s_ref, acc_ref, *, n_keep):
    TQ = qa_ref.shape[0]
    CH = KEY_CHUNK
    q0 = pl.program_id(1) * TQ
    n_ch = (q0 + TQ + CH - 1) // CH
    tpos = q0 + lax.broadcasted_iota(jnp.int32, (1, TQ), 1)
    lane = lax.broadcasted_iota(jnp.int32, (1, LANES), 1)
    k = float(n_keep)

    SUB = LANES
    subs_per_span = SPAN_CHUNKS * CH // SUB
    n_span = (n_ch + SPAN_CHUNKS - 1) // SPAN_CHUNKS

    def sub_rows(i):
        return pl.ds(pl.multiple_of(i * SUB, SUB), SUB)

    w_t = misc_ref[...].T[0:8, :]

    def index_q(h):
        qidx = qidx_ref[...]
        return jnp.where(lane // IDX_DIM == h, qidx, jnp.zeros_like(qidx))

    def score_span(p, carry):
        for i in [p * subs_per_span + j for j in range(subs_per_span)]:
            kb = kidx_ref[sub_rows(i), :]
            sc = jnp.zeros((SUB, TQ), f32)
            for h in range(N_IDX_HEADS):
                sc = sc + jnp.maximum(_nt_dot(kb, index_q(h)), 0.0) * w_t[h:h + 1, :]
            kpos = i * SUB + lax.broadcasted_iota(jnp.int32, (SUB, 1), 0)
            sc = jnp.where(sc == 0.0, 0.0, sc)
            key_ref[sub_rows(i), :] = _ordered_key(jnp.where(kpos <= tpos, sc, -jnp.inf))
        return carry

    lax.fori_loop(0, n_span, score_span, 0)

    def count_ge(v):
        vs = jnp.broadcast_to(v, (COUNT_SLAB, TQ))

        def chunk(c, acc):
            base = pl.multiple_of(c * CH, CH)
            for r in range(CH // COUNT_SLAB):
                ks = key_ref[pl.ds(base + r * COUNT_SLAB, COUNT_SLAB), :]
                acc = acc + jnp.where(ks >= vs, 1, 0)
            return acc

        acc = lax.fori_loop(0, n_ch, chunk, jnp.zeros((COUNT_SLAB, TQ), jnp.int32))
        return jnp.sum(acc.astype(f32), axis=0, keepdims=True)

    need = (tpos + 1) > n_keep
    c0 = count_ge(jnp.zeros((1, TQ), jnp.int32))
    c1 = count_ge(jnp.full((1, TQ), KEY_OF_ONE_ULP, jnp.int32))
    positive = c1 >= k
    settled = ~need | ((c0 >= k) & (c1 < k)) | (c1 == k)
    thr = jnp.where(need, jnp.where(c1 == k, KEY_OF_ONE_ULP, 0), KEY_OF_NEG_MAX)
    lo = jnp.where(positive, KEY_OF_ONE_ULP, KEY_OF_NEG_MAX)
    hi = jnp.where(positive, KEY_OF_INF, 0)
    active = jnp.where(settled, 0, 1)

    def search_cond(st):
        return st[4] > 0

    def search_body(st):
        lo, hi, thr, active, _ = st
        for _ in range(SEARCH_STEPS_PER_CHECK):
            act = active > 0
            mid = lo + ((hi - lo) >> 1)
            collapsed = (hi - lo) <= 1
            c = count_ge(mid)
            found = c == k
            thr = jnp.where(act & found, mid, jnp.where(act & collapsed, lo, thr))
            go = act & ~found & ~collapsed
            lo = jnp.where(go & (c > k), mid, lo)
            hi = jnp.where(go & (c < k), mid, hi)
            active = jnp.where(go, 1, 0)
        return lo, hi, thr, active, jnp.sum(active)

    _, _, thr, _, _ = lax.while_loop(search_cond, search_body, (lo, hi, thr, active, jnp.sum(active)))
    needed = k - count_ge(thr + 1)

    ri = lax.broadcasted_iota(jnp.int32, (SUB, SUB), 0)
    rj = lax.broadcasted_iota(jnp.int32, (SUB, SUB), 1)
    strict_lower = jnp.where(rj < ri, 1.0, 0.0).astype(bf16)

    def head_q(h):
        qc = qa_ref[:, (h // 2) * LANES:(h // 2 + 1) * LANES]
        return jnp.where(lane // HEAD_DIM == h % 2, qc, jnp.zeros_like(qc))

    def span_logits(p, ties_before):
        span_max = [None] * N_HEADS_A
        for j in range(subs_per_span):
            i = p * subs_per_span + j
            keys = key_ref[sub_rows(i), :]
            tie = keys == thr
            tie_f = jnp.where(tie, 1.0, 0.0)
            rank = jnp.dot(strict_lower, tie_f.astype(bf16), preferred_element_type=f32) + ties_before
            sel = (keys > thr) | (tie & (rank < needed))
            bias = jnp.where(sel, 0.0, NEG)
            kb = ka_ref[sub_rows(i), :]
            for h in range(N_HEADS_A):
                s = _nt_dot(kb, head_q(h)) + bias
                s_ref[h, j * SUB:(j + 1) * SUB, :] = s
                m = jnp.max(s, axis=0, keepdims=True)
                span_max[h] = m if j == 0 else jnp.maximum(span_max[h], m)
            ties_before = ties_before + jnp.sum(tie_f, axis=0, keepdims=True)
        return span_max, ties_before

    outs = _attend_spans(span_logits, lambda h, c: vat_ref[c], s_ref, acc_ref, 0, n_span,
                         jnp.zeros((1, TQ), f32))
    o_ref[...] = jnp.concatenate(outs, axis=0).T


def _dsa_call(pr, B, S):
    TQ = min(QUERY_TILE, S)
    n_keep = min(TOPK_MAX, S // 4)
    n_chunks = S // KEY_CHUNK
    qspec = lambda w_: pl.BlockSpec((TQ, w_), lambda b, i: (b * (S // TQ) + i, 0))
    kspec = lambda w_: pl.BlockSpec((S, w_), lambda b, i: (b, 0))
    vspec = pl.BlockSpec((n_chunks, HEAD_DIM, KEY_CHUNK), lambda b, i: (b, 0, 0))
    return pl.pallas_call(
        functools.partial(_dsa_kernel, n_keep=n_keep),
        grid=(B, S // TQ),
        in_specs=[qspec(256), qspec(128), qspec(128), kspec(128), kspec(128), vspec],
        out_specs=qspec(256),
        out_shape=jax.ShapeDtypeStruct((B * S, 256), f32),
        scratch_shapes=[pltpu.VMEM((S, TQ), jnp.int32),
                        pltpu.VMEM((N_HEADS_A, SPAN_CHUNKS * KEY_CHUNK, TQ), f32),
                        pltpu.VMEM((N_HEADS_A, HEAD_DIM + DENOM_ROWS, TQ), f32)],
        compiler_params=pltpu.CompilerParams(dimension_semantics=("arbitrary", "arbitrary"),
                                             vmem_limit_bytes=VMEM_LIMIT),
        name="dsa",
    )(pr["qa"], pr["qidx"], pr["misc"], pr["kidx"], pr["ka"], pr["vat"])


def _compress_kernel(ch_ref, pe_ref, w_ref, kc_ref, vct_ref):
    nch = ch_ref.shape[1]
    rowid = lax.broadcasted_iota(jnp.int32, (nch, 1), 0)
    for kv in range(2):
        for g in range(N_KV_GROUPS_B):
            ch = ch_ref[kv * N_KV_GROUPS_B + g]
            top = jnp.dot((ch + pe_ref[kv, 0:1, :]).astype(bf16), w_ref[kv, 0], preferred_element_type=f32)
            bot = jnp.dot((ch + pe_ref[kv, 1:2, :]).astype(bf16), w_ref[kv, 1], preferred_element_type=f32)
            blk = top + pltpu.roll(bot, nch - 1, 0)
            blk = jnp.where(rowid < nch - 1, blk, 0.0)
            if kv == 0:
                kc_ref[:, g * LANES:(g + 1) * LANES] = blk.astype(bf16)
            else:
                vct_ref[g] = blk.T[0:HEAD_DIM, :].astype(bf16)


def _compress_call(kvcmp, pe_k, pe_v, w_k, w_v, B, S):
    nch = S // CMP_STRIDE
    G, dh = N_KV_GROUPS_B, HEAD_DIM
    ch = kvcmp.reshape(B, nch, CMP_STRIDE, 2 * G, dh).transpose(0, 3, 1, 2, 4)
    ch = ch.reshape(B, 2 * G, nch, CMP_STRIDE * dh)
    pe = jnp.stack([pe_k, pe_v]).reshape(2, 2, CMP_STRIDE * dh)
    w = jnp.stack([w_k, w_v]).reshape(2, 2, CMP_STRIDE * dh, dh)
    w = jnp.concatenate([w, w], axis=-1).astype(bf16)
    return pl.pallas_call(
        _compress_kernel,
        grid=(B,),
        in_specs=[pl.BlockSpec((None, 2 * G, nch, CMP_STRIDE * dh), lambda b: (b, 0, 0, 0)),
                  pl.BlockSpec(pe.shape, lambda b: (0, 0, 0)),
                  pl.BlockSpec(w.shape, lambda b: (0, 0, 0, 0))],
        out_specs=[pl.BlockSpec((nch, G * LANES), lambda b: (b, 0)),
                   pl.BlockSpec((G, dh, nch), lambda b: (b, 0, 0))],
        out_shape=[jax.ShapeDtypeStruct((B * nch, G * LANES), bf16),
                   jax.ShapeDtypeStruct((B * G, dh, nch), bf16)],
        compiler_params=pltpu.CompilerParams(dimension_semantics=("arbitrary",),
                                             vmem_limit_bytes=VMEM_LIMIT),
        name="compress",
    )(ch, pe, w)


def _split3(v):
    a = v.astype(bf16)
    r = v - a.astype(f32)
    b = r.astype(bf16)
    c = (r - b.astype(f32)).astype(bf16)
    return a, b, c


def _nsa_kernel(qn_ref, qr_ref, misc_ref, kc_ref, vct_ref, kslc_ref, vslct_ref, kwin_ref, vwint_ref,
                o_ref, s_ref, acc_ref, blk_ref, out_ref, *, n_sel):
    TQ = qn_ref.shape[0]
    CH = KEY_CHUNK
    SUB = LANES
    S = kslc_ref.shape[0]
    nch = kc_ref.shape[0]
    n_cmp = nch - 1
    n_blk = S // SLC_BLOCK
    G = N_KV_GROUPS_B
    R = N_HEADS_B // G
    q0 = pl.program_id(1) * TQ
    span = SPAN_CHUNKS * CH
    subs_per_span = span // SUB
    n_span = (q0 + TQ + span - 1) // span
    first_win = jnp.maximum(q0 - WINDOW, 0) // span
    tpos = q0 + lax.broadcasted_iota(jnp.int32, (1, TQ), 1)
    lane = lax.broadcasted_iota(jnp.int32, (1, LANES), 1)
    gate = jax.nn.sigmoid(misc_ref[...].T)

    def sub_rows(i):
        return pl.ds(pl.multiple_of(i * SUB, SUB), SUB)

    def gate_of(h, branch):
        a = GATE_LANE0 + 3 * h + branch
        return gate[a:a + 1, :]

    def head_q(ref, h):
        qc = ref[:, (h // 2) * LANES:(h // 2 + 1) * LANES]
        return jnp.where(lane // HEAD_DIM == h % 2, qc, jnp.zeros_like(qc))

    def attend(k_ref, vt_ref, biases_of, p_lo, p_hi):
        def span_logits(p, extra):
            span_max = [None] * N_HEADS_B
            for j in range(subs_per_span):
                i = p * subs_per_span + j
                biases = biases_of(i)
                for g in range(G):
                    kb = k_ref[sub_rows(i), g * LANES:(g + 1) * LANES]
                    for h in range(g * R, (g + 1) * R):
                        s = _nt_dot(kb, head_q(qr_ref, h)) + biases[g]
                        s_ref[h, j * SUB:(j + 1) * SUB, :] = s
                        m = jnp.max(s, axis=0, keepdims=True)
                        span_max[h] = m if j == 0 else jnp.maximum(span_max[h], m)
            return span_max, extra

        def values_t(h, c):
            g = h // R
            return vt_ref[c, g * HEAD_DIM:(g + 1) * HEAD_DIM, :]

        return _attend_spans(span_logits, values_t, s_ref, acc_ref, p_lo, p_hi, 0)

    cpos = lax.broadcasted_iota(jnp.int32, (nch, 1), 0)
    cmask = (cpos * CMP_STRIDE + CMP_BLOCK - 1 <= tpos) & (cpos < n_cmp)
    oj = lax.broadcasted_iota(jnp.int32, (n_blk, nch), 0)
    oc = lax.broadcasted_iota(jnp.int32, (n_blk, nch), 1)
    overlap_t = ((oc * CMP_STRIDE <= oj * SLC_BLOCK + SLC_BLOCK - 1)
                 & (oc * CMP_STRIDE + CMP_BLOCK - 1 >= oj * SLC_BLOCK) & (oc < n_cmp))
    overlap_t = jnp.where(overlap_t, 1.0, 0.0).astype(bf16)
    bj = lax.broadcasted_iota(jnp.int32, (n_blk, 1), 0)
    cur = tpos // SLC_BLOCK
    forced = (bj == 0) | (bj == cur) | (bj == cur - 1)
    admissible = bj * SLC_BLOCK <= tpos

    def window_biases(i):
        kpos = i * SUB + lax.broadcasted_iota(jnp.int32, (SUB, 1), 0)
        return [jnp.where((kpos <= tpos) & (kpos > tpos - WINDOW), 0.0, NEG)] * G

    def selected_biases(i):
        per_sub = SUB // SLC_BLOCK
        kpos = i * SUB + lax.broadcasted_iota(jnp.int32, (SUB, 1), 0)
        biases = []
        for g in range(G):
            parts = []
            for jj in range(per_sub):
                b8 = blk_ref[g * n_blk + i * per_sub + jj]
                parts.extend([b8] * (SLC_BLOCK // b8.shape[0]))
            biases.append(jnp.where(kpos <= tpos, jnp.concatenate(parts, axis=0), NEG))
        return biases

    for g in range(G):
        heads = [g * R + r for r in range(R)]
        kc = kc_ref[:, g * LANES:(g + 1) * LANES]
        vct = vct_ref[g]
        psum = jnp.zeros((nch, TQ), f32)
        for h in heads:
            lc = jnp.where(cmask, _nt_dot(kc, head_q(qn_ref, h)), NEG)
            e = jnp.where(cmask, jnp.exp(lc - jnp.max(lc, axis=0, keepdims=True)), 0.0)
            l = jnp.sum(e, axis=0, keepdims=True)
            p = e / jnp.where(l > 0.0, l, 1.0)
            psum = psum + p
            out_ref[h] = gate_of(h, 0) * jnp.dot(vct, p.astype(bf16), preferred_element_type=f32)
        s_blk = sum(jnp.dot(overlap_t, piece, preferred_element_type=f32) for piece in _split3(psum))
        s_blk = jnp.where(forced, jnp.inf, jnp.where(admissible, s_blk, -jnp.inf))
        rank = jnp.zeros((n_blk, TQ), f32)
        for i in range(n_blk):
            row = s_blk[i:i + 1, :]
            beats = (row > s_blk) | ((row == s_blk) & (bj > i))
            rank = rank + jnp.where(beats, 1.0, 0.0)
        chosen_bias = jnp.where(rank < float(n_sel), 0.0, NEG)
        for j in range(n_blk):
            blk_ref[g * n_blk + j] = jnp.broadcast_to(chosen_bias[j:j + 1, :], blk_ref.shape[1:])

    for branch, k_ref, vt_ref, biases_of, p_lo in ((1, kslc_ref, vslct_ref, selected_biases, 0),
                                                   (2, kwin_ref, vwint_ref, window_biases, first_win)):
        o_branch = attend(k_ref, vt_ref, biases_of, p_lo, n_span)
        for h in range(N_HEADS_B):
            out_ref[h] += gate_of(h, branch) * o_branch[h]
    o_ref[...] = jnp.concatenate([out_ref[h] for h in range(N_HEADS_B)], axis=0).T


def _nsa_call(pr, kc, vct, B, S):
    TQ = min(QUERY_TILE, S)
    n_sel = min(N_SLC, S // SLC_BLOCK)
    nch = S // CMP_STRIDE
    n_chunks = S // KEY_CHUNK
    G, R = N_KV_GROUPS_B, N_HEADS_B // N_KV_GROUPS_B
    qspec = lambda w_: pl.BlockSpec((TQ, w_), lambda b, i: (b * (S // TQ) + i, 0))
    kspec = lambda w_: pl.BlockSpec((S, w_), lambda b, i: (b, 0))
    vspec = pl.BlockSpec((n_chunks, G * HEAD_DIM, KEY_CHUNK), lambda b, i: (b, 0, 0))
    return pl.pallas_call(
        functools.partial(_nsa_kernel, n_sel=n_sel),
        grid=(B, S // TQ),
        in_specs=[qspec(512), qspec(512), qspec(128),
                  pl.BlockSpec((nch, G * LANES), lambda b, i: (b, 0)),
                  pl.BlockSpec((G, HEAD_DIM, nch), lambda b, i: (b, 0, 0)),
                  kspec(256), vspec, kspec(256), vspec],
        out_specs=qspec(512),
        out_shape=jax.ShapeDtypeStruct((B * S, 512), f32),
        scratch_shapes=[pltpu.VMEM((N_HEADS_B, SPAN_CHUNKS * KEY_CHUNK, TQ), f32),
                        pltpu.VMEM((N_HEADS_B, HEAD_DIM + DENOM_ROWS, TQ), f32),
                        pltpu.VMEM((G * (S // SLC_BLOCK), 8, TQ), f32),
                        pltpu.VMEM((N_HEADS_B, HEAD_DIM, TQ), f32)],
        compiler_params=pltpu.CompilerParams(dimension_semantics=("arbitrary", "arbitrary"),
                                             vmem_limit_bytes=VMEM_LIMIT),
        name="nsa",
    )(pr["qbn"], pr["qbr"], pr["misc"], kc, vct, pr["kslc"], pr["vslct"], pr["kwin"], pr["vwint"])


def _pool_kernel(u_ref, w_ref, o_ref):
    u = u_ref[...]
    S, C = u.shape
    row = lax.broadcasted_iota(jnp.int32, (S, 1), 0)
    grp = lax.broadcasted_iota(jnp.int32, (1, C), 1) // POOL_GROUP_DIM
    run, width = u, 1
    win_sum = jnp.zeros_like(u)
    win_len = jnp.zeros((1, C), f32)
    for gi, w in enumerate(POOL_WINDOWS):
        while width < w:
            run = run + jnp.where(row >= width, pltpu.roll(run, width, 0), 0.0)
            width *= 2
        assert width == w, "pooling windows must be increasing powers of two"
        win_sum = jnp.where(grp == gi, run, win_sum)
        win_len = jnp.where(grp == gi, float(w), win_len)
    pooled = win_sum / jnp.minimum((row + 1).astype(f32), win_len) - u
    o_ref[...] = jnp.dot(pooled.astype(bf16), w_ref[...], preferred_element_type=f32)


def _pool_call(uc, w_pool, B, S):
    C = N_POOL_GROUPS * POOL_GROUP_DIM
    wbd = jnp.zeros((C, C), f32)
    for gi in range(N_POOL_GROUPS):
        sl = slice(gi * POOL_GROUP_DIM, (gi + 1) * POOL_GROUP_DIM)
        wbd = wbd.at[sl, sl].set(w_pool[gi])
    spec = pl.BlockSpec((S, C), lambda b: (b, 0))
    return pl.pallas_call(
        _pool_kernel,
        grid=(B,),
        in_specs=[spec, pl.BlockSpec((C, C), lambda b: (0, 0))],
        out_specs=spec,
        out_shape=jax.ShapeDtypeStruct((B * S, C), f32),
        compiler_params=pltpu.CompilerParams(dimension_semantics=("arbitrary",),
                                             vmem_limit_bytes=VMEM_LIMIT),
        name="pool",
    )(uc, wbd.astype(bf16))


def _merge_mlp_kernel(x_ref, oa_ref, ob_ref, oc_ref, ga_ref, gb_ref, ps_ref, wo_ref, gpm_ref,
                      gmlp_ref, wup_ref, wdn_ref, gpost_ref, o_ref):
    mixed = jnp.concatenate([_rms(oa_ref[...], ga_ref[...]), _rms(ob_ref[...], gb_ref[...]),
                             _rms(oc_ref[...]) * ps_ref[...]], axis=1).astype(bf16)
    x = x_ref[...] + _rms(jnp.dot(mixed, wo_ref[...], preferred_element_type=f32), gpm_ref[...])
    h = _rms(x, gmlp_ref[...]).astype(bf16)
    d_ff = wup_ref.shape[1]
    f = jnp.zeros_like(x)
    for c in range(d_ff // x.shape[1]):
        sl = slice(c * x.shape[1], (c + 1) * x.shape[1])
        up = jnp.maximum(jnp.dot(h, wup_ref[:, sl], preferred_element_type=f32), 0.0)
        f = f + jnp.dot((up * up).astype(bf16), wdn_ref[sl, :], preferred_element_type=f32)
    o_ref[...] = x + _rms(f, gpost_ref[...])


def _merge_mlp_call(x, oa, ob, oc, ga, gb, ps, wo, gpm, gmlp, wup, wdn, gpost):
    T, D = x.shape
    TM = TOKEN_TILE
    row = lambda a: pl.BlockSpec((TM, a.shape[1]), lambda i: (i, 0))
    full = lambda a: pl.BlockSpec(a.shape, lambda i: (0, 0))
    args = (x, oa, ob, oc, ga, gb, ps, wo, gpm, gmlp, wup, wdn, gpost)
    return pl.pallas_call(
        _merge_mlp_kernel,
        grid=(T // TM,),
        in_specs=[row(a) for a in args[:4]] + [full(a) for a in args[4:]],
        out_specs=pl.BlockSpec((TM, D), lambda i: (i, 0)),
        out_shape=jax.ShapeDtypeStruct((T, D), f32),
        compiler_params=pltpu.CompilerParams(dimension_semantics=("arbitrary",),
                                             vmem_limit_bytes=VMEM_LIMIT),
        name="merge_mlp",
    )(*args)


def _rope_lane_tables(positions, rot_dim, period):
    half = rot_dim // 2
    inv_freq = ROPE_THETA ** (-jnp.arange(0, rot_dim, 2, dtype=f32) / rot_dim)
    ang = positions.astype(f32).reshape(-1, 1) * inv_freq
    cos, sin = jnp.cos(ang), jnp.sin(ang)
    T = ang.shape[0]
    ones, zeros, zh = jnp.ones((T, period - rot_dim), f32), jnp.zeros((T, period - rot_dim), f32), jnp.zeros((T, half), f32)
    c = jnp.concatenate([cos, cos, ones], axis=1)
    sa = jnp.concatenate([-sin, zh, zeros], axis=1)
    sb = jnp.concatenate([zh, sin, zeros], axis=1)
    reps = LANES // period
    return jnp.concatenate([jnp.tile(t, (1, reps)) for t in (c, sa, sb)], axis=1)


def _dup_halves(w):
    D = w.shape[0]
    g = w.reshape(D, -1, 1, HEAD_DIM)
    return jnp.broadcast_to(g, (D, g.shape[1], 2, HEAD_DIM)).reshape(D, -1)


def _layout_w_in(w_in):
    D = w_in.shape[0]
    widths = (256, 128, 128, 32, 4, 512, 128, 128, 128, 128, 128, 128, 24, 256)
    offs = [0]
    for w_ in widths:
        offs.append(offs[-1] + w_)
    (q_a, ckv, q_idx, k_idx, w_idx, q_b, k_cmp, v_cmp, k_slc, v_slc, k_win, v_win, gates, u_c) = [
        w_in[:, offs[i]:offs[i + 1]] for i in range(len(widths))]
    misc = jnp.concatenate([w_idx, gates, jnp.zeros((D, LANES - 28), w_in.dtype)], axis=1)
    cols = [q_a, ckv, q_idx, jnp.tile(k_idx, (1, LANES // IDX_DIM)), q_b, k_cmp, v_cmp,
            _dup_halves(k_slc), v_slc, _dup_halves(k_win), v_win, misc, u_c]
    w = jnp.concatenate(cols, axis=1)
    assert w.shape[1] == N_PROJ
    return w.astype(bf16)


def kernel(x, positions, w_in, g_kv_a, w_kv_up_a, w_cmp_k, w_cmp_v, pe_cmp_k, pe_cmp_v, w_pool,
           pool_scale, g_out_a, g_out_b, w_out, g_pre_mix, g_post_mix, g_pre_mlp, g_post_mlp,
           w_up, w_down):
    B, S, D = x.shape
    depth = w_in.shape[0]
    assert S % (SPAN_CHUNKS * KEY_CHUNK) == 0 and QUERY_TILE == KEY_CHUNK
    assert (B * S) % TOKEN_TILE == 0
    rope_h = _rope_lane_tables(positions, ROT_DIM_HEAD, HEAD_DIM)
    rope_i = _rope_lane_tables(positions, ROT_DIM_IDX, IDX_DIM)
    row = lambda v: v.reshape(1, -1)
    xt = x.reshape(B * S, D)
    for l in range(depth):
        wkv = jnp.concatenate([_dup_halves(w_kv_up_a[l][:, :HEAD_DIM]), w_kv_up_a[l][:, HEAD_DIM:],
                               jnp.zeros((KV_RANK_A, LANES - HEAD_DIM), f32)], axis=1).astype(bf16)
        pr = _proj_call(xt, row(g_pre_mix[l]), _layout_w_in(w_in[l]), row(g_kv_a[l]), wkv, rope_h, rope_i)
        o_a = _dsa_call(pr, B, S)
        kc, vc = _compress_call(pr["kvcmp"], pe_cmp_k[l], pe_cmp_v[l], w_cmp_k[l], w_cmp_v[l], B, S)
        o_b = _nsa_call(pr, kc, vc, B, S)
        o_c = _pool_call(pr["uc"], w_pool[l], B, S)
        xt = _merge_mlp_call(xt, o_a, o_b, o_c, row(g_out_a[l]), row(g_out_b[l]), row(pool_scale[l]),
                             w_out[l].astype(bf16), row(g_post_mix[l]), row(g_pre_mlp[l]),
                             w_up[l].astype(bf16), w_down[l].astype(bf16), row(g_post_mlp[l]))
    return xt.reshape(B, S, D)
```

```python
import functools

import jax
import jax.numpy as jnp
from jax import lax
from jax.experimental import pallas as pl
from jax.experimental.pallas import tpu as pltpu

f32 = jnp.float32
bf16 = jnp.bfloat16

HEAD_DIM = 64
N_HEADS_A = 4
N_HEADS_B = 8
N_KV_GROUPS_B = 2
N_POOL_GROUPS = 4
POOL_GROUP_DIM = 64
POOL_WINDOWS = (2, 4, 8, 16)
KV_RANK_A = 128
N_IDX_HEADS = 4
IDX_DIM = 32
TOPK_MAX = 256
CMP_BLOCK = 32
CMP_STRIDE = 16
SLC_BLOCK = 64
N_SLC = 16
WINDOW = 512
ROPE_THETA = 500000.0
ROT_DIM_HEAD = HEAD_DIM // 4
ROT_DIM_IDX = IDX_DIM // 4
EPS = 1e-6
NEG = -1e30
SCALE = HEAD_DIM ** -0.5

LANES = 128
TOKEN_TILE = 512
QUERY_TILE = 256
KEY_CHUNK = 256
BF16_ROWS = 16
DENOM_ROWS = BF16_ROWS
LOG2E = 1.4426950408889634
LOGITS_LEAD = 3
SPAN_CHUNKS = 2
VMEM_LIMIT = 56 * 1024 * 1024

_COLS = {}
_off = 0
for _name, _w in (("qa", 256), ("ckv", 128), ("qidx", 128), ("kidx", 128), ("qb", 512),
                  ("kvcmp", 256), ("kslc", 256), ("vslc", 128), ("kwin", 256), ("vwin", 128),
                  ("misc", 128), ("uc", 256)):
    _COLS[_name] = (_off, _off + _w)
    _off += _w
N_PROJ = _off
GATE_LANE0 = N_IDX_HEADS


def _rms(v, gain=None):
    y = v * lax.rsqrt(jnp.mean(v * v, axis=-1, keepdims=True) + EPS)
    return y if gain is None else y * gain


def _rope128(v, c, sa, sb, half):
    return v * c + pltpu.roll(v, LANES - half, 1) * sa + pltpu.roll(v, half, 1) * sb


def _nt_dot(a, b):
    return lax.dot_general(a, b, (((1,), (1,)), ((), ())), preferred_element_type=f32)


def _proj_kernel(x_ref, g_ref, w_ref, gkv_ref, wkv_ref, rh_ref, ri_ref,
                 qa_ref, ka_ref, vat_ref, qidx_ref, kidx_ref, qbn_ref, qbr_ref,
                 kvcmp_ref, kslc_ref, vslct_ref, kwin_ref, vwint_ref, misc_ref, uc_ref):
    h = _rms(x_ref[...], g_ref[...])
    p = jnp.dot(h.astype(bf16), w_ref[...], preferred_element_type=f32)
    hc, hsa, hsb = rh_ref[:, 0:128], rh_ref[:, 128:256], rh_ref[:, 256:384]
    ic, isa, isb = ri_ref[:, 0:128], ri_ref[:, 128:256], ri_ref[:, 256:384]
    hh, ih = ROT_DIM_HEAD // 2, ROT_DIM_IDX // 2

    def col(name, j=0):
        a = _COLS[name][0] + j * LANES
        return p[:, a:a + LANES]

    for j in range(2):
        qa_ref[:, j * LANES:(j + 1) * LANES] = (_rope128(col("qa", j), hc, hsa, hsb, hh)
                                                * (SCALE * LOG2E)).astype(bf16)
    kv = jnp.dot(_rms(col("ckv"), gkv_ref[...]).astype(bf16), wkv_ref[...], preferred_element_type=f32)
    ka_ref[...] = _rope128(kv[:, 0:LANES], hc, hsa, hsb, hh).astype(bf16)
    for cc in range(vat_ref.shape[0]):
        tok = slice(cc * KEY_CHUNK, (cc + 1) * KEY_CHUNK)
        vat_ref[cc] = kv[tok, LANES:2 * LANES].T[0:HEAD_DIM, :].astype(bf16)
        vslct_ref[cc] = col("vslc")[tok, :].T.astype(bf16)
        vwint_ref[cc] = col("vwin")[tok, :].T.astype(bf16)
    qidx_ref[...] = _rope128(col("qidx"), ic, isa, isb, ih).astype(bf16)
    kidx_ref[...] = _rope128(col("kidx"), ic, isa, isb, ih).astype(bf16)
    for j in range(4):
        q = col("qb", j)
        qbn_ref[:, j * LANES:(j + 1) * LANES] = (q * SCALE).astype(bf16)
        qbr_ref[:, j * LANES:(j + 1) * LANES] = (_rope128(q, hc, hsa, hsb, hh) * (SCALE * LOG2E)).astype(bf16)
    for j in range(2):
        sl = slice(j * LANES, (j + 1) * LANES)
        kvcmp_ref[:, sl] = col("kvcmp", j)
        kslc_ref[:, sl] = _rope128(col("kslc", j), hc, hsa, hsb, hh).astype(bf16)
        kwin_ref[:, sl] = _rope128(col("kwin", j), hc, hsa, hsb, hh).astype(bf16)
        uc_ref[:, sl] = col("uc", j)
    misc_ref[...] = col("misc")


def _proj_call(x, g, w, gkv, wkv, rope_h, rope_i):
    T, D = x.shape
    TM = TOKEN_TILE
    row = lambda w_: pl.BlockSpec((TM, w_), lambda i: (i, 0))
    full = lambda a: pl.BlockSpec(a.shape, lambda i: (0, 0))
    outs = [("qa", 256, bf16), ("ka", 128, bf16), ("vat", -HEAD_DIM, bf16), ("qidx", 128, bf16),
            ("kidx", 128, bf16), ("qbn", 512, bf16), ("qbr", 512, bf16), ("kvcmp", 256, f32),
            ("kslc", 256, bf16), ("vslct", -LANES, bf16), ("kwin", 256, bf16), ("vwint", -LANES, bf16),
            ("misc", 128, f32), ("uc", 256, f32)]
    tspec = lambda d: pl.BlockSpec((TM // KEY_CHUNK, d, KEY_CHUNK), lambda i: (i, 0, 0))
    tshape = lambda d: jax.ShapeDtypeStruct((T // KEY_CHUNK, d, KEY_CHUNK), bf16)
    res = pl.pallas_call(
        _proj_kernel,
        grid=(T // TM,),
        in_specs=[row(D), full(g), full(w), full(gkv), full(wkv), row(384), row(384)],
        out_specs=[tspec(-w_) if w_ < 0 else row(w_) for _, w_, _ in outs],
        out_shape=[tshape(-w_) if w_ < 0 else jax.ShapeDtypeStruct((T, w_), dt) for _, w_, dt in outs],
        compiler_params=pltpu.CompilerParams(dimension_semantics=("arbitrary",),
                                             vmem_limit_bytes=VMEM_LIMIT),
        name="proj",
    )(x, g, w, gkv, wkv, rope_h, rope_i)
    return dict(zip([n for n, _, _ in outs], res))


KEY_OF_ONE_ULP = 1
KEY_OF_INF = 0x7F800000
KEY_OF_NEG_MAX = -(2 ** 31 - 2 ** 23)
COUNT_SLAB = 32
SEARCH_STEPS_PER_CHECK = 4


def _masked_logits(k_block, q_head, bias):
    s = _nt_dot(k_block, q_head).astype(bf16) + bias
    m = s[0:BF16_ROWS, :]
    for r in range(1, s.shape[0] // BF16_ROWS):
        m = jnp.maximum(m, s[r * BF16_ROWS:(r + 1) * BF16_ROWS, :])
    return s, jnp.max(m.astype(f32), axis=0, keepdims=True)


def _attend_spans(prepare_span, block_logits, values_t, s_ref, acc_ref, p_lo, p_hi, extra):
    n_heads = s_ref.shape[0]
    TQ = s_ref.shape[2]
    span = SPAN_CHUNKS * KEY_CHUNK
    ones = jnp.ones((acc_ref.shape[1] - HEAD_DIM, span), bf16)
    acc_ref[...] = jnp.zeros_like(acc_ref)

    def step(p, st):
        maxes, extra = st
        extra = prepare_span(p, extra)
        new_maxes = [None] * n_heads

        def logits(h):
            m_new = maxes[h]
            for j in range(span // LANES):
                s, m = block_logits(h, p, j)
                s_ref[h, j * LANES:(j + 1) * LANES, :] = s
                m_new = jnp.maximum(m_new, m)
            new_maxes[h] = m_new

        def weigh(h):
            vt = jnp.concatenate([values_t(h, SPAN_CHUNKS * p + j) for j in range(SPAN_CHUNKS)], axis=1)
            e = jnp.exp2(s_ref[h] - new_maxes[h].astype(bf16))
            pv = jnp.dot(jnp.concatenate([vt, ones], axis=0), e, preferred_element_type=f32)
            acc_ref[h] = jnp.exp2(maxes[h] - new_maxes[h]) * acc_ref[h] + pv

        for h in range(n_heads + LOGITS_LEAD):
            if h < n_heads:
                logits(h)
            if h >= LOGITS_LEAD:
                weigh(h - LOGITS_LEAD)
        return tuple(new_maxes), extra

    lax.fori_loop(p_lo, p_hi, step, ((jnp.full((1, TQ), -3.0e38, f32),) * n_heads, extra))
    return [acc_ref[h, 0:HEAD_DIM, :] / acc_ref[h, HEAD_DIM:HEAD_DIM + 1, :] for h in range(n_heads)]


def _ordered_key(v):
    bits = lax.bitcast_convert_type(v, jnp.int32)
    return bits ^ ((bits >> 31) & 0x7FFFFFFF)


def _dsa_kernel(qa_ref, qidx_ref, misc_ref, kidx_ref, ka_ref, vat_ref, o_ref,
                key_ref, s_ref, acc_ref, qm_ref, bias_ref, *, n_keep):
    TQ = qa_ref.shape[0]
    CH = KEY_CHUNK
    q0 = pl.program_id(1) * TQ
    n_ch = (q0 + TQ + CH - 1) // CH
    tpos = q0 + lax.broadcasted_iota(jnp.int32, (1, TQ), 1)
    lane = lax.broadcasted_iota(jnp.int32, (1, LANES), 1)
    k = float(n_keep)

    SUB = LANES
    subs_per_span = SPAN_CHUNKS * CH // SUB
    n_span = (n_ch + SPAN_CHUNKS - 1) // SPAN_CHUNKS

    def sub_rows(i):
        return pl.ds(pl.multiple_of(i * SUB, SUB), SUB)

    w_t = misc_ref[...].T[0:8, :]

    def index_q(h):
        qidx = qidx_ref[...]
        return jnp.where(lane // IDX_DIM == h, qidx, jnp.zeros_like(qidx))

    def score_span(p, carry):
        for i in [p * subs_per_span + j for j in range(subs_per_span)]:
            kb = kidx_ref[sub_rows(i), :]
            sc = jnp.zeros((SUB, TQ), f32)
            for h in range(N_IDX_HEADS):
                sc = sc + jnp.maximum(_nt_dot(kb, index_q(h)), 0.0) * w_t[h:h + 1, :]
            kpos = i * SUB + lax.broadcasted_iota(jnp.int32, (SUB, 1), 0)
            sc = jnp.where(sc == 0.0, 0.0, sc)
            key_ref[sub_rows(i), :] = _ordered_key(jnp.where(kpos <= tpos, sc, -jnp.inf))
        return carry

    lax.fori_loop(0, n_span, score_span, 0)

    def count_ge(v):
        vs = jnp.broadcast_to(v, (COUNT_SLAB, TQ))

        def chunk(c, acc):
            base = pl.multiple_of(c * CH, CH)
            for r in range(CH // COUNT_SLAB):
                ks = key_ref[pl.ds(base + r * COUNT_SLAB, COUNT_SLAB), :]
                acc = acc + jnp.where(ks >= vs, 1, 0)
            return acc

        acc = lax.fori_loop(0, n_ch, chunk, jnp.zeros((COUNT_SLAB, TQ), jnp.int32))
        return jnp.sum(acc.astype(f32), axis=0, keepdims=True)

    need = (tpos + 1) > n_keep
    c0 = count_ge(jnp.zeros((1, TQ), jnp.int32))
    c1 = count_ge(jnp.full((1, TQ), KEY_OF_ONE_ULP, jnp.int32))
    positive = c1 >= k
    settled = ~need | ((c0 >= k) & (c1 < k)) | (c1 == k)
    thr = jnp.where(need, jnp.where(c1 == k, KEY_OF_ONE_ULP, 0), KEY_OF_NEG_MAX)
    lo = jnp.where(positive, KEY_OF_ONE_ULP, KEY_OF_NEG_MAX)
    hi = jnp.where(positive, KEY_OF_INF, 0)
    active = jnp.where(settled, 0, 1)

    def search_cond(st):
        return st[4] > 0

    def search_body(st):
        lo, hi, thr, active, _ = st
        for _ in range(SEARCH_STEPS_PER_CHECK):
            act = active > 0
            mid = lo + ((hi - lo) >> 1)
            collapsed = (hi - lo) <= 1
            c = count_ge(mid)
            found = c == k
            thr = jnp.where(act & found, mid, jnp.where(act & collapsed, lo, thr))
            go = act & ~found & ~collapsed
            lo = jnp.where(go & (c > k), mid, lo)
            hi = jnp.where(go & (c < k), mid, hi)
            active = jnp.where(go, 1, 0)
        return lo, hi, thr, active, jnp.sum(active)

    _, _, thr, _, _ = lax.while_loop(search_cond, search_body, (lo, hi, thr, active, jnp.sum(active)))
    needed = k - count_ge(thr + 1)

    ri = lax.broadcasted_iota(jnp.int32, (SUB, SUB), 0)
    rj = lax.broadcasted_iota(jnp.int32, (SUB, SUB), 1)
    strict_lower = jnp.where(rj < ri, 1.0, 0.0).astype(bf16)
    for h in range(N_HEADS_A):
        qc = qa_ref[:, (h // 2) * LANES:(h // 2 + 1) * LANES]
        qm_ref[h] = jnp.where(lane // HEAD_DIM == h % 2, qc, jnp.zeros_like(qc))

    def prepare_span(p, ties_before):
        for j in range(subs_per_span):
            keys = key_ref[sub_rows(p * subs_per_span + j), :]
            tie = keys == thr
            tie_f = jnp.where(tie, 1.0, 0.0)
            rank = jnp.dot(strict_lower, tie_f.astype(bf16), preferred_element_type=f32) + ties_before
            sel = (keys > thr) | (tie & (rank < needed))
            bias_ref[j * SUB:(j + 1) * SUB, :] = jnp.where(sel, 0.0, NEG).astype(bf16)
            ties_before = ties_before + jnp.sum(tie_f, axis=0, keepdims=True)
        return ties_before

    def block_logits(h, p, j):
        return _masked_logits(ka_ref[sub_rows(p * subs_per_span + j), :], qm_ref[h],
                              bias_ref[j * SUB:(j + 1) * SUB, :])

    outs = _attend_spans(prepare_span, block_logits, lambda h, c: vat_ref[c], s_ref, acc_ref, 0, n_span,
                         jnp.zeros((1, TQ), f32))
    o_ref[...] = jnp.concatenate(outs, axis=0).T


def _dsa_call(pr, B, S):
    TQ = min(QUERY_TILE, S)
    n_keep = min(TOPK_MAX, S // 4)
    n_chunks = S // KEY_CHUNK
    qspec = lambda w_: pl.BlockSpec((TQ, w_), lambda b, i: (b * (S // TQ) + i, 0))
    kspec = lambda w_: pl.BlockSpec((S, w_), lambda b, i: (b, 0))
    vspec = pl.BlockSpec((n_chunks, HEAD_DIM, KEY_CHUNK), lambda b, i: (b, 0, 0))
    return pl.pallas_call(
        functools.partial(_dsa_kernel, n_keep=n_keep),
        grid=(B, S // TQ),
        in_specs=[qspec(256), qspec(128), qspec(128), kspec(128), kspec(128), vspec],
        out_specs=qspec(256),
        out_shape=jax.ShapeDtypeStruct((B * S, 256), f32),
        scratch_shapes=[pltpu.VMEM((S, TQ), jnp.int32),
                        pltpu.VMEM((N_HEADS_A, SPAN_CHUNKS * KEY_CHUNK, TQ), bf16),
                        pltpu.VMEM((N_HEADS_A, HEAD_DIM + DENOM_ROWS, TQ), f32),
                        pltpu.VMEM((N_HEADS_A, TQ, LANES), bf16),
                        pltpu.VMEM((SPAN_CHUNKS * KEY_CHUNK, TQ), bf16)],
        compiler_params=pltpu.CompilerParams(dimension_semantics=("arbitrary", "arbitrary"),
                                             vmem_limit_bytes=VMEM_LIMIT),
        name="dsa",
    )(pr["qa"], pr["qidx"], pr["misc"], pr["kidx"], pr["ka"], pr["vat"])


def _compress_kernel(ch_ref, pe_ref, w_ref, kc_ref, vct_ref):
    nch = ch_ref.shape[1]
    rowid = lax.broadcasted_iota(jnp.int32, (nch, 1), 0)
    for kv in range(2):
        for g in range(N_KV_GROUPS_B):
            ch = ch_ref[kv * N_KV_GROUPS_B + g]
            top = jnp.dot((ch + pe_ref[kv, 0:1, :]).astype(bf16), w_ref[kv, 0], preferred_element_type=f32)
            bot = jnp.dot((ch + pe_ref[kv, 1:2, :]).astype(bf16), w_ref[kv, 1], preferred_element_type=f32)
            blk = top + pltpu.roll(bot, nch - 1, 0)
            blk = jnp.where(rowid < nch - 1, blk, 0.0)
            if kv == 0:
                kc_ref[:, g * LANES:(g + 1) * LANES] = blk.astype(bf16)
            else:
                vct_ref[g] = blk.T[0:HEAD_DIM, :].astype(bf16)


def _compress_call(kvcmp, pe_k, pe_v, w_k, w_v, B, S):
    nch = S // CMP_STRIDE
    G, dh = N_KV_GROUPS_B, HEAD_DIM
    ch = kvcmp.reshape(B, nch, CMP_STRIDE, 2 * G, dh).transpose(0, 3, 1, 2, 4)
    ch = ch.reshape(B, 2 * G, nch, CMP_STRIDE * dh)
    pe = jnp.stack([pe_k, pe_v]).reshape(2, 2, CMP_STRIDE * dh)
    w = jnp.stack([w_k, w_v]).reshape(2, 2, CMP_STRIDE * dh, dh)
    w = jnp.concatenate([w, w], axis=-1).astype(bf16)
    return pl.pallas_call(
        _compress_kernel,
        grid=(B,),
        in_specs=[pl.BlockSpec((None, 2 * G, nch, CMP_STRIDE * dh), lambda b: (b, 0, 0, 0)),
                  pl.BlockSpec(pe.shape, lambda b: (0, 0, 0)),
                  pl.BlockSpec(w.shape, lambda b: (0, 0, 0, 0))],
        out_specs=[pl.BlockSpec((nch, G * LANES), lambda b: (b, 0)),
                   pl.BlockSpec((G, dh, nch), lambda b: (b, 0, 0))],
        out_shape=[jax.ShapeDtypeStruct((B * nch, G * LANES), bf16),
                   jax.ShapeDtypeStruct((B * G, dh, nch), bf16)],
        compiler_params=pltpu.CompilerParams(dimension_semantics=("arbitrary",),
                                             vmem_limit_bytes=VMEM_LIMIT),
        name="compress",
    )(ch, pe, w)


def _split3(v):
    a = v.astype(bf16)
    r = v - a.astype(f32)
    b = r.astype(bf16)
    c = (r - b.astype(f32)).astype(bf16)
    return a, b, c


def _nsa_kernel(qn_ref, qr_ref, misc_ref, kc_ref, vct_ref, kslc_ref, vslct_ref, kwin_ref, vwint_ref,
                o_ref, s_ref, acc_ref, blk_ref, out_ref, qm_ref, bias_ref, *, n_sel):
    TQ = qn_ref.shape[0]
    CH = KEY_CHUNK
    SUB = LANES
    S = kslc_ref.shape[0]
    nch = kc_ref.shape[0]
    n_cmp = nch - 1
    n_blk = S // SLC_BLOCK
    G = N_KV_GROUPS_B
    R = N_HEADS_B // G
    q0 = pl.program_id(1) * TQ
    span = SPAN_CHUNKS * CH
    subs_per_span = span // SUB
    n_span = (q0 + TQ + span - 1) // span
    first_win = jnp.maximum(q0 - WINDOW, 0) // span
    tpos = q0 + lax.broadcasted_iota(jnp.int32, (1, TQ), 1)
    lane = lax.broadcasted_iota(jnp.int32, (1, LANES), 1)
    gate = jax.nn.sigmoid(misc_ref[...].T)

    def sub_rows(i):
        return pl.ds(pl.multiple_of(i * SUB, SUB), SUB)

    def gate_of(h, branch):
        a = GATE_LANE0 + 3 * h + branch
        return gate[a:a + 1, :]

    def head_q(ref, h):
        qc = ref[:, (h // 2) * LANES:(h // 2 + 1) * LANES]
        return jnp.where(lane // HEAD_DIM == h % 2, qc, jnp.zeros_like(qc))

    def attend(k_ref, vt_ref, biases_of, p_lo, p_hi):
        def prepare_span(p, extra):
            for j in range(subs_per_span):
                biases = biases_of(p * subs_per_span + j)
                for g in range(G):
                    bias_ref[g, j * SUB:(j + 1) * SUB, :] = biases[g]
            return extra

        def block_logits(h, p, j):
            g = h // R
            return _masked_logits(k_ref[sub_rows(p * subs_per_span + j), g * LANES:(g + 1) * LANES],
                                  qm_ref[h], bias_ref[g, j * SUB:(j + 1) * SUB, :])

        def values_t(h, c):
            g = h // R
            return vt_ref[c, g * HEAD_DIM:(g + 1) * HEAD_DIM, :]

        return _attend_spans(prepare_span, block_logits, values_t, s_ref, acc_ref, p_lo, p_hi, 0)

    cpos = lax.broadcasted_iota(jnp.int32, (nch, 1), 0)
    cmask = (cpos * CMP_STRIDE + CMP_BLOCK - 1 <= tpos) & (cpos < n_cmp)
    oj = lax.broadcasted_iota(jnp.int32, (n_blk, nch), 0)
    oc = lax.broadcasted_iota(jnp.int32, (n_blk, nch), 1)
    overlap_t = ((oc * CMP_STRIDE <= oj * SLC_BLOCK + SLC_BLOCK - 1)
                 & (oc * CMP_STRIDE + CMP_BLOCK - 1 >= oj * SLC_BLOCK) & (oc < n_cmp))
    overlap_t = jnp.where(overlap_t, 1.0, 0.0).astype(bf16)
    bj = lax.broadcasted_iota(jnp.int32, (n_blk, 1), 0)
    cur = tpos // SLC_BLOCK
    forced = (bj == 0) | (bj == cur) | (bj == cur - 1)
    admissible = bj * SLC_BLOCK <= tpos

    for h in range(N_HEADS_B):
        qm_ref[h] = head_q(qr_ref, h)

    def window_biases(i):
        kpos = i * SUB + lax.broadcasted_iota(jnp.int32, (SUB, 1), 0)
        return [jnp.where((kpos <= tpos) & (kpos > tpos - WINDOW), 0.0, NEG).astype(bf16)] * G

    def selected_biases(i):
        per_sub = SUB // SLC_BLOCK
        kpos = i * SUB + lax.broadcasted_iota(jnp.int32, (SUB, 1), 0)
        biases = []
        for g in range(G):
            parts = []
            for jj in range(per_sub):
                b8 = blk_ref[g * n_blk + i * per_sub + jj]
                parts.extend([b8] * (SLC_BLOCK // b8.shape[0]))
            biases.append(jnp.where(kpos <= tpos, jnp.concatenate(parts, axis=0), NEG).astype(bf16))
        return biases

    for g in range(G):
        heads = [g * R + r for r in range(R)]
        kc = kc_ref[:, g * LANES:(g + 1) * LANES]
        vct = vct_ref[g]
        psum = jnp.zeros((nch, TQ), f32)
        for h in heads:
            lc = jnp.where(cmask, _nt_dot(kc, head_q(qn_ref, h)), NEG)
            e = jnp.where(cmask, jnp.exp(lc - jnp.max(lc, axis=0, keepdims=True)), 0.0)
            l = jnp.sum(e, axis=0, keepdims=True)
            p = e / jnp.where(l > 0.0, l, 1.0)
            psum = psum + p
            out_ref[h] = gate_of(h, 0) * jnp.dot(vct, p.astype(bf16), preferred_element_type=f32)
        s_blk = sum(jnp.dot(overlap_t, piece, preferred_element_type=f32) for piece in _split3(psum))
        s_blk = jnp.where(forced, jnp.inf, jnp.where(admissible, s_blk, -jnp.inf))
        rank = jnp.zeros((n_blk, TQ), f32)
        for i in range(n_blk):
            row = s_blk[i:i + 1, :]
            beats = (row > s_blk) | ((row == s_blk) & (bj > i))
            rank = rank + jnp.where(beats, 1.0, 0.0)
        chosen_bias = jnp.where(rank < float(n_sel), 0.0, NEG)
        for j in range(n_blk):
            blk_ref[g * n_blk + j] = jnp.broadcast_to(chosen_bias[j:j + 1, :], blk_ref.shape[1:])

    for branch, k_ref, vt_ref, biases_of, p_lo in ((1, kslc_ref, vslct_ref, selected_biases, 0),
                                                   (2, kwin_ref, vwint_ref, window_biases, first_win)):
        o_branch = attend(k_ref, vt_ref, biases_of, p_lo, n_span)
        for h in range(N_HEADS_B):
            out_ref[h] += gate_of(h, branch) * o_branch[h]
    o_ref[...] = jnp.concatenate([out_ref[h] for h in range(N_HEADS_B)], axis=0).T


def _nsa_call(pr, kc, vct, B, S):
    TQ = min(QUERY_TILE, S)
    n_sel = min(N_SLC, S // SLC_BLOCK)
    nch = S // CMP_STRIDE
    n_chunks = S // KEY_CHUNK
    G, R = N_KV_GROUPS_B, N_HEADS_B // N_KV_GROUPS_B
    qspec = lambda w_: pl.BlockSpec((TQ, w_), lambda b, i: (b * (S // TQ) + i, 0))
    kspec = lambda w_: pl.BlockSpec((S, w_), lambda b, i: (b, 0))
    vspec = pl.BlockSpec((n_chunks, G * HEAD_DIM, KEY_CHUNK), lambda b, i: (b, 0, 0))
    return pl.pallas_call(
        functools.partial(_nsa_kernel, n_sel=n_sel),
        grid=(B, S // TQ),
        in_specs=[qspec(512), qspec(512), qspec(128),
                  pl.BlockSpec((nch, G * LANES), lambda b, i: (b, 0)),
                  pl.BlockSpec((G, HEAD_DIM, nch), lambda b, i: (b, 0, 0)),
                  kspec(256), vspec, kspec(256), vspec],
        out_specs=qspec(512),
        out_shape=jax.ShapeDtypeStruct((B * S, 512), f32),
        scratch_shapes=[pltpu.VMEM((N_HEADS_B, SPAN_CHUNKS * KEY_CHUNK, TQ), bf16),
                        pltpu.VMEM((N_HEADS_B, HEAD_DIM + DENOM_ROWS, TQ), f32),
                        pltpu.VMEM((G * (S // SLC_BLOCK), 8, TQ), f32),
                        pltpu.VMEM((N_HEADS_B, HEAD_DIM, TQ), f32),
                        pltpu.VMEM((N_HEADS_B, TQ, LANES), bf16),
                        pltpu.VMEM((G, SPAN_CHUNKS * KEY_CHUNK, TQ), bf16)],
        compiler_params=pltpu.CompilerParams(dimension_semantics=("arbitrary", "arbitrary"),
                                             vmem_limit_bytes=VMEM_LIMIT),
        name="nsa",
    )(pr["qbn"], pr["qbr"], pr["misc"], kc, vct, pr["kslc"], pr["vslct"], pr["kwin"], pr["vwint"])


def _pool_kernel(u_ref, w_ref, o_ref):
    u = u_ref[...]
    S, C = u.shape
    row = lax.broadcasted_iota(jnp.int32, (S, 1), 0)
    grp = lax.broadcasted_iota(jnp.int32, (1, C), 1) // POOL_GROUP_DIM
    run, width = u, 1
    win_sum = jnp.zeros_like(u)
    win_len = jnp.zeros((1, C), f32)
    for gi, w in enumerate(POOL_WINDOWS):
        while width < w:
            run = run + jnp.where(row >= width, pltpu.roll(run, width, 0), 0.0)
            width *= 2
        assert width == w, "pooling windows must be increasing powers of two"
        win_sum = jnp.where(grp == gi, run, win_sum)
        win_len = jnp.where(grp == gi, float(w), win_len)
    pooled = win_sum / jnp.minimum((row + 1).astype(f32), win_len) - u
    o_ref[...] = jnp.dot(pooled.astype(bf16), w_ref[...], preferred_element_type=f32)


def _pool_call(uc, w_pool, B, S):
    C = N_POOL_GROUPS * POOL_GROUP_DIM
    wbd = jnp.zeros((C, C), f32)
    for gi in range(N_POOL_GROUPS):
        sl = slice(gi * POOL_GROUP_DIM, (gi + 1) * POOL_GROUP_DIM)
        wbd = wbd.at[sl, sl].set(w_pool[gi])
    spec = pl.BlockSpec((S, C), lambda b: (b, 0))
    return pl.pallas_call(
        _pool_kernel,
        grid=(B,),
        in_specs=[spec, pl.BlockSpec((C, C), lambda b: (0, 0))],
        out_specs=spec,
        out_shape=jax.ShapeDtypeStruct((B * S, C), f32),
        compiler_params=pltpu.CompilerParams(dimension_semantics=("arbitrary",),
                                             vmem_limit_bytes=VMEM_LIMIT),
        name="pool",
    )(uc, wbd.astype(bf16))


def _merge_mlp_kernel(x_ref, oa_ref, ob_ref, oc_ref, ga_ref, gb_ref, ps_ref, wo_ref, gpm_ref,
                      gmlp_ref, wup_ref, wdn_ref, gpost_ref, o_ref):
    mixed = jnp.concatenate([_rms(oa_ref[...], ga_ref[...]), _rms(ob_ref[...], gb_ref[...]),
                             _rms(oc_ref[...]) * ps_ref[...]], axis=1).astype(bf16)
    x = x_ref[...] + _rms(jnp.dot(mixed, wo_ref[...], preferred_element_type=f32), gpm_ref[...])
    h = _rms(x, gmlp_ref[...]).astype(bf16)
    d_ff = wup_ref.shape[1]
    f = jnp.zeros_like(x)
    for c in range(d_ff // x.shape[1]):
        sl = slice(c * x.shape[1], (c + 1) * x.shape[1])
        up = jnp.maximum(jnp.dot(h, wup_ref[:, sl], preferred_element_type=f32), 0.0)
        f = f + jnp.dot((up * up).astype(bf16), wdn_ref[sl, :], preferred_element_type=f32)
    o_ref[...] = x + _rms(f, gpost_ref[...])


def _merge_mlp_call(x, oa, ob, oc, ga, gb, ps, wo, gpm, gmlp, wup, wdn, gpost):
    T, D = x.shape
    TM = TOKEN_TILE
    row = lambda a: pl.BlockSpec((TM, a.shape[1]), lambda i: (i, 0))
    full = lambda a: pl.BlockSpec(a.shape, lambda i: (0, 0))
    args = (x, oa, ob, oc, ga, gb, ps, wo, gpm, gmlp, wup, wdn, gpost)
    return pl.pallas_call(
        _merge_mlp_kernel,
        grid=(T // TM,),
        in_specs=[row(a) for a in args[:4]] + [full(a) for a in args[4:]],
        out_specs=pl.BlockSpec((TM, D), lambda i: (i, 0)),
        out_shape=jax.ShapeDtypeStruct((T, D), f32),
        compiler_params=pltpu.CompilerParams(dimension_semantics=("arbitrary",),
                                             vmem_limit_bytes=VMEM_LIMIT),
        name="merge_mlp",
    )(*args)


def _rope_lane_tables(positions, rot_dim, period):
    half = rot_dim // 2
    inv_freq = ROPE_THETA ** (-jnp.arange(0, rot_dim, 2, dtype=f32) / rot_dim)
    ang = positions.astype(f32).reshape(-1, 1) * inv_freq
    cos, sin = jnp.cos(ang), jnp.sin(ang)
    T = ang.shape[0]
    ones, zeros, zh = jnp.ones((T, period - rot_dim), f32), jnp.zeros((T, period - rot_dim), f32), jnp.zeros((T, half), f32)
    c = jnp.concatenate([cos, cos, ones], axis=1)
    sa = jnp.concatenate([-sin, zh, zeros], axis=1)
    sb = jnp.concatenate([zh, sin, zeros], axis=1)
    reps = LANES // period
    return jnp.concatenate([jnp.tile(t, (1, reps)) for t in (c, sa, sb)], axis=1)


def _dup_halves(w):
    D = w.shape[0]
    g = w.reshape(D, -1, 1, HEAD_DIM)
    return jnp.broadcast_to(g, (D, g.shape[1], 2, HEAD_DIM)).reshape(D, -1)


def _layout_w_in(w_in):
    D = w_in.shape[0]
    widths = (256, 128, 128, 32, 4, 512, 128, 128, 128, 128, 128, 128, 24, 256)
    offs = [0]
    for w_ in widths:
        offs.append(offs[-1] + w_)
    (q_a, ckv, q_idx, k_idx, w_idx, q_b, k_cmp, v_cmp, k_slc, v_slc, k_win, v_win, gates, u_c) = [
        w_in[:, offs[i]:offs[i + 1]] for i in range(len(widths))]
    misc = jnp.concatenate([w_idx, gates, jnp.zeros((D, LANES - 28), w_in.dtype)], axis=1)
    cols = [q_a, ckv, q_idx, jnp.tile(k_idx, (1, LANES // IDX_DIM)), q_b, k_cmp, v_cmp,
            _dup_halves(k_slc), v_slc, _dup_halves(k_win), v_win, misc, u_c]
    w = jnp.concatenate(cols, axis=1)
    assert w.shape[1] == N_PROJ
    return w.astype(bf16)


def kernel(x, positions, w_in, g_kv_a, w_kv_up_a, w_cmp_k, w_cmp_v, pe_cmp_k, pe_cmp_v, w_pool,
           pool_scale, g_out_a, g_out_b, w_out, g_pre_mix, g_post_mix, g_pre_mlp, g_post_mlp,
           w_up, w_down):
    B, S, D = x.shape
    depth = w_in.shape[0]
    assert S % (SPAN_CHUNKS * KEY_CHUNK) == 0 and QUERY_TILE == KEY_CHUNK
    assert (B * S) % TOKEN_TILE == 0
    rope_h = _rope_lane_tables(positions, ROT_DIM_HEAD, HEAD_DIM)
    rope_i = _rope_lane_tables(positions, ROT_DIM_IDX, IDX_DIM)
    row = lambda v: v.reshape(1, -1)
    xt = x.reshape(B * S, D)
    for l in range(depth):
        wkv = jnp.concatenate([_dup_halves(w_kv_up_a[l][:, :HEAD_DIM]), w_kv_up_a[l][:, HEAD_DIM:],
                               jnp.zeros((KV_RANK_A, LANES - HEAD_DIM), f32)], axis=1).astype(bf16)
        pr = _proj_call(xt, row(g_pre_mix[l]), _layout_w_in(w_in[l]), row(g_kv_a[l]), wkv, rope_h, rope_i)
        o_a = _dsa_call(pr, B, S)
        kc, vc = _compress_call(pr["kvcmp"], pe_cmp_k[l], pe_cmp_v[l], w_cmp_k[l], w_cmp_v[l], B, S)
        o_b = _nsa_call(pr, kc, vc, B, S)
        o_c = _pool_call(pr["uc"], w_pool[l], B, S)
        xt = _merge_mlp_call(xt, o_a, o_b, o_c, row(g_out_a[l]), row(g_out_b[l]), row(pool_scale[l]),
                             w_out[l].astype(bf16), row(g_post_mix[l]), row(g_pre_mlp[l]),
                             w_up[l].astype(bf16), w_down[l].astype(bf16), row(g_post_mlp[l]))
    return xt.reshape(B, S, D)
```

```python
import functools

import jax
import jax.numpy as jnp
from jax import lax
from jax.experimental import pallas as pl
from jax.experimental.pallas import tpu as pltpu

f32 = jnp.float32
bf16 = jnp.bfloat16

HEAD_DIM = 64
N_HEADS_A = 4
N_HEADS_B = 8
N_KV_GROUPS_B = 2
N_POOL_GROUPS = 4
POOL_GROUP_DIM = 64
POOL_WINDOWS = (2, 4, 8, 16)
KV_RANK_A = 128
N_IDX_HEADS = 4
IDX_DIM = 32
TOPK_MAX = 256
CMP_BLOCK = 32
CMP_STRIDE = 16
SLC_BLOCK = 64
N_SLC = 16
WINDOW = 512
ROPE_THETA = 500000.0
ROT_DIM_HEAD = HEAD_DIM // 4
ROT_DIM_IDX = IDX_DIM // 4
EPS = 1e-6
NEG = -1e30
SCALE = HEAD_DIM ** -0.5

LANES = 128
TOKEN_TILE = 512
QUERY_TILE = 256
KEY_CHUNK = 256
BF16_ROWS = 16
DENOM_ROWS = BF16_ROWS
LOG2E = 1.4426950408889634
LOGITS_LEAD = 3
SPAN_CHUNKS = 2
VMEM_LIMIT = 56 * 1024 * 1024

_COLS = {}
_off = 0
for _name, _w in (("qa", 256), ("ckv", 128), ("qidx", 128), ("kidx", 128), ("qb", 512),
                  ("kvcmp", 256), ("kslc", 256), ("vslc", 128), ("kwin", 256), ("vwin", 128),
                  ("misc", 128), ("uc", 256)):
    _COLS[_name] = (_off, _off + _w)
    _off += _w
N_PROJ = _off
GATE_LANE0 = N_IDX_HEADS


def _rms(v, gain=None):
    y = v * lax.rsqrt(jnp.mean(v * v, axis=-1, keepdims=True) + EPS)
    return y if gain is None else y * gain


def _rope128(v, c, sa, sb, half):
    return v * c + pltpu.roll(v, LANES - half, 1) * sa + pltpu.roll(v, half, 1) * sb


def _nt_dot(a, b):
    return lax.dot_general(a, b, (((1,), (1,)), ((), ())), preferred_element_type=f32)


def _proj_kernel(x_ref, g_ref, w_ref, gkv_ref, wkv_ref, rh_ref, ri_ref,
                 qa_ref, ka_ref, vat_ref, qidx_ref, kidx_ref, qbn_ref, qbr_ref,
                 kvcmp_ref, kslc_ref, vslct_ref, kwin_ref, vwint_ref, misc_ref, uc_ref):
    h = _rms(x_ref[...], g_ref[...])
    p = jnp.dot(h.astype(bf16), w_ref[...], preferred_element_type=f32)
    hc, hsa, hsb = rh_ref[:, 0:128], rh_ref[:, 128:256], rh_ref[:, 256:384]
    ic, isa, isb = ri_ref[:, 0:128], ri_ref[:, 128:256], ri_ref[:, 256:384]
    hh, ih = ROT_DIM_HEAD // 2, ROT_DIM_IDX // 2

    def col(name, j=0):
        a = _COLS[name][0] + j * LANES
        return p[:, a:a + LANES]

    for j in range(2):
        qa_ref[:, j * LANES:(j + 1) * LANES] = (_rope128(col("qa", j), hc, hsa, hsb, hh)
                                                * (SCALE * LOG2E)).astype(bf16)
    kv = jnp.dot(_rms(col("ckv"), gkv_ref[...]).astype(bf16), wkv_ref[...], preferred_element_type=f32)
    ka_ref[...] = _rope128(kv[:, 0:LANES], hc, hsa, hsb, hh).astype(bf16)
    for cc in range(vat_ref.shape[0]):
        tok = slice(cc * KEY_CHUNK, (cc + 1) * KEY_CHUNK)
        vat_ref[cc] = kv[tok, LANES:2 * LANES].T[0:HEAD_DIM, :].astype(bf16)
        vslct_ref[cc] = col("vslc")[tok, :].T.astype(bf16)
        vwint_ref[cc] = col("vwin")[tok, :].T.astype(bf16)
    qidx_ref[...] = _rope128(col("qidx"), ic, isa, isb, ih).astype(bf16)
    kidx_ref[...] = _rope128(col("kidx"), ic, isa, isb, ih).astype(bf16)
    for j in range(4):
        q = col("qb", j)
        qbn_ref[:, j * LANES:(j + 1) * LANES] = (q * SCALE).astype(bf16)
        qbr_ref[:, j * LANES:(j + 1) * LANES] = (_rope128(q, hc, hsa, hsb, hh) * (SCALE * LOG2E)).astype(bf16)
    for j in range(2):
        sl = slice(j * LANES, (j + 1) * LANES)
        kvcmp_ref[:, sl] = col("kvcmp", j)
        kslc_ref[:, sl] = _rope128(col("kslc", j), hc, hsa, hsb, hh).astype(bf16)
        kwin_ref[:, sl] = _rope128(col("kwin", j), hc, hsa, hsb, hh).astype(bf16)
        uc_ref[:, sl] = col("uc", j)
    misc_ref[...] = col("misc")


def _proj_call(x, g, w, gkv, wkv, rope_h, rope_i):
    T, D = x.shape
    TM = TOKEN_TILE
    row = lambda w_: pl.BlockSpec((TM, w_), lambda i: (i, 0))
    full = lambda a: pl.BlockSpec(a.shape, lambda i: (0, 0))
    outs = [("qa", 256, bf16), ("ka", 128, bf16), ("vat", -HEAD_DIM, bf16), ("qidx", 128, bf16),
            ("kidx", 128, bf16), ("qbn", 512, bf16), ("qbr", 512, bf16), ("kvcmp", 256, f32),
            ("kslc", 256, bf16), ("vslct", -LANES, bf16), ("kwin", 256, bf16), ("vwint", -LANES, bf16),
            ("misc", 128, f32), ("uc", 256, f32)]
    tspec = lambda d: pl.BlockSpec((TM // KEY_CHUNK, d, KEY_CHUNK), lambda i: (i, 0, 0))
    tshape = lambda d: jax.ShapeDtypeStruct((T // KEY_CHUNK, d, KEY_CHUNK), bf16)
    res = pl.pallas_call(
        _proj_kernel,
        grid=(T // TM,),
        in_specs=[row(D), full(g), full(w), full(gkv), full(wkv), row(384), row(384)],
        out_specs=[tspec(-w_) if w_ < 0 else row(w_) for _, w_, _ in outs],
        out_shape=[tshape(-w_) if w_ < 0 else jax.ShapeDtypeStruct((T, w_), dt) for _, w_, dt in outs],
        compiler_params=pltpu.CompilerParams(dimension_semantics=("arbitrary",),
                                             vmem_limit_bytes=VMEM_LIMIT),
        name="proj",
    )(x, g, w, gkv, wkv, rope_h, rope_i)
    return dict(zip([n for n, _, _ in outs], res))


KEY_OF_ONE_ULP = 1
KEY_OF_INF = 0x7F800000
KEY_OF_NEG_MAX = -(2 ** 31 - 2 ** 23)
COUNT_SLAB = 32
SEARCH_STEPS_PER_CHECK = 4


def _column_max(s):
    m = s[0:BF16_ROWS, :]
    for r in range(1, s.shape[0] // BF16_ROWS):
        m = jnp.maximum(m, s[r * BF16_ROWS:(r + 1) * BF16_ROWS, :])
    return jnp.max(m.astype(f32), axis=0, keepdims=True)


def _attend_spans(prepare_span, block_operands, values_t, s_ref, acc_ref, p_lo, p_hi, extra):
    n_heads = s_ref.shape[0]
    TQ = s_ref.shape[2]
    span = SPAN_CHUNKS * KEY_CHUNK
    ones = jnp.ones((acc_ref.shape[1] - HEAD_DIM, span), bf16)
    acc_ref[...] = jnp.zeros_like(acc_ref)

    def step(p, st):
        maxes, extra = st
        extra = prepare_span(p, extra)
        new_maxes = [None] * n_heads

        def logits(h):
            blocks = [block_operands(h, p, j) for j in range(span // LANES)]
            raw = [_nt_dot(kb, q) for kb, q, _ in blocks]
            masked = [r.astype(bf16) + bias for r, (_, _, bias) in zip(raw, blocks)]
            for j, s in enumerate(masked):
                s_ref[h, j * LANES:(j + 1) * LANES, :] = s
            m_new = maxes[h]
            for s in masked:
                m_new = jnp.maximum(m_new, _column_max(s))
            new_maxes[h] = m_new

        def weigh(h):
            vt = jnp.concatenate([values_t(h, SPAN_CHUNKS * p + j) for j in range(SPAN_CHUNKS)], axis=1)
            e = jnp.exp2(s_ref[h] - new_maxes[h].astype(bf16))
            pv = jnp.dot(jnp.concatenate([vt, ones], axis=0), e, preferred_element_type=f32)
            acc_ref[h] = jnp.exp2(maxes[h] - new_maxes[h]) * acc_ref[h] + pv

        for h in range(n_heads + LOGITS_LEAD):
            if h < n_heads:
                logits(h)
            if h >= LOGITS_LEAD:
                weigh(h - LOGITS_LEAD)
        return tuple(new_maxes), extra

    lax.fori_loop(p_lo, p_hi, step, ((jnp.full((1, TQ), -3.0e38, f32),) * n_heads, extra))
    return [acc_ref[h, 0:HEAD_DIM, :] / acc_ref[h, HEAD_DIM:HEAD_DIM + 1, :] for h in range(n_heads)]


def _ordered_key(v):
    bits = lax.bitcast_convert_type(v, jnp.int32)
    return bits ^ ((bits >> 31) & 0x7FFFFFFF)


def _dsa_kernel(qa_ref, qidx_ref, misc_ref, kidx_ref, ka_ref, vat_ref, o_ref,
                key_ref, s_ref, acc_ref, qm_ref, bias_ref, *, n_keep):
    TQ = qa_ref.shape[0]
    CH = KEY_CHUNK
    q0 = pl.program_id(1) * TQ
    n_ch = (q0 + TQ + CH - 1) // CH
    tpos = q0 + lax.broadcasted_iota(jnp.int32, (1, TQ), 1)
    lane = lax.broadcasted_iota(jnp.int32, (1, LANES), 1)
    k = float(n_keep)

    SUB = LANES
    subs_per_span = SPAN_CHUNKS * CH // SUB
    n_span = (n_ch + SPAN_CHUNKS - 1) // SPAN_CHUNKS

    def sub_rows(i):
        return pl.ds(pl.multiple_of(i * SUB, SUB), SUB)

    w_t = misc_ref[...].T[0:8, :]

    def index_q(h):
        qidx = qidx_ref[...]
        return jnp.where(lane // IDX_DIM == h, qidx, jnp.zeros_like(qidx))

    def score_span(p, carry):
        for i in [p * subs_per_span + j for j in range(subs_per_span)]:
            kb = kidx_ref[sub_rows(i), :]
            sc = jnp.zeros((SUB, TQ), f32)
            for h in range(N_IDX_HEADS):
                sc = sc + jnp.maximum(_nt_dot(kb, index_q(h)), 0.0) * w_t[h:h + 1, :]
            kpos = i * SUB + lax.broadcasted_iota(jnp.int32, (SUB, 1), 0)
            sc = jnp.where(sc == 0.0, 0.0, sc)
            key_ref[sub_rows(i), :] = _ordered_key(jnp.where(kpos <= tpos, sc, -jnp.inf))
        return carry

    lax.fori_loop(0, n_span, score_span, 0)

    def count_ge(v):
        vs = jnp.broadcast_to(v, (COUNT_SLAB, TQ))

        def chunk(c, acc):
            base = pl.multiple_of(c * CH, CH)
            for r in range(CH // COUNT_SLAB):
                ks = key_ref[pl.ds(base + r * COUNT_SLAB, COUNT_SLAB), :]
                acc = acc + jnp.where(ks >= vs, 1, 0)
            return acc

        acc = lax.fori_loop(0, n_ch, chunk, jnp.zeros((COUNT_SLAB, TQ), jnp.int32))
        return jnp.sum(acc.astype(f32), axis=0, keepdims=True)

    need = (tpos + 1) > n_keep
    c0 = count_ge(jnp.zeros((1, TQ), jnp.int32))
    c1 = count_ge(jnp.full((1, TQ), KEY_OF_ONE_ULP, jnp.int32))
    positive = c1 >= k
    settled = ~need | ((c0 >= k) & (c1 < k)) | (c1 == k)
    thr = jnp.where(need, jnp.where(c1 == k, KEY_OF_ONE_ULP, 0), KEY_OF_NEG_MAX)
    lo = jnp.where(positive, KEY_OF_ONE_ULP, KEY_OF_NEG_MAX)
    hi = jnp.where(positive, KEY_OF_INF, 0)
    active = jnp.where(settled, 0, 1)

    def search_cond(st):
        return st[4] > 0

    def search_body(st):
        lo, hi, thr, active, _ = st
        for _ in range(SEARCH_STEPS_PER_CHECK):
            act = active > 0
            mid = lo + ((hi - lo) >> 1)
            collapsed = (hi - lo) <= 1
            c = count_ge(mid)
            found = c == k
            thr = jnp.where(act & found, mid, jnp.where(act & collapsed, lo, thr))
            go = act & ~found & ~collapsed
            lo = jnp.where(go & (c > k), mid, lo)
            hi = jnp.where(go & (c < k), mid, hi)
            active = jnp.where(go, 1, 0)
        return lo, hi, thr, active, jnp.sum(active)

    _, _, thr, _, _ = lax.while_loop(search_cond, search_body, (lo, hi, thr, active, jnp.sum(active)))
    needed = k - count_ge(thr + 1)

    ri = lax.broadcasted_iota(jnp.int32, (SUB, SUB), 0)
    rj = lax.broadcasted_iota(jnp.int32, (SUB, SUB), 1)
    strict_lower = jnp.where(rj < ri, 1.0, 0.0).astype(bf16)
    for h in range(N_HEADS_A):
        qc = qa_ref[:, (h // 2) * LANES:(h // 2 + 1) * LANES]
        qm_ref[h] = jnp.where(lane // HEAD_DIM == h % 2, qc, jnp.zeros_like(qc))

    def prepare_span(p, ties_before):
        for j in range(subs_per_span):
            keys = key_ref[sub_rows(p * subs_per_span + j), :]
            tie = keys == thr
            tie_f = jnp.where(tie, 1.0, 0.0)
            rank = jnp.dot(strict_lower, tie_f.astype(bf16), preferred_element_type=f32) + ties_before
            sel = (keys > thr) | (tie & (rank < needed))
            bias_ref[j * SUB:(j + 1) * SUB, :] = jnp.where(sel, 0.0, NEG).astype(bf16)
            ties_before = ties_before + jnp.sum(tie_f, axis=0, keepdims=True)
        return ties_before

    def block_operands(h, p, j):
        return ka_ref[sub_rows(p * subs_per_span + j), :], qm_ref[h], bias_ref[j * SUB:(j + 1) * SUB, :]

    outs = _attend_spans(prepare_span, block_operands, lambda h, c: vat_ref[c], s_ref, acc_ref, 0, n_span,
                         jnp.zeros((1, TQ), f32))
    o_ref[...] = jnp.concatenate(outs, axis=0).T


def _dsa_call(pr, B, S):
    TQ = min(QUERY_TILE, S)
    n_keep = min(TOPK_MAX, S // 4)
    n_chunks = S // KEY_CHUNK
    qspec = lambda w_: pl.BlockSpec((TQ, w_), lambda b, i: (b * (S // TQ) + i, 0))
    kspec = lambda w_: pl.BlockSpec((S, w_), lambda b, i: (b, 0))
    vspec = pl.BlockSpec((n_chunks, HEAD_DIM, KEY_CHUNK), lambda b, i: (b, 0, 0))
    return pl.pallas_call(
        functools.partial(_dsa_kernel, n_keep=n_keep),
        grid=(B, S // TQ),
        in_specs=[qspec(256), qspec(128), qspec(128), kspec(128), kspec(128), vspec],
        out_specs=qspec(256),
        out_shape=jax.ShapeDtypeStruct((B * S, 256), f32),
        scratch_shapes=[pltpu.VMEM((S, TQ), jnp.int32),
                        pltpu.VMEM((N_HEADS_A, SPAN_CHUNKS * KEY_CHUNK, TQ), bf16),
                        pltpu.VMEM((N_HEADS_A, HEAD_DIM + DENOM_ROWS, TQ), f32),
                        pltpu.VMEM((N_HEADS_A, TQ, LANES), bf16),
                        pltpu.VMEM((SPAN_CHUNKS * KEY_CHUNK, TQ), bf16)],
        compiler_params=pltpu.CompilerParams(dimension_semantics=("arbitrary", "arbitrary"),
                                             vmem_limit_bytes=VMEM_LIMIT),
        name="dsa",
    )(pr["qa"], pr["qidx"], pr["misc"], pr["kidx"], pr["ka"], pr["vat"])


def _compress_kernel(ch_ref, pe_ref, w_ref, kc_ref, vct_ref):
    nch = ch_ref.shape[1]
    rowid = lax.broadcasted_iota(jnp.int32, (nch, 1), 0)
    for kv in range(2):
        for g in range(N_KV_GROUPS_B):
            ch = ch_ref[kv * N_KV_GROUPS_B + g]
            top = jnp.dot((ch + pe_ref[kv, 0:1, :]).astype(bf16), w_ref[kv, 0], preferred_element_type=f32)
            bot = jnp.dot((ch + pe_ref[kv, 1:2, :]).astype(bf16), w_ref[kv, 1], preferred_element_type=f32)
            blk = top + pltpu.roll(bot, nch - 1, 0)
            blk = jnp.where(rowid < nch - 1, blk, 0.0)
            if kv == 0:
                kc_ref[:, g * LANES:(g + 1) * LANES] = blk.astype(bf16)
            else:
                vct_ref[g] = blk.T[0:HEAD_DIM, :].astype(bf16)


def _compress_call(kvcmp, pe_k, pe_v, w_k, w_v, B, S):
    nch = S // CMP_STRIDE
    G, dh = N_KV_GROUPS_B, HEAD_DIM
    ch = kvcmp.reshape(B, nch, CMP_STRIDE, 2 * G, dh).transpose(0, 3, 1, 2, 4)
    ch = ch.reshape(B, 2 * G, nch, CMP_STRIDE * dh)
    pe = jnp.stack([pe_k, pe_v]).reshape(2, 2, CMP_STRIDE * dh)
    w = jnp.stack([w_k, w_v]).reshape(2, 2, CMP_STRIDE * dh, dh)
    w = jnp.concatenate([w, w], axis=-1).astype(bf16)
    return pl.pallas_call(
        _compress_kernel,
        grid=(B,),
        in_specs=[pl.BlockSpec((None, 2 * G, nch, CMP_STRIDE * dh), lambda b: (b, 0, 0, 0)),
                  pl.BlockSpec(pe.shape, lambda b: (0, 0, 0)),
                  pl.BlockSpec(w.shape, lambda b: (0, 0, 0, 0))],
        out_specs=[pl.BlockSpec((nch, G * LANES), lambda b: (b, 0)),
                   pl.BlockSpec((G, dh, nch), lambda b: (b, 0, 0))],
        out_shape=[jax.ShapeDtypeStruct((B * nch, G * LANES), bf16),
                   jax.ShapeDtypeStruct((B * G, dh, nch), bf16)],
        compiler_params=pltpu.CompilerParams(dimension_semantics=("arbitrary",),
                                             vmem_limit_bytes=VMEM_LIMIT),
        name="compress",
    )(ch, pe, w)


def _split3(v):
    a = v.astype(bf16)
    r = v - a.astype(f32)
    b = r.astype(bf16)
    c = (r - b.astype(f32)).astype(bf16)
    return a, b, c


def _nsa_kernel(qn_ref, qr_ref, misc_ref, kc_ref, vct_ref, kslc_ref, vslct_ref, kwin_ref, vwint_ref,
                o_ref, s_ref, acc_ref, blk_ref, out_ref, qm_ref, bias_ref, *, n_sel):
    TQ = qn_ref.shape[0]
    CH = KEY_CHUNK
    SUB = LANES
    S = kslc_ref.shape[0]
    nch = kc_ref.shape[0]
    n_cmp = nch - 1
    n_blk = S // SLC_BLOCK
    G = N_KV_GROUPS_B
    R = N_HEADS_B // G
    q0 = pl.program_id(1) * TQ
    span = SPAN_CHUNKS * CH
    subs_per_span = span // SUB
    n_span = (q0 + TQ + span - 1) // span
    first_win = jnp.maximum(q0 - WINDOW, 0) // span
    tpos = q0 + lax.broadcasted_iota(jnp.int32, (1, TQ), 1)
    lane = lax.broadcasted_iota(jnp.int32, (1, LANES), 1)
    gate = jax.nn.sigmoid(misc_ref[...].T)

    def sub_rows(i):
        return pl.ds(pl.multiple_of(i * SUB, SUB), SUB)

    def gate_of(h, branch):
        a = GATE_LANE0 + 3 * h + branch
        return gate[a:a + 1, :]

    def head_q(ref, h):
        qc = ref[:, (h // 2) * LANES:(h // 2 + 1) * LANES]
        return jnp.where(lane // HEAD_DIM == h % 2, qc, jnp.zeros_like(qc))

    def attend(k_ref, vt_ref, biases_of, p_lo, p_hi):
        def prepare_span(p, extra):
            for j in range(subs_per_span):
                biases = biases_of(p * subs_per_span + j)
                for g in range(G):
                    bias_ref[g, j * SUB:(j + 1) * SUB, :] = biases[g]
            return extra

        def block_operands(h, p, j):
            g = h // R
            return (k_ref[sub_rows(p * subs_per_span + j), g * LANES:(g + 1) * LANES], qm_ref[h],
                    bias_ref[g, j * SUB:(j + 1) * SUB, :])

        def values_t(h, c):
            g = h // R
            return vt_ref[c, g * HEAD_DIM:(g + 1) * HEAD_DIM, :]

        return _attend_spans(prepare_span, block_operands, values_t, s_ref, acc_ref, p_lo, p_hi, 0)

    cpos = lax.broadcasted_iota(jnp.int32, (nch, 1), 0)
    cmask = (cpos * CMP_STRIDE + CMP_BLOCK - 1 <= tpos) & (cpos < n_cmp)
    oj = lax.broadcasted_iota(jnp.int32, (n_blk, nch), 0)
    oc = lax.broadcasted_iota(jnp.int32, (n_blk, nch), 1)
    overlap_t = ((oc * CMP_STRIDE <= oj * SLC_BLOCK + SLC_BLOCK - 1)
                 & (oc * CMP_STRIDE + CMP_BLOCK - 1 >= oj * SLC_BLOCK) & (oc < n_cmp))
    overlap_t = jnp.where(overlap_t, 1.0, 0.0).astype(bf16)
    bj = lax.broadcasted_iota(jnp.int32, (n_blk, 1), 0)
    cur = tpos // SLC_BLOCK
    forced = (bj == 0) | (bj == cur) | (bj == cur - 1)
    admissible = bj * SLC_BLOCK <= tpos

    for h in range(N_HEADS_B):
        qm_ref[h] = head_q(qr_ref, h)

    def window_biases(i):
        kpos = i * SUB + lax.broadcasted_iota(jnp.int32, (SUB, 1), 0)
        return [jnp.where((kpos <= tpos) & (kpos > tpos - WINDOW), 0.0, NEG).astype(bf16)] * G

    def selected_biases(i):
        per_sub = SUB // SLC_BLOCK
        kpos = i * SUB + lax.broadcasted_iota(jnp.int32, (SUB, 1), 0)
        biases = []
        for g in range(G):
            parts = []
            for jj in range(per_sub):
                b8 = blk_ref[g * n_blk + i * per_sub + jj]
                parts.extend([b8] * (SLC_BLOCK // b8.shape[0]))
            biases.append(jnp.where(kpos <= tpos, jnp.concatenate(parts, axis=0), NEG).astype(bf16))
        return biases

    all_heads = range(N_HEADS_B)
    lcs = [jnp.where(cmask, _nt_dot(kc_ref[:, (h // R) * LANES:(h // R + 1) * LANES], head_q(qn_ref, h)), NEG)
           for h in all_heads]
    cmaxes = [jnp.max(lc, axis=0, keepdims=True) for lc in lcs]
    es = [jnp.where(cmask, jnp.exp(lc - m), 0.0) for lc, m in zip(lcs, cmaxes)]
    sums = [jnp.sum(e, axis=0, keepdims=True) for e in es]
    ps = [e * (1.0 / jnp.where(l > 0.0, l, 1.0)) for e, l in zip(es, sums)]
    for h in all_heads:
        out_ref[h] = gate_of(h, 0) * jnp.dot(vct_ref[h // R], ps[h].astype(bf16), preferred_element_type=f32)
    psums = [sum(ps[g * R:(g + 1) * R]) for g in range(G)]
    s_blks = [sum(jnp.dot(overlap_t, piece, preferred_element_type=f32) for piece in _split3(psum))
              for psum in psums]
    s_blks = [jnp.where(forced, jnp.inf, jnp.where(admissible, s_blk, -jnp.inf)) for s_blk in s_blks]
    ranks = [jnp.zeros((n_blk, TQ), f32) for _ in range(G)]
    for i in range(n_blk):
        for g in range(G):
            row = s_blks[g][i:i + 1, :]
            beats = (row > s_blks[g]) | ((row == s_blks[g]) & (bj > i))
            ranks[g] = ranks[g] + jnp.where(beats, 1.0, 0.0)
    for g in range(G):
        chosen_bias = jnp.where(ranks[g] < float(n_sel), 0.0, NEG)
        for j in range(n_blk):
            blk_ref[g * n_blk + j] = jnp.broadcast_to(chosen_bias[j:j + 1, :], blk_ref.shape[1:])

    for branch, k_ref, vt_ref, biases_of, p_lo in ((1, kslc_ref, vslct_ref, selected_biases, 0),
                                                   (2, kwin_ref, vwint_ref, window_biases, first_win)):
        o_branch = attend(k_ref, vt_ref, biases_of, p_lo, n_span)
        for h in range(N_HEADS_B):
            out_ref[h] += gate_of(h, branch) * o_branch[h]
    o_ref[...] = jnp.concatenate([out_ref[h] for h in range(N_HEADS_B)], axis=0).T


def _nsa_call(pr, kc, vct, B, S):
    TQ = min(QUERY_TILE, S)
    n_sel = min(N_SLC, S // SLC_BLOCK)
    nch = S // CMP_STRIDE
    n_chunks = S // KEY_CHUNK
    G, R = N_KV_GROUPS_B, N_HEADS_B // N_KV_GROUPS_B
    qspec = lambda w_: pl.BlockSpec((TQ, w_), lambda b, i: (b * (S // TQ) + i, 0))
    kspec = lambda w_: pl.BlockSpec((S, w_), lambda b, i: (b, 0))
    vspec = pl.BlockSpec((n_chunks, G * HEAD_DIM, KEY_CHUNK), lambda b, i: (b, 0, 0))
    return pl.pallas_call(
        functools.partial(_nsa_kernel, n_sel=n_sel),
        grid=(B, S // TQ),
        in_specs=[qspec(512), qspec(512), qspec(128),
                  pl.BlockSpec((nch, G * LANES), lambda b, i: (b, 0)),
                  pl.BlockSpec((G, HEAD_DIM, nch), lambda b, i: (b, 0, 0)),
                  kspec(256), vspec, kspec(256), vspec],
        out_specs=qspec(512),
        out_shape=jax.ShapeDtypeStruct((B * S, 512), f32),
        scratch_shapes=[pltpu.VMEM((N_HEADS_B, SPAN_CHUNKS * KEY_CHUNK, TQ), bf16),
                        pltpu.VMEM((N_HEADS_B, HEAD_DIM + DENOM_ROWS, TQ), f32),
                        pltpu.VMEM((G * (S // SLC_BLOCK), 8, TQ), f32),
                        pltpu.VMEM((N_HEADS_B, HEAD_DIM, TQ), f32),
                        pltpu.VMEM((N_HEADS_B, TQ, LANES), bf16),
                        pltpu.VMEM((G, SPAN_CHUNKS * KEY_CHUNK, TQ), bf16)],
        compiler_params=pltpu.CompilerParams(dimension_semantics=("arbitrary", "arbitrary"),
                                             vmem_limit_bytes=VMEM_LIMIT),
        name="nsa",
    )(pr["qbn"], pr["qbr"], pr["misc"], kc, vct, pr["kslc"], pr["vslct"], pr["kwin"], pr["vwint"])


def _pool_kernel(u_ref, w_ref, o_ref):
    u = u_ref[...]
    S, C = u.shape
    row = lax.broadcasted_iota(jnp.int32, (S, 1), 0)
    grp = lax.broadcasted_iota(jnp.int32, (1, C), 1) // POOL_GROUP_DIM
    run, width = u, 1
    win_sum = jnp.zeros_like(u)
    win_len = jnp.zeros((1, C), f32)
    for gi, w in enumerate(POOL_WINDOWS):
        while width < w:
            run = run + jnp.where(row >= width, pltpu.roll(run, width, 0), 0.0)
            width *= 2
        assert width == w, "pooling windows must be increasing powers of two"
        win_sum = jnp.where(grp == gi, run, win_sum)
        win_len = jnp.where(grp == gi, float(w), win_len)
    pooled = win_sum / jnp.minimum((row + 1).astype(f32), win_len) - u
    o_ref[...] = jnp.dot(pooled.astype(bf16), w_ref[...], preferred_element_type=f32)


def _pool_call(uc, w_pool, B, S):
    C = N_POOL_GROUPS * POOL_GROUP_DIM
    wbd = jnp.zeros((C, C), f32)
    for gi in range(N_POOL_GROUPS):
        sl = slice(gi * POOL_GROUP_DIM, (gi + 1) * POOL_GROUP_DIM)
        wbd = wbd.at[sl, sl].set(w_pool[gi])
    spec = pl.BlockSpec((S, C), lambda b: (b, 0))
    return pl.pallas_call(
        _pool_kernel,
        grid=(B,),
        in_specs=[spec, pl.BlockSpec((C, C), lambda b: (0, 0))],
        out_specs=spec,
        out_shape=jax.ShapeDtypeStruct((B * S, C), f32),
        compiler_params=pltpu.CompilerParams(dimension_semantics=("arbitrary",),
                                             vmem_limit_bytes=VMEM_LIMIT),
        name="pool",
    )(uc, wbd.astype(bf16))


def _merge_mlp_kernel(x_ref, oa_ref, ob_ref, oc_ref, ga_ref, gb_ref, ps_ref, wo_ref, gpm_ref,
                      gmlp_ref, wup_ref, wdn_ref, gpost_ref, o_ref):
    mixed = jnp.concatenate([_rms(oa_ref[...], ga_ref[...]), _rms(ob_ref[...], gb_ref[...]),
                             _rms(oc_ref[...]) * ps_ref[...]], axis=1).astype(bf16)
    x = x_ref[...] + _rms(jnp.dot(mixed, wo_ref[...], preferred_element_type=f32), gpm_ref[...])
    h = _rms(x, gmlp_ref[...]).astype(bf16)
    up = jnp.maximum(jnp.dot(h, wup_ref[...], preferred_element_type=f32), 0.0)
    f = jnp.dot((up * up).astype(bf16), wdn_ref[...], preferred_element_type=f32)
    o_ref[...] = x + _rms(f, gpost_ref[...])


def _merge_mlp_call(x, oa, ob, oc, ga, gb, ps, wo, gpm, gmlp, wup, wdn, gpost):
    T, D = x.shape
    TM = TOKEN_TILE
    row = lambda a: pl.BlockSpec((TM, a.shape[1]), lambda i: (i, 0))
    full = lambda a: pl.BlockSpec(a.shape, lambda i: (0, 0), pipeline_mode=pl.Buffered(1))
    args = (x, oa, ob, oc, ga, gb, ps, wo, gpm, gmlp, wup, wdn, gpost)
    return pl.pallas_call(
        _merge_mlp_kernel,
        grid=(T // TM,),
        in_specs=[row(a) for a in args[:4]] + [full(a) for a in args[4:]],
        out_specs=pl.BlockSpec((TM, D), lambda i: (i, 0)),
        out_shape=jax.ShapeDtypeStruct((T, D), f32),
        compiler_params=pltpu.CompilerParams(dimension_semantics=("arbitrary",),
                                             vmem_limit_bytes=VMEM_LIMIT),
        name="merge_mlp",
    )(*args)


def _rope_lane_tables(positions, rot_dim, period):
    half = rot_dim // 2
    inv_freq = ROPE_THETA ** (-jnp.arange(0, rot_dim, 2, dtype=f32) / rot_dim)
    ang = positions.astype(f32).reshape(-1, 1) * inv_freq
    cos, sin = jnp.cos(ang), jnp.sin(ang)
    T = ang.shape[0]
    ones, zeros, zh = jnp.ones((T, period - rot_dim), f32), jnp.zeros((T, period - rot_dim), f32), jnp.zeros((T, half), f32)
    c = jnp.concatenate([cos, cos, ones], axis=1)
    sa = jnp.concatenate([-sin, zh, zeros], axis=1)
    sb = jnp.concatenate([zh, sin, zeros], axis=1)
    reps = LANES // period
    return jnp.concatenate([jnp.tile(t, (1, reps)) for t in (c, sa, sb)], axis=1)


def _dup_halves(w):
    D = w.shape[0]
    g = w.reshape(D, -1, 1, HEAD_DIM)
    return jnp.broadcast_to(g, (D, g.shape[1], 2, HEAD_DIM)).reshape(D, -1)


def _layout_w_in(w_in):
    D = w_in.shape[0]
    widths = (256, 128, 128, 32, 4, 512, 128, 128, 128, 128, 128, 128, 24, 256)
    offs = [0]
    for w_ in widths:
        offs.append(offs[-1] + w_)
    (q_a, ckv, q_idx, k_idx, w_idx, q_b, k_cmp, v_cmp, k_slc, v_slc, k_win, v_win, gates, u_c) = [
        w_in[:, offs[i]:offs[i + 1]] for i in range(len(widths))]
    misc = jnp.concatenate([w_idx, gates, jnp.zeros((D, LANES - 28), w_in.dtype)], axis=1)
    cols = [q_a, ckv, q_idx, jnp.tile(k_idx, (1, LANES // IDX_DIM)), q_b, k_cmp, v_cmp,
            _dup_halves(k_slc), v_slc, _dup_halves(k_win), v_win, misc, u_c]
    w = jnp.concatenate(cols, axis=1)
    assert w.shape[1] == N_PROJ
    return w.astype(bf16)


def kernel(x, positions, w_in, g_kv_a, w_kv_up_a, w_cmp_k, w_cmp_v, pe_cmp_k, pe_cmp_v, w_pool,
           pool_scale, g_out_a, g_out_b, w_out, g_pre_mix, g_post_mix, g_pre_mlp, g_post_mlp,
           w_up, w_down):
    B, S, D = x.shape
    depth = w_in.shape[0]
    assert S % (SPAN_CHUNKS * KEY_CHUNK) == 0 and QUERY_TILE == KEY_CHUNK
    assert (B * S) % TOKEN_TILE == 0
    rope_h = _rope_lane_tables(positions, ROT_DIM_HEAD, HEAD_DIM)
    rope_i = _rope_lane_tables(positions, ROT_DIM_IDX, IDX_DIM)
    row = lambda v: v.reshape(1, -1)
    xt = x.reshape(B * S, D)
    for l in range(depth):
        wkv = jnp.concatenate([_dup_halves(w_kv_up_a[l][:, :HEAD_DIM]), w_kv_up_a[l][:, HEAD_DIM:],
                               jnp.zeros((KV_RANK_A, LANES - HEAD_DIM), f32)], axis=1).astype(bf16)
        pr = _proj_call(xt, row(g_pre_mix[l]), _layout_w_in(w_in[l]), row(g_kv_a[l]), wkv, rope_h, rope_i)
        o_a = _dsa_call(pr, B, S)
        kc, vc = _compress_call(pr["kvcmp"], pe_cmp_k[l], pe_cmp_v[l], w_cmp_k[l], w_cmp_v[l], B, S)
        o_b = _nsa_call(pr, kc, vc, B, S)
        o_c = _pool_call(pr["uc"], w_pool[l], B, S)
        xt = _merge_mlp_call(xt, o_a, o_b, o_c, row(g_out_a[l]), row(g_out_b[l]), row(pool_scale[l]),
                             w_out[l].astype(bf16), row(g_post_mix[l]), row(g_pre_mlp[l]),
                             w_up[l].astype(bf16), w_down[l].astype(bf16), row(g_post_mlp[l]))
    return xt.reshape(B, S, D)
```

```python
import functools

import jax
import jax.numpy as jnp
from jax import lax
from jax.experimental import pallas as pl
from jax.experimental.pallas import tpu as pltpu

f32 = jnp.float32
bf16 = jnp.bfloat16

HEAD_DIM = 64
N_HEADS_A = 4
N_HEADS_B = 8
N_KV_GROUPS_B = 2
N_POOL_GROUPS = 4
POOL_GROUP_DIM = 64
POOL_WINDOWS = (2, 4, 8, 16)
KV_RANK_A = 128
N_IDX_HEADS = 4
IDX_DIM = 32
TOPK_MAX = 256
CMP_BLOCK = 32
CMP_STRIDE = 16
SLC_BLOCK = 64
N_SLC = 16
WINDOW = 512
ROPE_THETA = 500000.0
ROT_DIM_HEAD = HEAD_DIM // 4
ROT_DIM_IDX = IDX_DIM // 4
EPS = 1e-6
NEG = -1e30
SCALE = HEAD_DIM ** -0.5

LANES = 128
TOKEN_TILE = 512
QUERY_TILE = 256
KEY_CHUNK = 256
BF16_ROWS = 16
DENOM_ROWS = BF16_ROWS
LOG2E = 1.4426950408889634
LOGITS_LEAD = 3
SPAN_CHUNKS = 2
VMEM_LIMIT = 56 * 1024 * 1024

_COLS = {}
_off = 0
for _name, _w in (("qa", 256), ("ckv", 128), ("qidx", 128), ("kidx", 128), ("qb", 512),
                  ("kvcmp", 256), ("kslc", 256), ("vslc", 128), ("kwin", 256), ("vwin", 128),
                  ("misc", 128), ("uc", 256)):
    _COLS[_name] = (_off, _off + _w)
    _off += _w
N_PROJ = _off
GATE_LANE0 = N_IDX_HEADS


def _rms(v, gain=None):
    y = v * lax.rsqrt(jnp.mean(v * v, axis=-1, keepdims=True) + EPS)
    return y if gain is None else y * gain


def _rope128(v, c, sa, sb, half):
    return v * c + pltpu.roll(v, LANES - half, 1) * sa + pltpu.roll(v, half, 1) * sb


def _nt_dot(a, b):
    return lax.dot_general(a, b, (((1,), (1,)), ((), ())), preferred_element_type=f32)


def _proj_kernel(x_ref, g_ref, w_ref, gkv_ref, wkv_ref, rh_ref, ri_ref,
                 qa_ref, ka_ref, vat_ref, qidx_ref, kidx_ref, qbn_ref, qbr_ref,
                 kvcmp_ref, kslc_ref, vslct_ref, kwin_ref, vwint_ref, misc_ref, uc_ref):
    h = _rms(x_ref[...], g_ref[...])
    p = jnp.dot(h.astype(bf16), w_ref[...], preferred_element_type=f32)
    hc, hsa, hsb = rh_ref[:, 0:128], rh_ref[:, 128:256], rh_ref[:, 256:384]
    ic, isa, isb = ri_ref[:, 0:128], ri_ref[:, 128:256], ri_ref[:, 256:384]
    hh, ih = ROT_DIM_HEAD // 2, ROT_DIM_IDX // 2

    def col(name, j=0):
        a = _COLS[name][0] + j * LANES
        return p[:, a:a + LANES]

    for j in range(2):
        qa_ref[:, j * LANES:(j + 1) * LANES] = (_rope128(col("qa", j), hc, hsa, hsb, hh)
                                                * (SCALE * LOG2E)).astype(bf16)
    kv = jnp.dot(_rms(col("ckv"), gkv_ref[...]).astype(bf16), wkv_ref[...], preferred_element_type=f32)
    ka_ref[...] = _rope128(kv[:, 0:LANES], hc, hsa, hsb, hh).astype(bf16)
    for cc in range(vat_ref.shape[0]):
        tok = slice(cc * KEY_CHUNK, (cc + 1) * KEY_CHUNK)
        vat_ref[cc] = kv[tok, LANES:2 * LANES].T[0:HEAD_DIM, :].astype(bf16)
        vslct_ref[cc] = col("vslc")[tok, :].T.astype(bf16)
        vwint_ref[cc] = col("vwin")[tok, :].T.astype(bf16)
    qidx_ref[...] = _rope128(col("qidx"), ic, isa, isb, ih).astype(bf16)
    kidx_ref[...] = _rope128(col("kidx"), ic, isa, isb, ih).astype(bf16)
    for j in range(4):
        q = col("qb", j)
        qbn_ref[:, j * LANES:(j + 1) * LANES] = (q * SCALE).astype(bf16)
        qbr_ref[:, j * LANES:(j + 1) * LANES] = (_rope128(q, hc, hsa, hsb, hh) * (SCALE * LOG2E)).astype(bf16)
    for j in range(2):
        sl = slice(j * LANES, (j + 1) * LANES)
        kvcmp_ref[:, sl] = col("kvcmp", j)
        kslc_ref[:, sl] = _rope128(col("kslc", j), hc, hsa, hsb, hh).astype(bf16)
        kwin_ref[:, sl] = _rope128(col("kwin", j), hc, hsa, hsb, hh).astype(bf16)
        uc_ref[:, sl] = col("uc", j)
    misc_ref[...] = col("misc")


def _proj_call(x, g, w, gkv, wkv, rope_h, rope_i):
    T, D = x.shape
    TM = TOKEN_TILE
    row = lambda w_: pl.BlockSpec((TM, w_), lambda i: (i, 0))
    full = lambda a: pl.BlockSpec(a.shape, lambda i: (0, 0))
    outs = [("qa", 256, bf16), ("ka", 128, bf16), ("vat", -HEAD_DIM, bf16), ("qidx", 128, bf16),
            ("kidx", 128, bf16), ("qbn", 512, bf16), ("qbr", 512, bf16), ("kvcmp", 256, f32),
            ("kslc", 256, bf16), ("vslct", -LANES, bf16), ("kwin", 256, bf16), ("vwint", -LANES, bf16),
            ("misc", 128, f32), ("uc", 256, f32)]
    tspec = lambda d: pl.BlockSpec((TM // KEY_CHUNK, d, KEY_CHUNK), lambda i: (i, 0, 0))
    tshape = lambda d: jax.ShapeDtypeStruct((T // KEY_CHUNK, d, KEY_CHUNK), bf16)
    res = pl.pallas_call(
        _proj_kernel,
        grid=(T // TM,),
        in_specs=[row(D), full(g), full(w), full(gkv), full(wkv), row(384), row(384)],
        out_specs=[tspec(-w_) if w_ < 0 else row(w_) for _, w_, _ in outs],
        out_shape=[tshape(-w_) if w_ < 0 else jax.ShapeDtypeStruct((T, w_), dt) for _, w_, dt in outs],
        compiler_params=pltpu.CompilerParams(dimension_semantics=("arbitrary",),
                                             vmem_limit_bytes=VMEM_LIMIT),
        name="proj",
    )(x, g, w, gkv, wkv, rope_h, rope_i)
    return dict(zip([n for n, _, _ in outs], res))


KEY_OF_NEG_MAX = -(2 ** 31 - 2 ** 23)
HALF_BITS = 16
COUNT_SLAB = 32
COUNT16_SLAB = 64
SEARCH_STEPS_PER_CHECK = 4


def _column_max(s):
    m = s[0:BF16_ROWS, :]
    for r in range(1, s.shape[0] // BF16_ROWS):
        m = jnp.maximum(m, s[r * BF16_ROWS:(r + 1) * BF16_ROWS, :])
    return jnp.max(m.astype(f32), axis=0, keepdims=True)


def _attend_spans(prepare_span, block_operands, values_t, s_ref, acc_ref, p_lo, p_hi, extra):
    n_heads = s_ref.shape[0]
    TQ = s_ref.shape[2]
    span = SPAN_CHUNKS * KEY_CHUNK
    ones = jnp.ones((acc_ref.shape[1] - HEAD_DIM, span), bf16)
    acc_ref[...] = jnp.zeros_like(acc_ref)

    def step(p, st):
        maxes, extra = st
        extra = prepare_span(p, extra)
        new_maxes = [None] * n_heads

        def logits(h):
            blocks = [block_operands(h, p, j) for j in range(span // LANES)]
            raw = [_nt_dot(kb, q) for kb, q, _ in blocks]
            masked = [r.astype(bf16) + bias for r, (_, _, bias) in zip(raw, blocks)]
            for j, s in enumerate(masked):
                s_ref[h, j * LANES:(j + 1) * LANES, :] = s
            m_new = maxes[h]
            for s in masked:
                m_new = jnp.maximum(m_new, _column_max(s))
            new_maxes[h] = m_new

        def weigh(h):
            vt = jnp.concatenate([values_t(h, SPAN_CHUNKS * p + j) for j in range(SPAN_CHUNKS)], axis=1)
            e = jnp.exp2(s_ref[h] - new_maxes[h].astype(bf16))
            pv = jnp.dot(jnp.concatenate([vt, ones], axis=0), e, preferred_element_type=f32)
            acc_ref[h] = jnp.exp2(maxes[h] - new_maxes[h]) * acc_ref[h] + pv

        for h in range(n_heads + LOGITS_LEAD):
            if h < n_heads:
                logits(h)
            if h >= LOGITS_LEAD:
                weigh(h - LOGITS_LEAD)
        return tuple(new_maxes), extra

    lax.fori_loop(p_lo, p_hi, step, ((jnp.full((1, TQ), -3.0e38, f32),) * n_heads, extra))
    return [acc_ref[h, 0:HEAD_DIM, :] / acc_ref[h, HEAD_DIM:HEAD_DIM + 1, :] for h in range(n_heads)]


def _ordered_key(v):
    bits = lax.bitcast_convert_type(v, jnp.int32)
    return bits ^ ((bits >> 31) & 0x7FFFFFFF)


def _dsa_kernel(qa_ref, qidx_ref, misc_ref, kidx_ref, ka_ref, vat_ref, o_ref,
                key_ref, s_ref, acc_ref, qm_ref, bias_ref, k16_ref, *, n_keep):
    TQ = qa_ref.shape[0]
    CH = KEY_CHUNK
    q0 = pl.program_id(1) * TQ
    n_ch = (q0 + TQ + CH - 1) // CH
    tpos = q0 + lax.broadcasted_iota(jnp.int32, (1, TQ), 1)
    lane = lax.broadcasted_iota(jnp.int32, (1, LANES), 1)
    k = float(n_keep)

    SUB = LANES
    subs_per_span = SPAN_CHUNKS * CH // SUB
    n_span = (n_ch + SPAN_CHUNKS - 1) // SPAN_CHUNKS

    def sub_rows(i):
        return pl.ds(pl.multiple_of(i * SUB, SUB), SUB)

    w_t = misc_ref[...].T[0:8, :]

    def index_q(h):
        qidx = qidx_ref[...]
        return jnp.where(lane // IDX_DIM == h, qidx, jnp.zeros_like(qidx))

    def score_span(p, carry):
        for i in [p * subs_per_span + j for j in range(subs_per_span)]:
            kb = kidx_ref[sub_rows(i), :]
            sc = jnp.zeros((SUB, TQ), f32)
            for h in range(N_IDX_HEADS):
                sc = sc + jnp.maximum(_nt_dot(kb, index_q(h)), 0.0) * w_t[h:h + 1, :]
            kpos = i * SUB + lax.broadcasted_iota(jnp.int32, (SUB, 1), 0)
            sc = jnp.where(sc == 0.0, 0.0, sc)
            key = _ordered_key(jnp.where(kpos <= tpos, sc, -jnp.inf))
            key_ref[sub_rows(i), :] = key
            k16_ref[sub_rows(i), :] = (key >> HALF_BITS).astype(jnp.int16)
        return carry

    lax.fori_loop(0, n_span, score_span, 0)

    def count_ge(v):
        vs = jnp.broadcast_to(v, (COUNT_SLAB, TQ))

        def chunk(c, acc):
            base = pl.multiple_of(c * CH, CH)
            for r in range(CH // COUNT_SLAB):
                ks = key_ref[pl.ds(base + r * COUNT_SLAB, COUNT_SLAB), :]
                acc = acc + jnp.where(ks >= vs, 1, 0)
            return acc

        acc = lax.fori_loop(0, n_ch, chunk, jnp.zeros((COUNT_SLAB, TQ), jnp.int32))
        return jnp.sum(acc.astype(f32), axis=0, keepdims=True)

    def count16_ge(v):
        vs = jnp.broadcast_to(v.astype(jnp.int16), (COUNT16_SLAB, TQ))
        one, zero = jnp.int16(1), jnp.int16(0)

        def chunk(c, acc):
            base = pl.multiple_of(c * CH, CH)
            for r in range(CH // COUNT16_SLAB):
                ks = k16_ref[pl.ds(base + r * COUNT16_SLAB, COUNT16_SLAB), :]
                acc = acc + jnp.where(ks >= vs, one, zero)
            return acc

        acc = lax.fori_loop(0, n_ch, chunk, jnp.zeros((COUNT16_SLAB, TQ), jnp.int16))
        return jnp.sum(acc.astype(jnp.int32).astype(f32), axis=0, keepdims=True)

    def bisect16(st):
        lo, hi, at, hit, active = st
        act = active > 0
        mid = (lo + hi) >> 1
        collapsed = (hi - lo) <= 1
        c = count16_ge(mid)
        exact = act & ~collapsed & (c == k)
        at = jnp.where(exact, mid, at)
        hit = jnp.where(exact, 1, hit)
        go = act & ~collapsed & ~exact
        lo = jnp.where(go & (c > k), mid, lo)
        hi = jnp.where(go & (c < k), mid, hi)
        return lo, hi, at, hit, jnp.where(go, 1, 0)

    need = (tpos + 1) > n_keep
    half = 1 << (HALF_BITS - 1)
    lo0 = jnp.full((1, TQ), -half, jnp.int32)
    hi0 = jnp.full((1, TQ), half, jnp.int32)
    zero = jnp.zeros((1, TQ), jnp.int32)
    st = (lo0, hi0, zero, zero, jnp.where(need, 1, 0))
    st = lax.fori_loop(0, HALF_BITS, lambda _, s: bisect16(s), st)
    prefix, at_hi, hit_hi = st[0], st[2], st[3] > 0

    def low_keys_span(p, carry):
        for i in [p * subs_per_span + j for j in range(subs_per_span)]:
            key = key_ref[sub_rows(i), :]
            hi_half = key >> HALF_BITS
            low = (key & ((1 << HALF_BITS) - 1)) - half
            low = jnp.where(hi_half > prefix, half - 1, jnp.where(hi_half < prefix, -half, low))
            k16_ref[sub_rows(i), :] = low.astype(jnp.int16)
        return carry

    lax.fori_loop(0, n_span, low_keys_span, 0)

    def low_cond(st):
        return st[5] > 0

    def low_body(st):
        st = st[:5]
        for _ in range(SEARCH_STEPS_PER_CHECK):
            st = bisect16(st)
        return st + (jnp.sum(st[4]),)

    active = jnp.where(need & ~hit_hi, 1, 0)
    st = lax.while_loop(low_cond, low_body, (lo0, hi0, zero, zero, active, jnp.sum(active)))
    low_thr = jnp.where(st[3] > 0, st[2], st[0]) + half
    thr = jnp.where(hit_hi, at_hi << HALF_BITS, (prefix << HALF_BITS) | low_thr)
    thr = jnp.where(need, thr, KEY_OF_NEG_MAX)
    needed = k - count_ge(thr + 1)

    ri = lax.broadcasted_iota(jnp.int32, (SUB, SUB), 0)
    rj = lax.broadcasted_iota(jnp.int32, (SUB, SUB), 1)
    strict_lower = jnp.where(rj < ri, 1.0, 0.0).astype(bf16)
    for h in range(N_HEADS_A):
        qc = qa_ref[:, (h // 2) * LANES:(h // 2 + 1) * LANES]
        qm_ref[h] = jnp.where(lane // HEAD_DIM == h % 2, qc, jnp.zeros_like(qc))

    def prepare_span(p, ties_before):
        for j in range(subs_per_span):
            keys = key_ref[sub_rows(p * subs_per_span + j), :]
            tie = keys == thr
            tie_f = jnp.where(tie, 1.0, 0.0)
            rank = jnp.dot(strict_lower, tie_f.astype(bf16), preferred_element_type=f32) + ties_before
            sel = (keys > thr) | (tie & (rank < needed))
            bias_ref[j * SUB:(j + 1) * SUB, :] = jnp.where(sel, 0.0, NEG).astype(bf16)
            ties_before = ties_before + jnp.sum(tie_f, axis=0, keepdims=True)
        return ties_before

    def block_operands(h, p, j):
        return ka_ref[sub_rows(p * subs_per_span + j), :], qm_ref[h], bias_ref[j * SUB:(j + 1) * SUB, :]

    outs = _attend_spans(prepare_span, block_operands, lambda h, c: vat_ref[c], s_ref, acc_ref, 0, n_span,
                         jnp.zeros((1, TQ), f32))
    o_ref[...] = jnp.concatenate(outs, axis=0).T


def _dsa_call(pr, B, S):
    TQ = min(QUERY_TILE, S)
    n_keep = min(TOPK_MAX, S // 4)
    n_chunks = S // KEY_CHUNK
    qspec = lambda w_: pl.BlockSpec((TQ, w_), lambda b, i: (b * (S // TQ) + i, 0))
    kspec = lambda w_: pl.BlockSpec((S, w_), lambda b, i: (b, 0))
    vspec = pl.BlockSpec((n_chunks, HEAD_DIM, KEY_CHUNK), lambda b, i: (b, 0, 0))
    return pl.pallas_call(
        functools.partial(_dsa_kernel, n_keep=n_keep),
        grid=(B, S // TQ),
        in_specs=[qspec(256), qspec(128), qspec(128), kspec(128), kspec(128), vspec],
        out_specs=qspec(256),
        out_shape=jax.ShapeDtypeStruct((B * S, 256), f32),
        scratch_shapes=[pltpu.VMEM((S, TQ), jnp.int32),
                        pltpu.VMEM((N_HEADS_A, SPAN_CHUNKS * KEY_CHUNK, TQ), bf16),
                        pltpu.VMEM((N_HEADS_A, HEAD_DIM + DENOM_ROWS, TQ), f32),
                        pltpu.VMEM((N_HEADS_A, TQ, LANES), bf16),
                        pltpu.VMEM((SPAN_CHUNKS * KEY_CHUNK, TQ), bf16),
                        pltpu.VMEM((S, TQ), jnp.int16)],
        compiler_params=pltpu.CompilerParams(dimension_semantics=("arbitrary", "arbitrary"),
                                             vmem_limit_bytes=VMEM_LIMIT),
        name="dsa",
    )(pr["qa"], pr["qidx"], pr["misc"], pr["kidx"], pr["ka"], pr["vat"])


def _compress_kernel(ch_ref, pe_ref, w_ref, kc_ref, vct_ref):
    nch = ch_ref.shape[1]
    rowid = lax.broadcasted_iota(jnp.int32, (nch, 1), 0)
    for kv in range(2):
        for g in range(N_KV_GROUPS_B):
            ch = ch_ref[kv * N_KV_GROUPS_B + g]
            top = jnp.dot((ch + pe_ref[kv, 0:1, :]).astype(bf16), w_ref[kv, 0], preferred_element_type=f32)
            bot = jnp.dot((ch + pe_ref[kv, 1:2, :]).astype(bf16), w_ref[kv, 1], preferred_element_type=f32)
            blk = top + pltpu.roll(bot, nch - 1, 0)
            blk = jnp.where(rowid < nch - 1, blk, 0.0)
            if kv == 0:
                kc_ref[:, g * LANES:(g + 1) * LANES] = blk.astype(bf16)
            else:
                vct_ref[g] = blk.T[0:HEAD_DIM, :].astype(bf16)


def _compress_call(kvcmp, pe_k, pe_v, w_k, w_v, B, S):
    nch = S // CMP_STRIDE
    G, dh = N_KV_GROUPS_B, HEAD_DIM
    ch = kvcmp.reshape(B, nch, CMP_STRIDE, 2 * G, dh).transpose(0, 3, 1, 2, 4)
    ch = ch.reshape(B, 2 * G, nch, CMP_STRIDE * dh)
    pe = jnp.stack([pe_k, pe_v]).reshape(2, 2, CMP_STRIDE * dh)
    w = jnp.stack([w_k, w_v]).reshape(2, 2, CMP_STRIDE * dh, dh)
    w = jnp.concatenate([w, w], axis=-1).astype(bf16)
    return pl.pallas_call(
        _compress_kernel,
        grid=(B,),
        in_specs=[pl.BlockSpec((None, 2 * G, nch, CMP_STRIDE * dh), lambda b: (b, 0, 0, 0)),
                  pl.BlockSpec(pe.shape, lambda b: (0, 0, 0)),
                  pl.BlockSpec(w.shape, lambda b: (0, 0, 0, 0))],
        out_specs=[pl.BlockSpec((nch, G * LANES), lambda b: (b, 0)),
                   pl.BlockSpec((G, dh, nch), lambda b: (b, 0, 0))],
        out_shape=[jax.ShapeDtypeStruct((B * nch, G * LANES), bf16),
                   jax.ShapeDtypeStruct((B * G, dh, nch), bf16)],
        compiler_params=pltpu.CompilerParams(dimension_semantics=("arbitrary",),
                                             vmem_limit_bytes=VMEM_LIMIT),
        name="compress",
    )(ch, pe, w)


def _split3(v):
    a = v.astype(bf16)
    r = v - a.astype(f32)
    b = r.astype(bf16)
    c = (r - b.astype(f32)).astype(bf16)
    return a, b, c


def _nsa_kernel(qn_ref, qr_ref, misc_ref, kc_ref, vct_ref, kslc_ref, vslct_ref, kwin_ref, vwint_ref,
                o_ref, s_ref, acc_ref, blk_ref, out_ref, qm_ref, bias_ref, *, n_sel):
    TQ = qn_ref.shape[0]
    CH = KEY_CHUNK
    SUB = LANES
    S = kslc_ref.shape[0]
    nch = kc_ref.shape[0]
    n_cmp = nch - 1
    n_blk = S // SLC_BLOCK
    G = N_KV_GROUPS_B
    R = N_HEADS_B // G
    q0 = pl.program_id(1) * TQ
    span = SPAN_CHUNKS * CH
    subs_per_span = span // SUB
    n_span = (q0 + TQ + span - 1) // span
    first_win = jnp.maximum(q0 - WINDOW, 0) // span
    tpos = q0 + lax.broadcasted_iota(jnp.int32, (1, TQ), 1)
    lane = lax.broadcasted_iota(jnp.int32, (1, LANES), 1)
    gate = jax.nn.sigmoid(misc_ref[...].T)

    def sub_rows(i):
        return pl.ds(pl.multiple_of(i * SUB, SUB), SUB)

    def gate_of(h, branch):
        a = GATE_LANE0 + 3 * h + branch
        return gate[a:a + 1, :]

    def head_q(ref, h):
        qc = ref[:, (h // 2) * LANES:(h // 2 + 1) * LANES]
        return jnp.where(lane // HEAD_DIM == h % 2, qc, jnp.zeros_like(qc))

    def attend(k_ref, vt_ref, biases_of, p_lo, p_hi):
        def prepare_span(p, extra):
            for j in range(subs_per_span):
                biases = biases_of(p * subs_per_span + j)
                for g in range(G):
                    bias_ref[g, j * SUB:(j + 1) * SUB, :] = biases[g]
            return extra

        def block_operands(h, p, j):
            g = h // R
            return (k_ref[sub_rows(p * subs_per_span + j), g * LANES:(g + 1) * LANES], qm_ref[h],
                    bias_ref[g, j * SUB:(j + 1) * SUB, :])

        def values_t(h, c):
            g = h // R
            return vt_ref[c, g * HEAD_DIM:(g + 1) * HEAD_DIM, :]

        return _attend_spans(prepare_span, block_operands, values_t, s_ref, acc_ref, p_lo, p_hi, 0)

    cpos = lax.broadcasted_iota(jnp.int32, (nch, 1), 0)
    cmask = (cpos * CMP_STRIDE + CMP_BLOCK - 1 <= tpos) & (cpos < n_cmp)
    oj = lax.broadcasted_iota(jnp.int32, (n_blk, nch), 0)
    oc = lax.broadcasted_iota(jnp.int32, (n_blk, nch), 1)
    overlap_t = ((oc * CMP_STRIDE <= oj * SLC_BLOCK + SLC_BLOCK - 1)
                 & (oc * CMP_STRIDE + CMP_BLOCK - 1 >= oj * SLC_BLOCK) & (oc < n_cmp))
    overlap_t = jnp.where(overlap_t, 1.0, 0.0).astype(bf16)
    bj = lax.broadcasted_iota(jnp.int32, (n_blk, 1), 0)
    cur = tpos // SLC_BLOCK
    forced = (bj == 0) | (bj == cur) | (bj == cur - 1)
    admissible = bj * SLC_BLOCK <= tpos

    for h in range(N_HEADS_B):
        qm_ref[h] = head_q(qr_ref, h)

    def window_biases(i):
        kpos = i * SUB + lax.broadcasted_iota(jnp.int32, (SUB, 1), 0)
        return [jnp.where((kpos <= tpos) & (kpos > tpos - WINDOW), 0.0, NEG).astype(bf16)] * G

    def selected_biases(i):
        per_sub = SUB // SLC_BLOCK
        kpos = i * SUB + lax.broadcasted_iota(jnp.int32, (SUB, 1), 0)
        biases = []
        for g in range(G):
            parts = []
            for jj in range(per_sub):
                b8 = blk_ref[g * n_blk + i * per_sub + jj]
                parts.extend([b8] * (SLC_BLOCK // b8.shape[0]))
            biases.append(jnp.where(kpos <= tpos, jnp.concatenate(parts, axis=0), NEG).astype(bf16))
        return biases

    all_heads = range(N_HEADS_B)
    lcs = [jnp.where(cmask, _nt_dot(kc_ref[:, (h // R) * LANES:(h // R + 1) * LANES], head_q(qn_ref, h)), NEG)
           for h in all_heads]
    cmaxes = [jnp.max(lc, axis=0, keepdims=True) for lc in lcs]
    es = [jnp.where(cmask, jnp.exp(lc - m), 0.0) for lc, m in zip(lcs, cmaxes)]
    sums = [jnp.sum(e, axis=0, keepdims=True) for e in es]
    ps = [e * (1.0 / jnp.where(l > 0.0, l, 1.0)) for e, l in zip(es, sums)]
    for h in all_heads:
        out_ref[h] = gate_of(h, 0) * jnp.dot(vct_ref[h // R], ps[h].astype(bf16), preferred_element_type=f32)
    psums = [sum(ps[g * R:(g + 1) * R]) for g in range(G)]
    s_blks = [sum(jnp.dot(overlap_t, piece, preferred_element_type=f32) for piece in _split3(psum))
              for psum in psums]
    s_blks = [jnp.where(forced, jnp.inf, jnp.where(admissible, s_blk, -jnp.inf)) for s_blk in s_blks]
    ranks = [jnp.zeros((n_blk, TQ), f32) for _ in range(G)]
    for i in range(n_blk):
        for g in range(G):
            row = s_blks[g][i:i + 1, :]
            beats = (row > s_blks[g]) | ((row == s_blks[g]) & (bj > i))
            ranks[g] = ranks[g] + jnp.where(beats, 1.0, 0.0)
    for g in range(G):
        chosen_bias = jnp.where(ranks[g] < float(n_sel), 0.0, NEG)
        for j in range(n_blk):
            blk_ref[g * n_blk + j] = jnp.broadcast_to(chosen_bias[j:j + 1, :], blk_ref.shape[1:])

    for branch, k_ref, vt_ref, biases_of, p_lo in ((1, kslc_ref, vslct_ref, selected_biases, 0),
                                                   (2, kwin_ref, vwint_ref, window_biases, first_win)):
        o_branch = attend(k_ref, vt_ref, biases_of, p_lo, n_span)
        for h in range(N_HEADS_B):
            out_ref[h] += gate_of(h, branch) * o_branch[h]
    o_ref[...] = jnp.concatenate([out_ref[h] for h in range(N_HEADS_B)], axis=0).T


def _nsa_call(pr, kc, vct, B, S):
    TQ = min(QUERY_TILE, S)
    n_sel = min(N_SLC, S // SLC_BLOCK)
    nch = S // CMP_STRIDE
    n_chunks = S // KEY_CHUNK
    G, R = N_KV_GROUPS_B, N_HEADS_B // N_KV_GROUPS_B
    qspec = lambda w_: pl.BlockSpec((TQ, w_), lambda b, i: (b * (S // TQ) + i, 0))
    kspec = lambda w_: pl.BlockSpec((S, w_), lambda b, i: (b, 0))
    vspec = pl.BlockSpec((n_chunks, G * HEAD_DIM, KEY_CHUNK), lambda b, i: (b, 0, 0))
    return pl.pallas_call(
        functools.partial(_nsa_kernel, n_sel=n_sel),
        grid=(B, S // TQ),
        in_specs=[qspec(512), qspec(512), qspec(128),
                  pl.BlockSpec((nch, G * LANES), lambda b, i: (b, 0)),
                  pl.BlockSpec((G, HEAD_DIM, nch), lambda b, i: (b, 0, 0)),
                  kspec(256), vspec, kspec(256), vspec],
        out_specs=qspec(512),
        out_shape=jax.ShapeDtypeStruct((B * S, 512), f32),
        scratch_shapes=[pltpu.VMEM((N_HEADS_B, SPAN_CHUNKS * KEY_CHUNK, TQ), bf16),
                        pltpu.VMEM((N_HEADS_B, HEAD_DIM + DENOM_ROWS, TQ), f32),
                        pltpu.VMEM((G * (S // SLC_BLOCK), 8, TQ), f32),
                        pltpu.VMEM((N_HEADS_B, HEAD_DIM, TQ), f32),
                        pltpu.VMEM((N_HEADS_B, TQ, LANES), bf16),
                        pltpu.VMEM((G, SPAN_CHUNKS * KEY_CHUNK, TQ), bf16)],
        compiler_params=pltpu.CompilerParams(dimension_semantics=("arbitrary", "arbitrary"),
                                             vmem_limit_bytes=VMEM_LIMIT),
        name="nsa",
    )(pr["qbn"], pr["qbr"], pr["misc"], kc, vct, pr["kslc"], pr["vslct"], pr["kwin"], pr["vwint"])


def _pool_kernel(u_ref, w_ref, o_ref):
    u = u_ref[...]
    S, C = u.shape
    row = lax.broadcasted_iota(jnp.int32, (S, 1), 0)
    grp = lax.broadcasted_iota(jnp.int32, (1, C), 1) // POOL_GROUP_DIM
    run, width = u, 1
    win_sum = jnp.zeros_like(u)
    win_len = jnp.zeros((1, C), f32)
    for gi, w in enumerate(POOL_WINDOWS):
        while width < w:
            run = run + jnp.where(row >= width, pltpu.roll(run, width, 0), 0.0)
            width *= 2
        assert width == w, "pooling windows must be increasing powers of two"
        win_sum = jnp.where(grp == gi, run, win_sum)
        win_len = jnp.where(grp == gi, float(w), win_len)
    pooled = win_sum / jnp.minimum((row + 1).astype(f32), win_len) - u
    o_ref[...] = jnp.dot(pooled.astype(bf16), w_ref[...], preferred_element_type=f32)


def _pool_call(uc, w_pool, B, S):
    C = N_POOL_GROUPS * POOL_GROUP_DIM
    wbd = jnp.zeros((C, C), f32)
    for gi in range(N_POOL_GROUPS):
        sl = slice(gi * POOL_GROUP_DIM, (gi + 1) * POOL_GROUP_DIM)
        wbd = wbd.at[sl, sl].set(w_pool[gi])
    spec = pl.BlockSpec((S, C), lambda b: (b, 0))
    return pl.pallas_call(
        _pool_kernel,
        grid=(B,),
        in_specs=[spec, pl.BlockSpec((C, C), lambda b: (0, 0))],
        out_specs=spec,
        out_shape=jax.ShapeDtypeStruct((B * S, C), f32),
        compiler_params=pltpu.CompilerParams(dimension_semantics=("arbitrary",),
                                             vmem_limit_bytes=VMEM_LIMIT),
        name="pool",
    )(uc, wbd.astype(bf16))


def _merge_mlp_kernel(x_ref, oa_ref, ob_ref, oc_ref, ga_ref, gb_ref, ps_ref, wo_ref, gpm_ref,
                      gmlp_ref, wup_ref, wdn_ref, gpost_ref, o_ref):
    mixed = jnp.concatenate([_rms(oa_ref[...], ga_ref[...]), _rms(ob_ref[...], gb_ref[...]),
                             _rms(oc_ref[...]) * ps_ref[...]], axis=1).astype(bf16)
    x = x_ref[...] + _rms(jnp.dot(mixed, wo_ref[...], preferred_element_type=f32), gpm_ref[...])
    h = _rms(x, gmlp_ref[...]).astype(bf16)
    up = jnp.maximum(jnp.dot(h, wup_ref[...], preferred_element_type=f32), 0.0)
    f = jnp.dot((up * up).astype(bf16), wdn_ref[...], preferred_element_type=f32)
    o_ref[...] = x + _rms(f, gpost_ref[...])


def _merge_mlp_call(x, oa, ob, oc, ga, gb, ps, wo, gpm, gmlp, wup, wdn, gpost):
    T, D = x.shape
    TM = TOKEN_TILE
    row = lambda a: pl.BlockSpec((TM, a.shape[1]), lambda i: (i, 0))
    full = lambda a: pl.BlockSpec(a.shape, lambda i: (0, 0), pipeline_mode=pl.Buffered(1))
    args = (x, oa, ob, oc, ga, gb, ps, wo, gpm, gmlp, wup, wdn, gpost)
    return pl.pallas_call(
        _merge_mlp_kernel,
        grid=(T // TM,),
        in_specs=[row(a) for a in args[:4]] + [full(a) for a in args[4:]],
        out_specs=pl.BlockSpec((TM, D), lambda i: (i, 0)),
        out_shape=jax.ShapeDtypeStruct((T, D), f32),
        compiler_params=pltpu.CompilerParams(dimension_semantics=("arbitrary",),
                                             vmem_limit_bytes=VMEM_LIMIT),
        name="merge_mlp",
    )(*args)


def _rope_lane_tables(positions, rot_dim, period):
    half = rot_dim // 2
    inv_freq = ROPE_THETA ** (-jnp.arange(0, rot_dim, 2, dtype=f32) / rot_dim)
    ang = positions.astype(f32).reshape(-1, 1) * inv_freq
    cos, sin = jnp.cos(ang), jnp.sin(ang)
    T = ang.shape[0]
    ones, zeros, zh = jnp.ones((T, period - rot_dim), f32), jnp.zeros((T, period - rot_dim), f32), jnp.zeros((T, half), f32)
    c = jnp.concatenate([cos, cos, ones], axis=1)
    sa = jnp.concatenate([-sin, zh, zeros], axis=1)
    sb = jnp.concatenate([zh, sin, zeros], axis=1)
    reps = LANES // period
    return jnp.concatenate([jnp.tile(t, (1, reps)) for t in (c, sa, sb)], axis=1)


def _dup_halves(w):
    D = w.shape[0]
    g = w.reshape(D, -1, 1, HEAD_DIM)
    return jnp.broadcast_to(g, (D, g.shape[1], 2, HEAD_DIM)).reshape(D, -1)


def _layout_w_in(w_in):
    D = w_in.shape[0]
    widths = (256, 128, 128, 32, 4, 512, 128, 128, 128, 128, 128, 128, 24, 256)
    offs = [0]
    for w_ in widths:
        offs.append(offs[-1] + w_)
    (q_a, ckv, q_idx, k_idx, w_idx, q_b, k_cmp, v_cmp, k_slc, v_slc, k_win, v_win, gates, u_c) = [
        w_in[:, offs[i]:offs[i + 1]] for i in range(len(widths))]
    misc = jnp.concatenate([w_idx, gates, jnp.zeros((D, LANES - 28), w_in.dtype)], axis=1)
    cols = [q_a, ckv, q_idx, jnp.tile(k_idx, (1, LANES // IDX_DIM)), q_b, k_cmp, v_cmp,
            _dup_halves(k_slc), v_slc, _dup_halves(k_win), v_win, misc, u_c]
    w = jnp.concatenate(cols, axis=1)
    assert w.shape[1] == N_PROJ
    return w.astype(bf16)


def kernel(x, positions, w_in, g_kv_a, w_kv_up_a, w_cmp_k, w_cmp_v, pe_cmp_k, pe_cmp_v, w_pool,
           pool_scale, g_out_a, g_out_b, w_out, g_pre_mix, g_post_mix, g_pre_mlp, g_post_mlp,
           w_up, w_down):
    B, S, D = x.shape
    depth = w_in.shape[0]
    assert S % (SPAN_CHUNKS * KEY_CHUNK) == 0 and QUERY_TILE == KEY_CHUNK
    assert (B * S) % TOKEN_TILE == 0
    rope_h = _rope_lane_tables(positions, ROT_DIM_HEAD, HEAD_DIM)
    rope_i = _rope_lane_tables(positions, ROT_DIM_IDX, IDX_DIM)
    row = lambda v: v.reshape(1, -1)
    xt = x.reshape(B * S, D)
    for l in range(depth):
        wkv = jnp.concatenate([_dup_halves(w_kv_up_a[l][:, :HEAD_DIM]), w_kv_up_a[l][:, HEAD_DIM:],
                               jnp.zeros((KV_RANK_A, LANES - HEAD_DIM), f32)], axis=1).astype(bf16)
        pr = _proj_call(xt, row(g_pre_mix[l]), _layout_w_in(w_in[l]), row(g_kv_a[l]), wkv, rope_h, rope_i)
        o_a = _dsa_call(pr, B, S)
        kc, vc = _compress_call(pr["kvcmp"], pe_cmp_k[l], pe_cmp_v[l], w_cmp_k[l], w_cmp_v[l], B, S)
        o_b = _nsa_call(pr, kc, vc, B, S)
        o_c = _pool_call(pr["uc"], w_pool[l], B, S)
        xt = _merge_mlp_call(xt, o_a, o_b, o_c, row(g_out_a[l]), row(g_out_b[l]), row(pool_scale[l]),
                             w_out[l].astype(bf16), row(g_post_mix[l]), row(g_pre_mlp[l]),
                             w_up[l].astype(bf16), w_down[l].astype(bf16), row(g_post_mlp[l]))
    return xt.reshape(B, S, D)
```

```python
import functools

import jax
import jax.numpy as jnp
from jax import lax
from jax.experimental import pallas as pl
from jax.experimental.pallas import tpu as pltpu

f32 = jnp.float32
bf16 = jnp.bfloat16

HEAD_DIM = 64
N_HEADS_A = 4
N_HEADS_B = 8
N_KV_GROUPS_B = 2
N_POOL_GROUPS = 4
POOL_GROUP_DIM = 64
POOL_WINDOWS = (2, 4, 8, 16)
KV_RANK_A = 128
N_IDX_HEADS = 4
IDX_DIM = 32
TOPK_MAX = 256
CMP_BLOCK = 32
CMP_STRIDE = 16
SLC_BLOCK = 64
N_SLC = 16
WINDOW = 512
ROPE_THETA = 500000.0
ROT_DIM_HEAD = HEAD_DIM // 4
ROT_DIM_IDX = IDX_DIM // 4
EPS = 1e-6
NEG = -1e30
SCALE = HEAD_DIM ** -0.5

LANES = 128
TOKEN_TILE = 512
QUERY_TILE = 256
KEY_CHUNK = 256
BF16_ROWS = 16
DENOM_ROWS = BF16_ROWS
LOG2E = 1.4426950408889634
LOGITS_LEAD = 3
SPAN_CHUNKS = 2
VMEM_LIMIT = 56 * 1024 * 1024

_COLS = {}
_off = 0
for _name, _w in (("qa", 256), ("ckv", 128), ("qidx", 128), ("kidx", 128), ("qb", 512),
                  ("kvcmp", 256), ("kslc", 256), ("vslc", 128), ("kwin", 256), ("vwin", 128),
                  ("misc", 128), ("uc", 256)):
    _COLS[_name] = (_off, _off + _w)
    _off += _w
N_PROJ = _off
GATE_LANE0 = N_IDX_HEADS


def _rms(v, gain=None):
    y = v * lax.rsqrt(jnp.mean(v * v, axis=-1, keepdims=True) + EPS)
    return y if gain is None else y * gain


def _rope128(v, c, sa, sb, half):
    return v * c + pltpu.roll(v, LANES - half, 1) * sa + pltpu.roll(v, half, 1) * sb


def _nt_dot(a, b):
    return lax.dot_general(a, b, (((1,), (1,)), ((), ())), preferred_element_type=f32)


def _proj_kernel(x_ref, g_ref, w_ref, gkv_ref, wkv_ref, rh_ref, ri_ref,
                 qa_ref, ka_ref, vat_ref, qidx_ref, kidx_ref, qbn_ref, qbr_ref,
                 kvcmp_ref, kslc_ref, vslct_ref, kwin_ref, vwint_ref, misc_ref, uc_ref):
    h = _rms(x_ref[...], g_ref[...])
    p = jnp.dot(h.astype(bf16), w_ref[...], preferred_element_type=f32)
    hc, hsa, hsb = rh_ref[:, 0:128], rh_ref[:, 128:256], rh_ref[:, 256:384]
    ic, isa, isb = ri_ref[:, 0:128], ri_ref[:, 128:256], ri_ref[:, 256:384]
    hh, ih = ROT_DIM_HEAD // 2, ROT_DIM_IDX // 2

    def col(name, j=0):
        a = _COLS[name][0] + j * LANES
        return p[:, a:a + LANES]

    for j in range(2):
        qa_ref[:, j * LANES:(j + 1) * LANES] = (_rope128(col("qa", j), hc, hsa, hsb, hh)
                                                * (SCALE * LOG2E)).astype(bf16)
    kv = jnp.dot(_rms(col("ckv"), gkv_ref[...]).astype(bf16), wkv_ref[...], preferred_element_type=f32)
    ka_ref[...] = _rope128(kv[:, 0:LANES], hc, hsa, hsb, hh).astype(bf16)
    for cc in range(vat_ref.shape[0]):
        tok = slice(cc * KEY_CHUNK, (cc + 1) * KEY_CHUNK)
        vat_ref[cc] = kv[tok, LANES:2 * LANES].T[0:HEAD_DIM, :].astype(bf16)
        vslct_ref[cc] = col("vslc")[tok, :].T.astype(bf16)
        vwint_ref[cc] = col("vwin")[tok, :].T.astype(bf16)
    qidx_ref[...] = _rope128(col("qidx"), ic, isa, isb, ih).astype(bf16)
    kidx_ref[...] = _rope128(col("kidx"), ic, isa, isb, ih).astype(bf16)
    for j in range(4):
        q = col("qb", j)
        qbn_ref[:, j * LANES:(j + 1) * LANES] = (q * SCALE).astype(bf16)
        qbr_ref[:, j * LANES:(j + 1) * LANES] = (_rope128(q, hc, hsa, hsb, hh) * (SCALE * LOG2E)).astype(bf16)
    for j in range(2):
        sl = slice(j * LANES, (j + 1) * LANES)
        kvcmp_ref[:, sl] = col("kvcmp", j)
        kslc_ref[:, sl] = _rope128(col("kslc", j), hc, hsa, hsb, hh).astype(bf16)
        kwin_ref[:, sl] = _rope128(col("kwin", j), hc, hsa, hsb, hh).astype(bf16)
        uc_ref[:, sl] = col("uc", j)
    misc_ref[...] = col("misc")


def _proj_call(x, g, w, gkv, wkv, rope_h, rope_i):
    T, D = x.shape
    TM = TOKEN_TILE
    row = lambda w_: pl.BlockSpec((TM, w_), lambda i: (i, 0))
    full = lambda a: pl.BlockSpec(a.shape, lambda i: (0, 0))
    outs = [("qa", 256, bf16), ("ka", 128, bf16), ("vat", -HEAD_DIM, bf16), ("qidx", 128, bf16),
            ("kidx", 128, bf16), ("qbn", 512, bf16), ("qbr", 512, bf16), ("kvcmp", 256, f32),
            ("kslc", 256, bf16), ("vslct", -LANES, bf16), ("kwin", 256, bf16), ("vwint", -LANES, bf16),
            ("misc", 128, f32), ("uc", 256, f32)]
    tspec = lambda d: pl.BlockSpec((TM // KEY_CHUNK, d, KEY_CHUNK), lambda i: (i, 0, 0))
    tshape = lambda d: jax.ShapeDtypeStruct((T // KEY_CHUNK, d, KEY_CHUNK), bf16)
    res = pl.pallas_call(
        _proj_kernel,
        grid=(T // TM,),
        in_specs=[row(D), full(g), full(w), full(gkv), full(wkv), row(384), row(384)],
        out_specs=[tspec(-w_) if w_ < 0 else row(w_) for _, w_, _ in outs],
        out_shape=[tshape(-w_) if w_ < 0 else jax.ShapeDtypeStruct((T, w_), dt) for _, w_, dt in outs],
        compiler_params=pltpu.CompilerParams(dimension_semantics=("arbitrary",),
                                             vmem_limit_bytes=VMEM_LIMIT),
        name="proj",
    )(x, g, w, gkv, wkv, rope_h, rope_i)
    return dict(zip([n for n, _, _ in outs], res))


KEY_OF_NEG_MAX = -(2 ** 31 - 2 ** 23)
HALF_BITS = 16
COUNT_SLAB = 32
COUNT16_SLAB = 64


def _column_max(s):
    m = s[0:BF16_ROWS, :]
    for r in range(1, s.shape[0] // BF16_ROWS):
        m = jnp.maximum(m, s[r * BF16_ROWS:(r + 1) * BF16_ROWS, :])
    return jnp.max(m.astype(f32), axis=0, keepdims=True)


def _attend_spans(prepare_span, block_operands, values_t, s_ref, acc_ref, p_lo, p_hi, extra):
    n_heads = s_ref.shape[0]
    TQ = s_ref.shape[2]
    span = SPAN_CHUNKS * KEY_CHUNK
    ones = jnp.ones((acc_ref.shape[1] - HEAD_DIM, span), bf16)
    acc_ref[...] = jnp.zeros_like(acc_ref)

    def step(p, st):
        maxes, extra = st
        extra = prepare_span(p, extra)
        new_maxes = [None] * n_heads

        def logits(h):
            blocks = [block_operands(h, p, j) for j in range(span // LANES)]
            raw = [_nt_dot(kb, q) for kb, q, _ in blocks]
            masked = [r.astype(bf16) + bias for r, (_, _, bias) in zip(raw, blocks)]
            for j, s in enumerate(masked):
                s_ref[h, j * LANES:(j + 1) * LANES, :] = s
            m_new = maxes[h]
            for s in masked:
                m_new = jnp.maximum(m_new, _column_max(s))
            new_maxes[h] = m_new

        def weigh(h):
            vt = jnp.concatenate([values_t(h, SPAN_CHUNKS * p + j) for j in range(SPAN_CHUNKS)], axis=1)
            e = jnp.exp2(s_ref[h] - new_maxes[h].astype(bf16))
            pv = jnp.dot(jnp.concatenate([vt, ones], axis=0), e, preferred_element_type=f32)
            acc_ref[h] = jnp.exp2(maxes[h] - new_maxes[h]) * acc_ref[h] + pv

        for h in range(n_heads + LOGITS_LEAD):
            if h < n_heads:
                logits(h)
            if h >= LOGITS_LEAD:
                weigh(h - LOGITS_LEAD)
        return tuple(new_maxes), extra

    lax.fori_loop(p_lo, p_hi, step, ((jnp.full((1, TQ), -3.0e38, f32),) * n_heads, extra))
    return [acc_ref[h, 0:HEAD_DIM, :] / acc_ref[h, HEAD_DIM:HEAD_DIM + 1, :] for h in range(n_heads)]


def _ordered_key(v):
    bits = lax.bitcast_convert_type(v, jnp.int32)
    return bits ^ ((bits >> 31) & 0x7FFFFFFF)


def _dsa_kernel(qa_ref, qidx_ref, misc_ref, kidx_ref, ka_ref, vat_ref, o_ref,
                key_ref, s_ref, acc_ref, qm_ref, bias_ref, k16_ref, *, n_keep):
    TQ = qa_ref.shape[0]
    CH = KEY_CHUNK
    q0 = pl.program_id(1) * TQ
    tpos = q0 + lax.broadcasted_iota(jnp.int32, (1, TQ), 1)
    lane = lax.broadcasted_iota(jnp.int32, (1, LANES), 1)
    k = float(n_keep)

    SUB = LANES
    span = SPAN_CHUNKS * CH
    subs_per_span = span // SUB
    n_span = (q0 + TQ + span - 1) // span

    def sub_rows(i):
        return pl.ds(pl.multiple_of(i * SUB, SUB), SUB)

    w_t = misc_ref[...].T[0:8, :]

    def index_q(h):
        qidx = qidx_ref[...]
        return jnp.where(lane // IDX_DIM == h, qidx, jnp.zeros_like(qidx))

    def score_span(p, carry):
        for i in [p * subs_per_span + j for j in range(subs_per_span)]:
            kb = kidx_ref[sub_rows(i), :]
            sc = jnp.zeros((SUB, TQ), f32)
            for h in range(N_IDX_HEADS):
                sc = sc + jnp.maximum(_nt_dot(kb, index_q(h)), 0.0) * w_t[h:h + 1, :]
            kpos = i * SUB + lax.broadcasted_iota(jnp.int32, (SUB, 1), 0)
            sc = jnp.where(sc == 0.0, 0.0, sc)
            key = _ordered_key(jnp.where(kpos <= tpos, sc, -jnp.inf))
            key_ref[sub_rows(i), :] = key
            k16_ref[sub_rows(i), :] = (key >> HALF_BITS).astype(jnp.int16)
        return carry

    lax.fori_loop(0, n_span, score_span, 0)

    def count_ge(v):
        vs = jnp.broadcast_to(v, (COUNT_SLAB, TQ))

        def span_count(p, acc):
            base = pl.multiple_of(p * span, span)
            for r in range(span // COUNT_SLAB):
                ks = key_ref[pl.ds(base + r * COUNT_SLAB, COUNT_SLAB), :]
                acc = acc + jnp.where(ks >= vs, 1, 0)
            return acc

        acc = lax.fori_loop(0, n_span, span_count, jnp.zeros((COUNT_SLAB, TQ), jnp.int32))
        return jnp.sum(acc.astype(f32), axis=0, keepdims=True)

    def count16_ge(v):
        vs = jnp.broadcast_to(v.astype(jnp.int16), (COUNT16_SLAB, TQ))
        one, zero = jnp.int16(1), jnp.int16(0)

        def span_count(p, acc):
            base = pl.multiple_of(p * span, span)
            for r in range(span // COUNT16_SLAB):
                ks = k16_ref[pl.ds(base + r * COUNT16_SLAB, COUNT16_SLAB), :]
                acc = acc + jnp.where(ks >= vs, one, zero)
            return acc

        acc = lax.fori_loop(0, n_span, span_count, jnp.zeros((COUNT16_SLAB, TQ), jnp.int16))
        return jnp.sum(acc.astype(jnp.int32).astype(f32), axis=0, keepdims=True)

    def bisect16(st):
        lo, hi, at, hit, active = st
        act = active > 0
        mid = (lo + hi) >> 1
        collapsed = (hi - lo) <= 1
        c = count16_ge(mid)
        exact = act & ~collapsed & (c == k)
        at = jnp.where(exact, mid, at)
        hit = jnp.where(exact, 1, hit)
        go = act & ~collapsed & ~exact
        lo = jnp.where(go & (c > k), mid, lo)
        hi = jnp.where(go & (c < k), mid, hi)
        return lo, hi, at, hit, jnp.where(go, 1, 0)

    need = (tpos + 1) > n_keep
    half = 1 << (HALF_BITS - 1)
    lo0 = jnp.full((1, TQ), -half, jnp.int32)
    hi0 = jnp.full((1, TQ), half, jnp.int32)
    zero = jnp.zeros((1, TQ), jnp.int32)
    st = (lo0, hi0, zero, zero, jnp.where(need, 1, 0))
    st = lax.fori_loop(0, HALF_BITS, lambda _, s: bisect16(s), st)
    prefix, at_hi, hit_hi = st[0], st[2], st[3] > 0

    def low_keys_span(p, carry):
        for i in [p * subs_per_span + j for j in range(subs_per_span)]:
            key = key_ref[sub_rows(i), :]
            hi_half = key >> HALF_BITS
            low = (key & ((1 << HALF_BITS) - 1)) - half
            low = jnp.where(hi_half > prefix, half - 1, jnp.where(hi_half < prefix, -half, low))
            k16_ref[sub_rows(i), :] = low.astype(jnp.int16)
        return carry

    lax.fori_loop(0, n_span, low_keys_span, 0)

    st = (lo0, hi0, zero, zero, jnp.where(need & ~hit_hi, 1, 0))
    st = lax.fori_loop(0, HALF_BITS, lambda _, s: bisect16(s), st)
    low_thr = jnp.where(st[3] > 0, st[2], st[0]) + half
    thr = jnp.where(hit_hi, at_hi << HALF_BITS, (prefix << HALF_BITS) | low_thr)
    thr = jnp.where(need, thr, KEY_OF_NEG_MAX)
    needed = k - count_ge(thr + 1)

    ri = lax.broadcasted_iota(jnp.int32, (SUB, SUB), 0)
    rj = lax.broadcasted_iota(jnp.int32, (SUB, SUB), 1)
    strict_lower = jnp.where(rj < ri, 1.0, 0.0).astype(bf16)
    for h in range(N_HEADS_A):
        qc = qa_ref[:, (h // 2) * LANES:(h // 2 + 1) * LANES]
        qm_ref[h] = jnp.where(lane // HEAD_DIM == h % 2, qc, jnp.zeros_like(qc))

    def prepare_span(p, ties_before):
        for j in range(subs_per_span):
            keys = key_ref[sub_rows(p * subs_per_span + j), :]
            tie = keys == thr
            tie_f = jnp.where(tie, 1.0, 0.0)
            rank = jnp.dot(strict_lower, tie_f.astype(bf16), preferred_element_type=f32) + ties_before
            sel = (keys > thr) | (tie & (rank < needed))
            bias_ref[j * SUB:(j + 1) * SUB, :] = jnp.where(sel, 0.0, NEG).astype(bf16)
            ties_before = ties_before + jnp.sum(tie_f, axis=0, keepdims=True)
        return ties_before

    def block_operands(h, p, j):
        return ka_ref[sub_rows(p * subs_per_span + j), :], qm_ref[h], bias_ref[j * SUB:(j + 1) * SUB, :]

    outs = _attend_spans(prepare_span, block_operands, lambda h, c: vat_ref[c], s_ref, acc_ref, 0, n_span,
                         jnp.zeros((1, TQ), f32))
    o_ref[...] = jnp.concatenate(outs, axis=0).T


def _dsa_call(pr, B, S):
    TQ = min(QUERY_TILE, S)
    n_keep = min(TOPK_MAX, S // 4)
    n_chunks = S // KEY_CHUNK
    qspec = lambda w_: pl.BlockSpec((TQ, w_), lambda b, i: (b * (S // TQ) + i, 0))
    kspec = lambda w_: pl.BlockSpec((S, w_), lambda b, i: (b, 0))
    vspec = pl.BlockSpec((n_chunks, HEAD_DIM, KEY_CHUNK), lambda b, i: (b, 0, 0))
    return pl.pallas_call(
        functools.partial(_dsa_kernel, n_keep=n_keep),
        grid=(B, S // TQ),
        in_specs=[qspec(256), qspec(128), qspec(128), kspec(128), kspec(128), vspec],
        out_specs=qspec(256),
        out_shape=jax.ShapeDtypeStruct((B * S, 256), f32),
        scratch_shapes=[pltpu.VMEM((S, TQ), jnp.int32),
                        pltpu.VMEM((N_HEADS_A, SPAN_CHUNKS * KEY_CHUNK, TQ), bf16),
                        pltpu.VMEM((N_HEADS_A, HEAD_DIM + DENOM_ROWS, TQ), f32),
                        pltpu.VMEM((N_HEADS_A, TQ, LANES), bf16),
                        pltpu.VMEM((SPAN_CHUNKS * KEY_CHUNK, TQ), bf16),
                        pltpu.VMEM((S, TQ), jnp.int16)],
        compiler_params=pltpu.CompilerParams(dimension_semantics=("arbitrary", "arbitrary"),
                                             vmem_limit_bytes=VMEM_LIMIT),
        name="dsa",
    )(pr["qa"], pr["qidx"], pr["misc"], pr["kidx"], pr["ka"], pr["vat"])


def _compress_kernel(ch_ref, pe_ref, w_ref, kc_ref, vct_ref):
    nch = ch_ref.shape[1]
    rowid = lax.broadcasted_iota(jnp.int32, (nch, 1), 0)
    for kv in range(2):
        for g in range(N_KV_GROUPS_B):
            ch = ch_ref[kv * N_KV_GROUPS_B + g]
            top = jnp.dot((ch + pe_ref[kv, 0:1, :]).astype(bf16), w_ref[kv, 0], preferred_element_type=f32)
            bot = jnp.dot((ch + pe_ref[kv, 1:2, :]).astype(bf16), w_ref[kv, 1], preferred_element_type=f32)
            blk = top + pltpu.roll(bot, nch - 1, 0)
            blk = jnp.where(rowid < nch - 1, blk, 0.0)
            if kv == 0:
                kc_ref[:, g * LANES:(g + 1) * LANES] = blk.astype(bf16)
            else:
                vct_ref[g] = blk.T[0:HEAD_DIM, :].astype(bf16)


def _compress_call(kvcmp, pe_k, pe_v, w_k, w_v, B, S):
    nch = S // CMP_STRIDE
    G, dh = N_KV_GROUPS_B, HEAD_DIM
    ch = kvcmp.reshape(B, nch, CMP_STRIDE, 2 * G, dh).transpose(0, 3, 1, 2, 4)
    ch = ch.reshape(B, 2 * G, nch, CMP_STRIDE * dh)
    pe = jnp.stack([pe_k, pe_v]).reshape(2, 2, CMP_STRIDE * dh)
    w = jnp.stack([w_k, w_v]).reshape(2, 2, CMP_STRIDE * dh, dh)
    w = jnp.concatenate([w, w], axis=-1).astype(bf16)
    return pl.pallas_call(
        _compress_kernel,
        grid=(B,),
        in_specs=[pl.BlockSpec((None, 2 * G, nch, CMP_STRIDE * dh), lambda b: (b, 0, 0, 0)),
                  pl.BlockSpec(pe.shape, lambda b: (0, 0, 0)),
                  pl.BlockSpec(w.shape, lambda b: (0, 0, 0, 0))],
        out_specs=[pl.BlockSpec((nch, G * LANES), lambda b: (b, 0)),
                   pl.BlockSpec((G, dh, nch), lambda b: (b, 0, 0))],
        out_shape=[jax.ShapeDtypeStruct((B * nch, G * LANES), bf16),
                   jax.ShapeDtypeStruct((B * G, dh, nch), bf16)],
        compiler_params=pltpu.CompilerParams(dimension_semantics=("arbitrary",),
                                             vmem_limit_bytes=VMEM_LIMIT),
        name="compress",
    )(ch, pe, w)


def _split3(v):
    a = v.astype(bf16)
    r = v - a.astype(f32)
    b = r.astype(bf16)
    c = (r - b.astype(f32)).astype(bf16)
    return a, b, c


def _nsa_kernel(qn_ref, qr_ref, misc_ref, kc_ref, vct_ref, kslc_ref, vslct_ref, kwin_ref, vwint_ref,
                o_ref, s_ref, acc_ref, blk_ref, out_ref, qm_ref, bias_ref, *, n_sel):
    TQ = qn_ref.shape[0]
    CH = KEY_CHUNK
    SUB = LANES
    S = kslc_ref.shape[0]
    nch = kc_ref.shape[0]
    n_cmp = nch - 1
    n_blk = S // SLC_BLOCK
    G = N_KV_GROUPS_B
    R = N_HEADS_B // G
    q0 = pl.program_id(1) * TQ
    span = SPAN_CHUNKS * CH
    subs_per_span = span // SUB
    n_span = (q0 + TQ + span - 1) // span
    first_win = jnp.maximum(q0 - WINDOW, 0) // span
    tpos = q0 + lax.broadcasted_iota(jnp.int32, (1, TQ), 1)
    lane = lax.broadcasted_iota(jnp.int32, (1, LANES), 1)
    gate = jax.nn.sigmoid(misc_ref[...].T[0:32, :])

    def sub_rows(i):
        return pl.ds(pl.multiple_of(i * SUB, SUB), SUB)

    def gate_of(h, branch):
        a = GATE_LANE0 + 3 * h + branch
        return gate[a:a + 1, :]

    def head_q(ref, h):
        qc = ref[:, (h // 2) * LANES:(h // 2 + 1) * LANES]
        return jnp.where(lane // HEAD_DIM == h % 2, qc, jnp.zeros_like(qc))

    def attend(k_ref, vt_ref, biases_of, p_lo, p_hi):
        def prepare_span(p, extra):
            for j in range(subs_per_span):
                biases = biases_of(p * subs_per_span + j)
                for g in range(G):
                    bias_ref[g, j * SUB:(j + 1) * SUB, :] = biases[g]
            return extra

        def block_operands(h, p, j):
            g = h // R
            return (k_ref[sub_rows(p * subs_per_span + j), g * LANES:(g + 1) * LANES], qm_ref[h],
                    bias_ref[g, j * SUB:(j + 1) * SUB, :])

        def values_t(h, c):
            g = h // R
            return vt_ref[c, g * HEAD_DIM:(g + 1) * HEAD_DIM, :]

        return _attend_spans(prepare_span, block_operands, values_t, s_ref, acc_ref, p_lo, p_hi, 0)

    cpos = lax.broadcasted_iota(jnp.int32, (nch, 1), 0)
    cmask = (cpos * CMP_STRIDE + CMP_BLOCK - 1 <= tpos) & (cpos < n_cmp)
    oj = lax.broadcasted_iota(jnp.int32, (n_blk, nch), 0)
    oc = lax.broadcasted_iota(jnp.int32, (n_blk, nch), 1)
    overlap_t = ((oc * CMP_STRIDE <= oj * SLC_BLOCK + SLC_BLOCK - 1)
                 & (oc * CMP_STRIDE + CMP_BLOCK - 1 >= oj * SLC_BLOCK) & (oc < n_cmp))
    overlap_t = jnp.where(overlap_t, 1.0, 0.0).astype(bf16)
    bj = lax.broadcasted_iota(jnp.int32, (n_blk, 1), 0)
    cur = tpos // SLC_BLOCK
    forced = (bj == 0) | (bj == cur) | (bj == cur - 1)
    admissible = bj * SLC_BLOCK <= tpos

    for h in range(N_HEADS_B):
        qm_ref[h] = head_q(qr_ref, h)

    def window_biases(i):
        kpos = i * SUB + lax.broadcasted_iota(jnp.int32, (SUB, 1), 0)
        return [jnp.where((kpos <= tpos) & (kpos > tpos - WINDOW), 0.0, NEG).astype(bf16)] * G

    def selected_biases(i):
        per_sub = SUB // SLC_BLOCK
        kpos = i * SUB + lax.broadcasted_iota(jnp.int32, (SUB, 1), 0)
        biases = []
        for g in range(G):
            parts = []
            for jj in range(per_sub):
                b8 = blk_ref[g * n_blk + i * per_sub + jj]
                parts.extend([b8] * (SLC_BLOCK // b8.shape[0]))
            biases.append(jnp.where(kpos <= tpos, jnp.concatenate(parts, axis=0), NEG).astype(bf16))
        return biases

    all_heads = range(N_HEADS_B)
    lcs = [jnp.where(cmask, _nt_dot(kc_ref[:, (h // R) * LANES:(h // R + 1) * LANES], head_q(qn_ref, h)), NEG)
           for h in all_heads]
    cmaxes = [jnp.max(lc, axis=0, keepdims=True) for lc in lcs]
    es = [jnp.where(cmask, jnp.exp(lc - m), 0.0) for lc, m in zip(lcs, cmaxes)]
    sums = [jnp.sum(e, axis=0, keepdims=True) for e in es]
    ps = [e * (1.0 / jnp.where(l > 0.0, l, 1.0)) for e, l in zip(es, sums)]
    for h in all_heads:
        out_ref[h] = gate_of(h, 0) * jnp.dot(vct_ref[h // R], ps[h].astype(bf16), preferred_element_type=f32)
    psums = [sum(ps[g * R:(g + 1) * R]) for g in range(G)]
    s_blks = [sum(jnp.dot(overlap_t, piece, preferred_element_type=f32) for piece in _split3(psum))
              for psum in psums]
    s_blks = [jnp.where(forced, jnp.inf, jnp.where(admissible, s_blk, -jnp.inf)) for s_blk in s_blks]
    ranks = [jnp.zeros((n_blk, TQ), f32) for _ in range(G)]
    for i in range(n_blk):
        for g in range(G):
            row = s_blks[g][i:i + 1, :]
            beats = (row > s_blks[g]) | ((row == s_blks[g]) & (bj > i))
            ranks[g] = ranks[g] + jnp.where(beats, 1.0, 0.0)
    for g in range(G):
        chosen_bias = jnp.where(ranks[g] < float(n_sel), 0.0, NEG)
        for j in range(n_blk):
            blk_ref[g * n_blk + j] = jnp.broadcast_to(chosen_bias[j:j + 1, :], blk_ref.shape[1:])

    for branch, k_ref, vt_ref, biases_of, p_lo in ((1, kslc_ref, vslct_ref, selected_biases, 0),
                                                   (2, kwin_ref, vwint_ref, window_biases, first_win)):
        o_branch = attend(k_ref, vt_ref, biases_of, p_lo, n_span)
        for h in range(N_HEADS_B):
            out_ref[h] += gate_of(h, branch) * o_branch[h]
    o_ref[...] = jnp.concatenate([out_ref[h] for h in range(N_HEADS_B)], axis=0).T


def _nsa_call(pr, kc, vct, B, S):
    TQ = min(QUERY_TILE, S)
    n_sel = min(N_SLC, S // SLC_BLOCK)
    nch = S // CMP_STRIDE
    n_chunks = S // KEY_CHUNK
    G, R = N_KV_GROUPS_B, N_HEADS_B // N_KV_GROUPS_B
    qspec = lambda w_: pl.BlockSpec((TQ, w_), lambda b, i: (b * (S // TQ) + i, 0))
    kspec = lambda w_: pl.BlockSpec((S, w_), lambda b, i: (b, 0))
    vspec = pl.BlockSpec((n_chunks, G * HEAD_DIM, KEY_CHUNK), lambda b, i: (b, 0, 0))
    return pl.pallas_call(
        functools.partial(_nsa_kernel, n_sel=n_sel),
        grid=(B, S // TQ),
        in_specs=[qspec(512), qspec(512), qspec(128),
                  pl.BlockSpec((nch, G * LANES), lambda b, i: (b, 0)),
                  pl.BlockSpec((G, HEAD_DIM, nch), lambda b, i: (b, 0, 0)),
                  kspec(256), vspec, kspec(256), vspec],
        out_specs=qspec(512),
        out_shape=jax.ShapeDtypeStruct((B * S, 512), f32),
        scratch_shapes=[pltpu.VMEM((N_HEADS_B, SPAN_CHUNKS * KEY_CHUNK, TQ), bf16),
                        pltpu.VMEM((N_HEADS_B, HEAD_DIM + DENOM_ROWS, TQ), f32),
                        pltpu.VMEM((G * (S // SLC_BLOCK), 8, TQ), f32),
                        pltpu.VMEM((N_HEADS_B, HEAD_DIM, TQ), f32),
                        pltpu.VMEM((N_HEADS_B, TQ, LANES), bf16),
                        pltpu.VMEM((G, SPAN_CHUNKS * KEY_CHUNK, TQ), bf16)],
        compiler_params=pltpu.CompilerParams(dimension_semantics=("arbitrary", "arbitrary"),
                                             vmem_limit_bytes=VMEM_LIMIT),
        name="nsa",
    )(pr["qbn"], pr["qbr"], pr["misc"], kc, vct, pr["kslc"], pr["vslct"], pr["kwin"], pr["vwint"])


def _pool_kernel(u_ref, w_ref, o_ref):
    u = u_ref[...]
    S, C = u.shape
    row = lax.broadcasted_iota(jnp.int32, (S, 1), 0)
    grp = lax.broadcasted_iota(jnp.int32, (1, C), 1) // POOL_GROUP_DIM
    run, width = u, 1
    win_sum = jnp.zeros_like(u)
    win_len = jnp.zeros((1, C), f32)
    for gi, w in enumerate(POOL_WINDOWS):
        while width < w:
            run = run + jnp.where(row >= width, pltpu.roll(run, width, 0), 0.0)
            width *= 2
        assert width == w, "pooling windows must be increasing powers of two"
        win_sum = jnp.where(grp == gi, run, win_sum)
        win_len = jnp.where(grp == gi, float(w), win_len)
    pooled = win_sum / jnp.minimum((row + 1).astype(f32), win_len) - u
    o_ref[...] = jnp.dot(pooled.astype(bf16), w_ref[...], preferred_element_type=f32)


def _pool_call(uc, w_pool, B, S):
    C = N_POOL_GROUPS * POOL_GROUP_DIM
    wbd = jnp.zeros((C, C), f32)
    for gi in range(N_POOL_GROUPS):
        sl = slice(gi * POOL_GROUP_DIM, (gi + 1) * POOL_GROUP_DIM)
        wbd = wbd.at[sl, sl].set(w_pool[gi])
    spec = pl.BlockSpec((S, C), lambda b: (b, 0))
    return pl.pallas_call(
        _pool_kernel,
        grid=(B,),
        in_specs=[spec, pl.BlockSpec((C, C), lambda b: (0, 0))],
        out_specs=spec,
        out_shape=jax.ShapeDtypeStruct((B * S, C), f32),
        compiler_params=pltpu.CompilerParams(dimension_semantics=("arbitrary",),
                                             vmem_limit_bytes=VMEM_LIMIT),
        name="pool",
    )(uc, wbd.astype(bf16))


def _merge_mlp_kernel(x_ref, oa_ref, ob_ref, oc_ref, ga_ref, gb_ref, ps_ref, wo_ref, gpm_ref,
                      gmlp_ref, wup_ref, wdn_ref, gpost_ref, o_ref):
    mixed = jnp.concatenate([_rms(oa_ref[...], ga_ref[...]), _rms(ob_ref[...], gb_ref[...]),
                             _rms(oc_ref[...]) * ps_ref[...]], axis=1).astype(bf16)
    x = x_ref[...] + _rms(jnp.dot(mixed, wo_ref[...], preferred_element_type=f32), gpm_ref[...])
    h = _rms(x, gmlp_ref[...]).astype(bf16)
    up = jnp.maximum(jnp.dot(h, wup_ref[...], preferred_element_type=f32), 0.0)
    f = jnp.dot((up * up).astype(bf16), wdn_ref[...], preferred_element_type=f32)
    o_ref[...] = x + _rms(f, gpost_ref[...])


def _merge_mlp_call(x, oa, ob, oc, ga, gb, ps, wo, gpm, gmlp, wup, wdn, gpost):
    T, D = x.shape
    TM = TOKEN_TILE
    row = lambda a: pl.BlockSpec((TM, a.shape[1]), lambda i: (i, 0))
    full = lambda a: pl.BlockSpec(a.shape, lambda i: (0, 0), pipeline_mode=pl.Buffered(1))
    args = (x, oa, ob, oc, ga, gb, ps, wo, gpm, gmlp, wup, wdn, gpost)
    return pl.pallas_call(
        _merge_mlp_kernel,
        grid=(T // TM,),
        in_specs=[row(a) for a in args[:4]] + [full(a) for a in args[4:]],
        out_specs=pl.BlockSpec((TM, D), lambda i: (i, 0)),
        out_shape=jax.ShapeDtypeStruct((T, D), f32),
        compiler_params=pltpu.CompilerParams(dimension_semantics=("arbitrary",),
                                             vmem_limit_bytes=VMEM_LIMIT),
        name="merge_mlp",
    )(*args)


def _rope_lane_tables(positions, rot_dim, period):
    half = rot_dim // 2
    inv_freq = ROPE_THETA ** (-jnp.arange(0, rot_dim, 2, dtype=f32) / rot_dim)
    ang = positions.astype(f32).reshape(-1, 1) * inv_freq
    cos, sin = jnp.cos(ang), jnp.sin(ang)
    T = ang.shape[0]
    ones, zeros, zh = jnp.ones((T, period - rot_dim), f32), jnp.zeros((T, period - rot_dim), f32), jnp.zeros((T, half), f32)
    c = jnp.concatenate([cos, cos, ones], axis=1)
    sa = jnp.concatenate([-sin, zh, zeros], axis=1)
    sb = jnp.concatenate([zh, sin, zeros], axis=1)
    reps = LANES // period
    return jnp.concatenate([jnp.tile(t, (1, reps)) for t in (c, sa, sb)], axis=1)


def _dup_halves(w):
    D = w.shape[0]
    g = w.reshape(D, -1, 1, HEAD_DIM)
    return jnp.broadcast_to(g, (D, g.shape[1], 2, HEAD_DIM)).reshape(D, -1)


def _layout_w_in(w_in):
    D = w_in.shape[0]
    widths = (256, 128, 128, 32, 4, 512, 128, 128, 128, 128, 128, 128, 24, 256)
    offs = [0]
    for w_ in widths:
        offs.append(offs[-1] + w_)
    (q_a, ckv, q_idx, k_idx, w_idx, q_b, k_cmp, v_cmp, k_slc, v_slc, k_win, v_win, gates, u_c) = [
        w_in[:, offs[i]:offs[i + 1]] for i in range(len(widths))]
    misc = jnp.concatenate([w_idx, gates, jnp.zeros((D, LANES - 28), w_in.dtype)], axis=1)
    cols = [q_a, ckv, q_idx, jnp.tile(k_idx, (1, LANES // IDX_DIM)), q_b, k_cmp, v_cmp,
            _dup_halves(k_slc), v_slc, _dup_halves(k_win), v_win, misc, u_c]
    w = jnp.concatenate(cols, axis=1)
    assert w.shape[1] == N_PROJ
    return w.astype(bf16)


def kernel(x, positions, w_in, g_kv_a, w_kv_up_a, w_cmp_k, w_cmp_v, pe_cmp_k, pe_cmp_v, w_pool,
           pool_scale, g_out_a, g_out_b, w_out, g_pre_mix, g_post_mix, g_pre_mlp, g_post_mlp,
           w_up, w_down):
    B, S, D = x.shape
    depth = w_in.shape[0]
    assert S % (SPAN_CHUNKS * KEY_CHUNK) == 0 and QUERY_TILE == KEY_CHUNK
    assert (B * S) % TOKEN_TILE == 0
    rope_h = _rope_lane_tables(positions, ROT_DIM_HEAD, HEAD_DIM)
    rope_i = _rope_lane_tables(positions, ROT_DIM_IDX, IDX_DIM)
    row = lambda v: v.reshape(1, -1)
    xt = x.reshape(B * S, D)
    for l in range(depth):
        wkv = jnp.concatenate([_dup_halves(w_kv_up_a[l][:, :HEAD_DIM]), w_kv_up_a[l][:, HEAD_DIM:],
                               jnp.zeros((KV_RANK_A, LANES - HEAD_DIM), f32)], axis=1).astype(bf16)
        pr = _proj_call(xt, row(g_pre_mix[l]), _layout_w_in(w_in[l]), row(g_kv_a[l]), wkv, rope_h, rope_i)
        o_a = _dsa_call(pr, B, S)
        kc, vc = _compress_call(pr["kvcmp"], pe_cmp_k[l], pe_cmp_v[l], w_cmp_k[l], w_cmp_v[l], B, S)
        o_b = _nsa_call(pr, kc, vc, B, S)
        o_c = _pool_call(pr["uc"], w_pool[l], B, S)
        xt = _merge_mlp_call(xt, o_a, o_b, o_c, row(g_out_a[l]), row(g_out_b[l]), row(pool_scale[l]),
                             w_out[l].astype(bf16), row(g_post_mix[l]), row(g_pre_mlp[l]),
                             w_up[l].astype(bf16), w_down[l].astype(bf16), row(g_post_mlp[l]))
    return xt.reshape(B, S, D)
```

```python
import functools

import jax
import jax.numpy as jnp
from jax import lax
from jax.experimental import pallas as pl
from jax.experimental.pallas import tpu as pltpu

f32 = jnp.float32
bf16 = jnp.bfloat16

HEAD_DIM = 64
N_HEADS_A = 4
N_HEADS_B = 8
N_KV_GROUPS_B = 2
N_POOL_GROUPS = 4
POOL_GROUP_DIM = 64
POOL_WINDOWS = (2, 4, 8, 16)
KV_RANK_A = 128
N_IDX_HEADS = 4
IDX_DIM = 32
TOPK_MAX = 256
CMP_BLOCK = 32
CMP_STRIDE = 16
SLC_BLOCK = 64
N_SLC = 16
WINDOW = 512
ROPE_THETA = 500000.0
ROT_DIM_HEAD = HEAD_DIM // 4
ROT_DIM_IDX = IDX_DIM // 4
EPS = 1e-6
NEG = -1e30
SCALE = HEAD_DIM ** -0.5

LANES = 128
TOKEN_TILE = 512
QUERY_TILE = 256
KEY_CHUNK = 256
BF16_ROWS = 16
DENOM_ROWS = BF16_ROWS
LOG2E = 1.4426950408889634
LOGITS_LEAD = 3
SPAN_CHUNKS = 2
VMEM_LIMIT = 56 * 1024 * 1024

_COLS = {}
_off = 0
for _name, _w in (("qa", 256), ("ckv", 128), ("qidx", 128), ("kidx", 128), ("qb", 512),
                  ("kvcmp", 256), ("kslc", 256), ("vslc", 128), ("kwin", 256), ("vwin", 128),
                  ("misc", 128), ("uc", 256)):
    _COLS[_name] = (_off, _off + _w)
    _off += _w
N_PROJ = _off
GATE_LANE0 = N_IDX_HEADS


def _rms(v, gain=None):
    y = v * lax.rsqrt(jnp.mean(v * v, axis=-1, keepdims=True) + EPS)
    return y if gain is None else y * gain


def _rope128(v, c, sa, sb, half):
    return v * c + pltpu.roll(v, LANES - half, 1) * sa + pltpu.roll(v, half, 1) * sb


def _nt_dot(a, b):
    return lax.dot_general(a, b, (((1,), (1,)), ((), ())), preferred_element_type=f32)


def _proj_kernel(x_ref, g_ref, w_ref, gkv_ref, wkv_ref, rh_ref, ri_ref,
                 qa_ref, ka_ref, vat_ref, qidx_ref, kidx_ref, qbn_ref, qbr_ref,
                 kcmp_ref, vcmp_ref, kslc_ref, vslct_ref, kwin_ref, vwint_ref, misc_ref, uc_ref):
    h = _rms(x_ref[...], g_ref[...])
    p = jnp.dot(h.astype(bf16), w_ref[...], preferred_element_type=f32)
    hc, hsa, hsb = rh_ref[:, 0:128], rh_ref[:, 128:256], rh_ref[:, 256:384]
    ic, isa, isb = ri_ref[:, 0:128], ri_ref[:, 128:256], ri_ref[:, 256:384]
    hh, ih = ROT_DIM_HEAD // 2, ROT_DIM_IDX // 2

    def col(name, j=0):
        a = _COLS[name][0] + j * LANES
        return p[:, a:a + LANES]

    for j in range(2):
        qa_ref[:, j * LANES:(j + 1) * LANES] = (_rope128(col("qa", j), hc, hsa, hsb, hh)
                                                * (SCALE * LOG2E)).astype(bf16)
    kv = jnp.dot(_rms(col("ckv"), gkv_ref[...]).astype(bf16), wkv_ref[...], preferred_element_type=f32)
    ka_ref[...] = _rope128(kv[:, 0:LANES], hc, hsa, hsb, hh).astype(bf16)
    for cc in range(vat_ref.shape[0]):
        tok = slice(cc * KEY_CHUNK, (cc + 1) * KEY_CHUNK)
        vat_ref[cc] = kv[tok, LANES:2 * LANES].T[0:HEAD_DIM, :].astype(bf16)
        vslct_ref[cc] = col("vslc")[tok, :].T.astype(bf16)
        vwint_ref[cc] = col("vwin")[tok, :].T.astype(bf16)
    qidx_ref[...] = _rope128(col("qidx"), ic, isa, isb, ih).astype(bf16)
    kidx_ref[...] = _rope128(col("kidx"), ic, isa, isb, ih).astype(bf16)
    for j in range(4):
        q = col("qb", j)
        qbn_ref[:, j * LANES:(j + 1) * LANES] = (q * SCALE).astype(bf16)
        qbr_ref[:, j * LANES:(j + 1) * LANES] = (_rope128(q, hc, hsa, hsb, hh) * (SCALE * LOG2E)).astype(bf16)
    for j in range(2):
        sl = slice(j * LANES, (j + 1) * LANES)
        (kcmp_ref, vcmp_ref)[j][...] = col("kvcmp", j)
        kslc_ref[:, sl] = _rope128(col("kslc", j), hc, hsa, hsb, hh).astype(bf16)
        kwin_ref[:, sl] = _rope128(col("kwin", j), hc, hsa, hsb, hh).astype(bf16)
        uc_ref[:, sl] = col("uc", j)
    misc_ref[...] = col("misc")


def _proj_call(x, g, w, gkv, wkv, rope_h, rope_i):
    T, D = x.shape
    TM = TOKEN_TILE
    row = lambda w_: pl.BlockSpec((TM, w_), lambda i: (i, 0))
    full = lambda a: pl.BlockSpec(a.shape, lambda i: (0, 0))
    outs = [("qa", 256, bf16), ("ka", 128, bf16), ("vat", -HEAD_DIM, bf16), ("qidx", 128, bf16),
            ("kidx", 128, bf16), ("qbn", 512, bf16), ("qbr", 512, bf16), ("kcmp", 128, f32), ("vcmp", 128, f32),
            ("kslc", 256, bf16), ("vslct", -LANES, bf16), ("kwin", 256, bf16), ("vwint", -LANES, bf16),
            ("misc", 128, f32), ("uc", 256, f32)]
    tspec = lambda d: pl.BlockSpec((TM // KEY_CHUNK, d, KEY_CHUNK), lambda i: (i, 0, 0))
    tshape = lambda d: jax.ShapeDtypeStruct((T // KEY_CHUNK, d, KEY_CHUNK), bf16)
    res = pl.pallas_call(
        _proj_kernel,
        grid=(T // TM,),
        in_specs=[row(D), full(g), full(w), full(gkv), full(wkv), row(384), row(384)],
        out_specs=[tspec(-w_) if w_ < 0 else row(w_) for _, w_, _ in outs],
        out_shape=[tshape(-w_) if w_ < 0 else jax.ShapeDtypeStruct((T, w_), dt) for _, w_, dt in outs],
        compiler_params=pltpu.CompilerParams(dimension_semantics=("arbitrary",),
                                             vmem_limit_bytes=VMEM_LIMIT),
        name="proj",
    )(x, g, w, gkv, wkv, rope_h, rope_i)
    return dict(zip([n for n, _, _ in outs], res))


KEY_OF_NEG_MAX = -(2 ** 31 - 2 ** 23)
HALF_BITS = 16
COUNT_SLAB = 32
COUNT16_SLAB = 64


def _column_max(s):
    m = s[0:BF16_ROWS, :]
    for r in range(1, s.shape[0] // BF16_ROWS):
        m = jnp.maximum(m, s[r * BF16_ROWS:(r + 1) * BF16_ROWS, :])
    return jnp.max(m.astype(f32), axis=0, keepdims=True)


def _attend_spans(prepare_span, block_operands, values_t, s_ref, acc_ref, p_lo, p_hi, extra):
    n_heads = s_ref.shape[0]
    TQ = s_ref.shape[2]
    span = SPAN_CHUNKS * KEY_CHUNK
    ones = jnp.ones((acc_ref.shape[1] - HEAD_DIM, span), bf16)
    acc_ref[...] = jnp.zeros_like(acc_ref)

    def step(p, st):
        maxes, extra = st
        extra = prepare_span(p, extra)
        new_maxes = [None] * n_heads

        def logits(h):
            blocks = [block_operands(h, p, j) for j in range(span // LANES)]
            raw = [_nt_dot(kb, q) for kb, q, _ in blocks]
            masked = [r.astype(bf16) + bias for r, (_, _, bias) in zip(raw, blocks)]
            for j, s in enumerate(masked):
                s_ref[h, j * LANES:(j + 1) * LANES, :] = s
            m_new = maxes[h]
            for s in masked:
                m_new = jnp.maximum(m_new, _column_max(s))
            new_maxes[h] = m_new

        def weigh(h):
            vt = jnp.concatenate([values_t(h, SPAN_CHUNKS * p + j) for j in range(SPAN_CHUNKS)], axis=1)
            e = jnp.exp2(s_ref[h] - new_maxes[h].astype(bf16))
            pv = jnp.dot(jnp.concatenate([vt, ones], axis=0), e, preferred_element_type=f32)
            acc_ref[h] = jnp.exp2(maxes[h] - new_maxes[h]) * acc_ref[h] + pv

        for h in range(n_heads + LOGITS_LEAD):
            if h < n_heads:
                logits(h)
            if h >= LOGITS_LEAD:
                weigh(h - LOGITS_LEAD)
        return tuple(new_maxes), extra

    lax.fori_loop(p_lo, p_hi, step, ((jnp.full((1, TQ), -3.0e38, f32),) * n_heads, extra))
    return [acc_ref[h, 0:HEAD_DIM, :] / acc_ref[h, HEAD_DIM:HEAD_DIM + 1, :] for h in range(n_heads)]


def _ordered_key(v):
    bits = lax.bitcast_convert_type(v, jnp.int32)
    return bits ^ ((bits >> 31) & 0x7FFFFFFF)


def _dsa_kernel(qa_ref, qidx_ref, misc_ref, kidx_ref, ka_ref, vat_ref, o_ref,
                key_ref, s_ref, acc_ref, qm_ref, bias_ref, k16_ref, *, n_keep):
    TQ = qa_ref.shape[0]
    CH = KEY_CHUNK
    q0 = pl.program_id(1) * TQ
    tpos = q0 + lax.broadcasted_iota(jnp.int32, (1, TQ), 1)
    lane = lax.broadcasted_iota(jnp.int32, (1, LANES), 1)
    k = float(n_keep)

    SUB = LANES
    span = SPAN_CHUNKS * CH
    subs_per_span = span // SUB
    n_span = (q0 + TQ + span - 1) // span

    def sub_rows(i):
        return pl.ds(pl.multiple_of(i * SUB, SUB), SUB)

    w_t = misc_ref[...].T[0:8, :]

    def index_q(h):
        qidx = qidx_ref[...]
        return jnp.where(lane // IDX_DIM == h, qidx, jnp.zeros_like(qidx))

    def score_span(p, carry):
        for i in [p * subs_per_span + j for j in range(subs_per_span)]:
            kb = kidx_ref[sub_rows(i), :]
            sc = jnp.zeros((SUB, TQ), f32)
            for h in range(N_IDX_HEADS):
                sc = sc + jnp.maximum(_nt_dot(kb, index_q(h)), 0.0) * w_t[h:h + 1, :]
            kpos = i * SUB + lax.broadcasted_iota(jnp.int32, (SUB, 1), 0)
            sc = jnp.where(sc == 0.0, 0.0, sc)
            key = _ordered_key(jnp.where(kpos <= tpos, sc, -jnp.inf))
            key_ref[sub_rows(i), :] = key
            k16_ref[sub_rows(i), :] = (key >> HALF_BITS).astype(jnp.int16)
        return carry

    lax.fori_loop(0, n_span, score_span, 0)

    need = (tpos + 1) > n_keep
    half = 1 << (HALF_BITS - 1)

    def search(spans):
        def count_ge(v):
            vs = jnp.broadcast_to(v, (COUNT_SLAB, TQ))
            acc = jnp.zeros((COUNT_SLAB, TQ), jnp.int32)
            for r in range(spans * span // COUNT_SLAB):
                acc = acc + jnp.where(key_ref[r * COUNT_SLAB:(r + 1) * COUNT_SLAB, :] >= vs, 1, 0)
            return jnp.sum(acc.astype(f32), axis=0, keepdims=True)

        def count16_ge(v):
            vs = jnp.broadcast_to(v.astype(jnp.int16), (COUNT16_SLAB, TQ))
            one, zero = jnp.int16(1), jnp.int16(0)
            acc = jnp.zeros((COUNT16_SLAB, TQ), jnp.int16)
            for r in range(spans * span // COUNT16_SLAB):
                acc = acc + jnp.where(k16_ref[r * COUNT16_SLAB:(r + 1) * COUNT16_SLAB, :] >= vs, one, zero)
            return jnp.sum(acc.astype(jnp.int32).astype(f32), axis=0, keepdims=True)

        def bisect16(_, st):
            lo, hi, at, hit, active = st
            act = active > 0
            mid = (lo + hi) >> 1
            collapsed = (hi - lo) <= 1
            c = count16_ge(mid)
            exact = act & ~collapsed & (c == k)
            at = jnp.where(exact, mid, at)
            hit = jnp.where(exact, 1, hit)
            go = act & ~collapsed & ~exact
            lo = jnp.where(go & (c > k), mid, lo)
            hi = jnp.where(go & (c < k), mid, hi)
            return lo, hi, at, hit, jnp.where(go, 1, 0)

        lo0 = jnp.full((1, TQ), -half, jnp.int32)
        hi0 = jnp.full((1, TQ), half, jnp.int32)
        zero = jnp.zeros((1, TQ), jnp.int32)
        st = lax.fori_loop(0, HALF_BITS, bisect16, (lo0, hi0, zero, zero, jnp.where(need, 1, 0)))
        prefix, at_hi, hit_hi = st[0], st[2], st[3] > 0

        for i in range(spans * subs_per_span):
            key = key_ref[i * SUB:(i + 1) * SUB, :]
            hi_half = key >> HALF_BITS
            low = (key & ((1 << HALF_BITS) - 1)) - half
            low = jnp.where(hi_half > prefix, half - 1, jnp.where(hi_half < prefix, -half, low))
            k16_ref[i * SUB:(i + 1) * SUB, :] = low.astype(jnp.int16)

        st = lax.fori_loop(0, HALF_BITS, bisect16, (lo0, hi0, zero, zero, jnp.where(need & ~hit_hi, 1, 0)))
        low_thr = jnp.where(st[3] > 0, st[2], st[0]) + half
        thr = jnp.where(hit_hi, at_hi << HALF_BITS, (prefix << HALF_BITS) | low_thr)
        thr = jnp.where(need, thr, KEY_OF_NEG_MAX)
        return thr, k - count_ge(thr + 1)

    max_spans = key_ref.shape[0] // span
    thr, needed = lax.switch(n_span - 1, [functools.partial(search, n) for n in range(1, max_spans + 1)])

    ri = lax.broadcasted_iota(jnp.int32, (SUB, SUB), 0)
    rj = lax.broadcasted_iota(jnp.int32, (SUB, SUB), 1)
    strict_lower = jnp.where(rj < ri, 1.0, 0.0).astype(bf16)
    for h in range(N_HEADS_A):
        qc = qa_ref[:, (h // 2) * LANES:(h // 2 + 1) * LANES]
        qm_ref[h] = jnp.where(lane // HEAD_DIM == h % 2, qc, jnp.zeros_like(qc))

    def prepare_span(p, ties_before):
        for j in range(subs_per_span):
            keys = key_ref[sub_rows(p * subs_per_span + j), :]
            tie = keys == thr
            tie_f = jnp.where(tie, 1.0, 0.0)
            rank = jnp.dot(strict_lower, tie_f.astype(bf16), preferred_element_type=f32) + ties_before
            sel = (keys > thr) | (tie & (rank < needed))
            bias_ref[j * SUB:(j + 1) * SUB, :] = jnp.where(sel, 0.0, NEG).astype(bf16)
            ties_before = ties_before + jnp.sum(tie_f, axis=0, keepdims=True)
        return ties_before

    def block_operands(h, p, j):
        return ka_ref[sub_rows(p * subs_per_span + j), :], qm_ref[h], bias_ref[j * SUB:(j + 1) * SUB, :]

    outs = _attend_spans(prepare_span, block_operands, lambda h, c: vat_ref[c], s_ref, acc_ref, 0, n_span,
                         jnp.zeros((1, TQ), f32))
    o_ref[...] = jnp.concatenate(outs, axis=0).T


def _dsa_call(pr, B, S):
    TQ = min(QUERY_TILE, S)
    n_keep = min(TOPK_MAX, S // 4)
    n_chunks = S // KEY_CHUNK
    qspec = lambda w_: pl.BlockSpec((TQ, w_), lambda b, i: (b * (S // TQ) + i, 0))
    kspec = lambda w_: pl.BlockSpec((S, w_), lambda b, i: (b, 0))
    vspec = pl.BlockSpec((n_chunks, HEAD_DIM, KEY_CHUNK), lambda b, i: (b, 0, 0))
    return pl.pallas_call(
        functools.partial(_dsa_kernel, n_keep=n_keep),
        grid=(B, S // TQ),
        in_specs=[qspec(256), qspec(128), qspec(128), kspec(128), kspec(128), vspec],
        out_specs=qspec(256),
        out_shape=jax.ShapeDtypeStruct((B * S, 256), f32),
        scratch_shapes=[pltpu.VMEM((S, TQ), jnp.int32),
                        pltpu.VMEM((N_HEADS_A, SPAN_CHUNKS * KEY_CHUNK, TQ), bf16),
                        pltpu.VMEM((N_HEADS_A, HEAD_DIM + DENOM_ROWS, TQ), f32),
                        pltpu.VMEM((N_HEADS_A, TQ, LANES), bf16),
                        pltpu.VMEM((SPAN_CHUNKS * KEY_CHUNK, TQ), bf16),
                        pltpu.VMEM((S, TQ), jnp.int16)],
        compiler_params=pltpu.CompilerParams(dimension_semantics=("arbitrary", "arbitrary"),
                                             vmem_limit_bytes=VMEM_LIMIT),
        name="dsa",
    )(pr["qa"], pr["qidx"], pr["misc"], pr["kidx"], pr["ka"], pr["vat"])


def _compress_kernel(k_ref, v_ref, pe_ref, w_ref, kc_ref, vct_ref):
    nch = k_ref.shape[0] // CMP_STRIDE
    rowid = lax.broadcasted_iota(jnp.int32, (nch, 1), 0)
    for kv, x_ref in enumerate((k_ref, v_ref)):
        top = jnp.zeros((nch, w_ref.shape[3]), f32)
        bot = jnp.zeros((nch, w_ref.shape[3]), f32)
        for l in range(CMP_STRIDE):
            x = x_ref[pl.ds(l, nch, stride=CMP_STRIDE), :]
            top = top + jnp.dot((x + pe_ref[kv, l:l + 1, :]).astype(bf16), w_ref[kv, l],
                                preferred_element_type=f32)
            bot = bot + jnp.dot((x + pe_ref[kv, CMP_STRIDE + l:CMP_STRIDE + l + 1, :]).astype(bf16),
                                w_ref[kv, CMP_STRIDE + l], preferred_element_type=f32)
        blk = top + pltpu.roll(bot, nch - 1, 0)
        blk = jnp.where(rowid < nch - 1, blk, 0.0)
        if kv == 0:
            kc_ref[...] = blk.astype(bf16)
        else:
            for g in range(N_KV_GROUPS_B):
                vct_ref[g] = blk[:, g * LANES:(g + 1) * LANES].T[0:HEAD_DIM, :].astype(bf16)


def _compress_call(kcmp, vcmp, pe_k, pe_v, w_k, w_v, B, S):
    nch = S // CMP_STRIDE
    G, dh = N_KV_GROUPS_B, HEAD_DIM
    pe = jnp.stack([jnp.concatenate([p] * G, axis=1) for p in (pe_k, pe_v)])
    w = jnp.zeros((2, CMP_BLOCK, G * dh, G * LANES), f32)
    for kv, ws in enumerate((w_k, w_v)):
        for g in range(G):
            w = w.at[kv, :, g * dh:(g + 1) * dh, g * LANES:(g + 1) * LANES].set(
                jnp.concatenate([ws, ws], axis=-1))
    w = w.astype(bf16)
    xspec = pl.BlockSpec((S, G * dh), lambda b: (b, 0))
    return pl.pallas_call(
        _compress_kernel,
        grid=(B,),
        in_specs=[xspec, xspec,
                  pl.BlockSpec(pe.shape, lambda b: (0, 0, 0)),
                  pl.BlockSpec(w.shape, lambda b: (0, 0, 0, 0))],
        out_specs=[pl.BlockSpec((nch, G * LANES), lambda b: (b, 0)),
                   pl.BlockSpec((G, dh, nch), lambda b: (b, 0, 0))],
        out_shape=[jax.ShapeDtypeStruct((B * nch, G * LANES), bf16),
                   jax.ShapeDtypeStruct((B * G, dh, nch), bf16)],
        compiler_params=pltpu.CompilerParams(dimension_semantics=("arbitrary",),
                                             vmem_limit_bytes=VMEM_LIMIT),
        name="compress",
    )(kcmp, vcmp, pe, w)


def _split3(v):
    a = v.astype(bf16)
    r = v - a.astype(f32)
    b = r.astype(bf16)
    c = (r - b.astype(f32)).astype(bf16)
    return a, b, c


def _nsa_kernel(qn_ref, qr_ref, misc_ref, kc_ref, vct_ref, kslc_ref, vslct_ref, kwin_ref, vwint_ref,
                o_ref, s_ref, acc_ref, blk_ref, out_ref, qm_ref, bias_ref, *, n_sel):
    TQ = qn_ref.shape[0]
    CH = KEY_CHUNK
    SUB = LANES
    S = kslc_ref.shape[0]
    nch = kc_ref.shape[0]
    n_cmp = nch - 1
    n_blk = S // SLC_BLOCK
    G = N_KV_GROUPS_B
    R = N_HEADS_B // G
    q0 = pl.program_id(1) * TQ
    span = SPAN_CHUNKS * CH
    subs_per_span = span // SUB
    n_span = (q0 + TQ + span - 1) // span
    first_win = jnp.maximum(q0 - WINDOW, 0) // span
    tpos = q0 + lax.broadcasted_iota(jnp.int32, (1, TQ), 1)
    lane = lax.broadcasted_iota(jnp.int32, (1, LANES), 1)
    gate = jax.nn.sigmoid(misc_ref[...].T[0:32, :])

    def sub_rows(i):
        return pl.ds(pl.multiple_of(i * SUB, SUB), SUB)

    def gate_of(h, branch):
        a = GATE_LANE0 + 3 * h + branch
        return gate[a:a + 1, :]

    def head_q(ref, h):
        qc = ref[:, (h // 2) * LANES:(h // 2 + 1) * LANES]
        return jnp.where(lane // HEAD_DIM == h % 2, qc, jnp.zeros_like(qc))

    def attend(k_ref, vt_ref, biases_of, p_lo, p_hi):
        def prepare_span(p, extra):
            for j in range(subs_per_span):
                biases = biases_of(p * subs_per_span + j)
                for g in range(G):
                    bias_ref[g, j * SUB:(j + 1) * SUB, :] = biases[g]
            return extra

        def block_operands(h, p, j):
            g = h // R
            return (k_ref[sub_rows(p * subs_per_span + j), g * LANES:(g + 1) * LANES], qm_ref[h],
                    bias_ref[g, j * SUB:(j + 1) * SUB, :])

        def values_t(h, c):
            g = h // R
            return vt_ref[c, g * HEAD_DIM:(g + 1) * HEAD_DIM, :]

        return _attend_spans(prepare_span, block_operands, values_t, s_ref, acc_ref, p_lo, p_hi, 0)

    cpos = lax.broadcasted_iota(jnp.int32, (nch, 1), 0)
    cmask = (cpos * CMP_STRIDE + CMP_BLOCK - 1 <= tpos) & (cpos < n_cmp)
    oj = lax.broadcasted_iota(jnp.int32, (n_blk, nch), 0)
    oc = lax.broadcasted_iota(jnp.int32, (n_blk, nch), 1)
    overlap_t = ((oc * CMP_STRIDE <= oj * SLC_BLOCK + SLC_BLOCK - 1)
                 & (oc * CMP_STRIDE + CMP_BLOCK - 1 >= oj * SLC_BLOCK) & (oc < n_cmp))
    overlap_t = jnp.where(overlap_t, 1.0, 0.0).astype(bf16)
    bj = lax.broadcasted_iota(jnp.int32, (n_blk, 1), 0)
    cur = tpos // SLC_BLOCK
    forced = (bj == 0) | (bj == cur) | (bj == cur - 1)
    admissible = bj * SLC_BLOCK <= tpos

    for h in range(N_HEADS_B):
        qm_ref[h] = head_q(qr_ref, h)

    def window_biases(i):
        kpos = i * SUB + lax.broadcasted_iota(jnp.int32, (SUB, 1), 0)
        return [jnp.where((kpos <= tpos) & (kpos > tpos - WINDOW), 0.0, NEG).astype(bf16)] * G

    def selected_biases(i):
        per_sub = SUB // SLC_BLOCK
        kpos = i * SUB + lax.broadcasted_iota(jnp.int32, (SUB, 1), 0)
        biases = []
        for g in range(G):
            parts = []
            for jj in range(per_sub):
                b8 = blk_ref[g * n_blk + i * per_sub + jj]
                parts.extend([b8] * (SLC_BLOCK // b8.shape[0]))
            biases.append(jnp.where(kpos <= tpos, jnp.concatenate(parts, axis=0), NEG).astype(bf16))
        return biases

    all_heads = range(N_HEADS_B)
    lcs = [jnp.where(cmask, _nt_dot(kc_ref[:, (h // R) * LANES:(h // R + 1) * LANES], head_q(qn_ref, h)), NEG)
           for h in all_heads]
    cmaxes = [jnp.max(lc, axis=0, keepdims=True) for lc in lcs]
    es = [jnp.where(cmask, jnp.exp(lc - m), 0.0) for lc, m in zip(lcs, cmaxes)]
    sums = [jnp.sum(e, axis=0, keepdims=True) for e in es]
    ps = [e * (1.0 / jnp.where(l > 0.0, l, 1.0)) for e, l in zip(es, sums)]
    for h in all_heads:
        out_ref[h] = gate_of(h, 0) * jnp.dot(vct_ref[h // R], ps[h].astype(bf16), preferred_element_type=f32)
    psums = [sum(ps[g * R:(g + 1) * R]) for g in range(G)]
    s_blks = [sum(jnp.dot(overlap_t, piece, preferred_element_type=f32) for piece in _split3(psum))
              for psum in psums]
    s_blks = [jnp.where(forced, jnp.inf, jnp.where(admissible, s_blk, -jnp.inf)) for s_blk in s_blks]
    ranks = [jnp.zeros((n_blk, TQ), f32) for _ in range(G)]
    for i in range(n_blk):
        for g in range(G):
            row = s_blks[g][i:i + 1, :]
            beats = (row > s_blks[g]) | ((row == s_blks[g]) & (bj > i))
            ranks[g] = ranks[g] + jnp.where(beats, 1.0, 0.0)
    for g in range(G):
        chosen_bias = jnp.where(ranks[g] < float(n_sel), 0.0, NEG)
        for j in range(n_blk):
            blk_ref[g * n_blk + j] = jnp.broadcast_to(chosen_bias[j:j + 1, :], blk_ref.shape[1:])

    for branch, k_ref, vt_ref, biases_of, p_lo in ((1, kslc_ref, vslct_ref, selected_biases, 0),
                                                   (2, kwin_ref, vwint_ref, window_biases, first_win)):
        o_branch = attend(k_ref, vt_ref, biases_of, p_lo, n_span)
        for h in range(N_HEADS_B):
            out_ref[h] += gate_of(h, branch) * o_branch[h]
    o_ref[...] = jnp.concatenate([out_ref[h] for h in range(N_HEADS_B)], axis=0).T


def _nsa_call(pr, kc, vct, B, S):
    TQ = min(QUERY_TILE, S)
    n_sel = min(N_SLC, S // SLC_BLOCK)
    nch = S // CMP_STRIDE
    n_chunks = S // KEY_CHUNK
    G, R = N_KV_GROUPS_B, N_HEADS_B // N_KV_GROUPS_B
    qspec = lambda w_: pl.BlockSpec((TQ, w_), lambda b, i: (b * (S // TQ) + i, 0))
    kspec = lambda w_: pl.BlockSpec((S, w_), lambda b, i: (b, 0))
    vspec = pl.BlockSpec((n_chunks, G * HEAD_DIM, KEY_CHUNK), lambda b, i: (b, 0, 0))
    return pl.pallas_call(
        functools.partial(_nsa_kernel, n_sel=n_sel),
        grid=(B, S // TQ),
        in_specs=[qspec(512), qspec(512), qspec(128),
                  pl.BlockSpec((nch, G * LANES), lambda b, i: (b, 0)),
                  pl.BlockSpec((G, HEAD_DIM, nch), lambda b, i: (b, 0, 0)),
                  kspec(256), vspec, kspec(256), vspec],
        out_specs=qspec(512),
        out_shape=jax.ShapeDtypeStruct((B * S, 512), f32),
        scratch_shapes=[pltpu.VMEM((N_HEADS_B, SPAN_CHUNKS * KEY_CHUNK, TQ), bf16),
                        pltpu.VMEM((N_HEADS_B, HEAD_DIM + DENOM_ROWS, TQ), f32),
                        pltpu.VMEM((G * (S // SLC_BLOCK), 8, TQ), f32),
                        pltpu.VMEM((N_HEADS_B, HEAD_DIM, TQ), f32),
                        pltpu.VMEM((N_HEADS_B, TQ, LANES), bf16),
                        pltpu.VMEM((G, SPAN_CHUNKS * KEY_CHUNK, TQ), bf16)],
        compiler_params=pltpu.CompilerParams(dimension_semantics=("arbitrary", "arbitrary"),
                                             vmem_limit_bytes=VMEM_LIMIT),
        name="nsa",
    )(pr["qbn"], pr["qbr"], pr["misc"], kc, vct, pr["kslc"], pr["vslct"], pr["kwin"], pr["vwint"])


def _pool_kernel(u_ref, w_ref, o_ref):
    u = u_ref[...]
    S, C = u.shape
    row = lax.broadcasted_iota(jnp.int32, (S, 1), 0)
    grp = lax.broadcasted_iota(jnp.int32, (1, C), 1) // POOL_GROUP_DIM
    run, width = u, 1
    win_sum = jnp.zeros_like(u)
    win_len = jnp.zeros((1, C), f32)
    for gi, w in enumerate(POOL_WINDOWS):
        while width < w:
            run = run + jnp.where(row >= width, pltpu.roll(run, width, 0), 0.0)
            width *= 2
        assert width == w, "pooling windows must be increasing powers of two"
        win_sum = jnp.where(grp == gi, run, win_sum)
        win_len = jnp.where(grp == gi, float(w), win_len)
    pooled = win_sum / jnp.minimum((row + 1).astype(f32), win_len) - u
    o_ref[...] = jnp.dot(pooled.astype(bf16), w_ref[...], preferred_element_type=f32)


def _pool_call(uc, w_pool, B, S):
    C = N_POOL_GROUPS * POOL_GROUP_DIM
    wbd = jnp.zeros((C, C), f32)
    for gi in range(N_POOL_GROUPS):
        sl = slice(gi * POOL_GROUP_DIM, (gi + 1) * POOL_GROUP_DIM)
        wbd = wbd.at[sl, sl].set(w_pool[gi])
    spec = pl.BlockSpec((S, C), lambda b: (b, 0))
    return pl.pallas_call(
        _pool_kernel,
        grid=(B,),
        in_specs=[spec, pl.BlockSpec((C, C), lambda b: (0, 0))],
        out_specs=spec,
        out_shape=jax.ShapeDtypeStruct((B * S, C), f32),
        compiler_params=pltpu.CompilerParams(dimension_semantics=("arbitrary",),
                                             vmem_limit_bytes=VMEM_LIMIT),
        name="pool",
    )(uc, wbd.astype(bf16))


def _merge_mlp_kernel(x_ref, oa_ref, ob_ref, oc_ref, ga_ref, gb_ref, ps_ref, wo_ref, gpm_ref,
                      gmlp_ref, wup_ref, wdn_ref, gpost_ref, o_ref):
    mixed = jnp.concatenate([_rms(oa_ref[...], ga_ref[...]), _rms(ob_ref[...], gb_ref[...]),
                             _rms(oc_ref[...]) * ps_ref[...]], axis=1).astype(bf16)
    x = x_ref[...] + _rms(jnp.dot(mixed, wo_ref[...], preferred_element_type=f32), gpm_ref[...])
    h = _rms(x, gmlp_ref[...]).astype(bf16)
    up = jnp.maximum(jnp.dot(h, wup_ref[...], preferred_element_type=f32), 0.0)
    f = jnp.dot((up * up).astype(bf16), wdn_ref[...], preferred_element_type=f32)
    o_ref[...] = x + _rms(f, gpost_ref[...])


def _merge_mlp_call(x, oa, ob, oc, ga, gb, ps, wo, gpm, gmlp, wup, wdn, gpost):
    T, D = x.shape
    TM = TOKEN_TILE
    row = lambda a: pl.BlockSpec((TM, a.shape[1]), lambda i: (i, 0))
    full = lambda a: pl.BlockSpec(a.shape, lambda i: (0, 0), pipeline_mode=pl.Buffered(1))
    args = (x, oa, ob, oc, ga, gb, ps, wo, gpm, gmlp, wup, wdn, gpost)
    return pl.pallas_call(
        _merge_mlp_kernel,
        grid=(T // TM,),
        in_specs=[row(a) for a in args[:4]] + [full(a) for a in args[4:]],
        out_specs=pl.BlockSpec((TM, D), lambda i: (i, 0)),
        out_shape=jax.ShapeDtypeStruct((T, D), f32),
        compiler_params=pltpu.CompilerParams(dimension_semantics=("arbitrary",),
                                             vmem_limit_bytes=VMEM_LIMIT),
        name="merge_mlp",
    )(*args)


def _rope_lane_tables(positions, rot_dim, period):
    half = rot_dim // 2
    inv_freq = ROPE_THETA ** (-jnp.arange(0, rot_dim, 2, dtype=f32) / rot_dim)
    ang = positions.astype(f32).reshape(-1, 1) * inv_freq
    cos, sin = jnp.cos(ang), jnp.sin(ang)
    T = ang.shape[0]
    ones, zeros, zh = jnp.ones((T, period - rot_dim), f32), jnp.zeros((T, period - rot_dim), f32), jnp.zeros((T, half), f32)
    c = jnp.concatenate([cos, cos, ones], axis=1)
    sa = jnp.concatenate([-sin, zh, zeros], axis=1)
    sb = jnp.concatenate([zh, sin, zeros], axis=1)
    reps = LANES // period
    return jnp.concatenate([jnp.tile(t, (1, reps)) for t in (c, sa, sb)], axis=1)


def _dup_halves(w):
    D = w.shape[0]
    g = w.reshape(D, -1, 1, HEAD_DIM)
    return jnp.broadcast_to(g, (D, g.shape[1], 2, HEAD_DIM)).reshape(D, -1)


def _layout_w_in(w_in):
    D = w_in.shape[0]
    widths = (256, 128, 128, 32, 4, 512, 128, 128, 128, 128, 128, 128, 24, 256)
    offs = [0]
    for w_ in widths:
        offs.append(offs[-1] + w_)
    (q_a, ckv, q_idx, k_idx, w_idx, q_b, k_cmp, v_cmp, k_slc, v_slc, k_win, v_win, gates, u_c) = [
        w_in[:, offs[i]:offs[i + 1]] for i in range(len(widths))]
    misc = jnp.concatenate([w_idx, gates, jnp.zeros((D, LANES - 28), w_in.dtype)], axis=1)
    cols = [q_a, ckv, q_idx, jnp.tile(k_idx, (1, LANES // IDX_DIM)), q_b, k_cmp, v_cmp,
            _dup_halves(k_slc), v_slc, _dup_halves(k_win), v_win, misc, u_c]
    w = jnp.concatenate(cols, axis=1)
    assert w.shape[1] == N_PROJ
    return w.astype(bf16)


def kernel(x, positions, w_in, g_kv_a, w_kv_up_a, w_cmp_k, w_cmp_v, pe_cmp_k, pe_cmp_v, w_pool,
           pool_scale, g_out_a, g_out_b, w_out, g_pre_mix, g_post_mix, g_pre_mlp, g_post_mlp,
           w_up, w_down):
    B, S, D = x.shape
    depth = w_in.shape[0]
    assert S % (SPAN_CHUNKS * KEY_CHUNK) == 0 and QUERY_TILE == KEY_CHUNK
    assert (B * S) % TOKEN_TILE == 0
    rope_h = _rope_lane_tables(positions, ROT_DIM_HEAD, HEAD_DIM)
    rope_i = _rope_lane_tables(positions, ROT_DIM_IDX, IDX_DIM)
    row = lambda v: v.reshape(1, -1)
    xt = x.reshape(B * S, D)
    for l in range(depth):
        wkv = jnp.concatenate([_dup_halves(w_kv_up_a[l][:, :HEAD_DIM]), w_kv_up_a[l][:, HEAD_DIM:],
                               jnp.zeros((KV_RANK_A, LANES - HEAD_DIM), f32)], axis=1).astype(bf16)
        pr = _proj_call(xt, row(g_pre_mix[l]), _layout_w_in(w_in[l]), row(g_kv_a[l]), wkv, rope_h, rope_i)
        o_a = _dsa_call(pr, B, S)
        kc, vc = _compress_call(pr["kcmp"], pr["vcmp"], pe_cmp_k[l], pe_cmp_v[l], w_cmp_k[l], w_cmp_v[l], B, S)
        o_b = _nsa_call(pr, kc, vc, B, S)
        o_c = _pool_call(pr["uc"], w_pool[l], B, S)
        xt = _merge_mlp_call(xt, o_a, o_b, o_c, row(g_out_a[l]), row(g_out_b[l]), row(pool_scale[l]),
                             w_out[l].astype(bf16), row(g_post_mix[l]), row(g_pre_mlp[l]),
                             w_up[l].astype(bf16), w_down[l].astype(bf16), row(g_post_mlp[l]))
    return xt.reshape(B, S, D)
```

```python
import functools

import jax
import jax.numpy as jnp
from jax import lax
from jax.experimental import pallas as pl
from jax.experimental.pallas import tpu as pltpu

f32 = jnp.float32
bf16 = jnp.bfloat16

HEAD_DIM = 64
N_HEADS_A = 4
N_HEADS_B = 8
N_KV_GROUPS_B = 2
N_POOL_GROUPS = 4
POOL_GROUP_DIM = 64
POOL_WINDOWS = (2, 4, 8, 16)
KV_RANK_A = 128
N_IDX_HEADS = 4
IDX_DIM = 32
TOPK_MAX = 256
CMP_BLOCK = 32
CMP_STRIDE = 16
SLC_BLOCK = 64
N_SLC = 16
WINDOW = 512
ROPE_THETA = 500000.0
ROT_DIM_HEAD = HEAD_DIM // 4
ROT_DIM_IDX = IDX_DIM // 4
EPS = 1e-6
NEG = -1e30
SCALE = HEAD_DIM ** -0.5

LANES = 128
TOKEN_TILE = 512
QUERY_TILE = 256
KEY_CHUNK = 256
BF16_ROWS = 16
DENOM_ROWS = BF16_ROWS
LOG2E = 1.4426950408889634
LOGITS_LEAD = 3
SPAN_CHUNKS = 2
VMEM_LIMIT = 56 * 1024 * 1024

_COLS = {}
_off = 0
for _name, _w in (("qa", 256), ("ckv", 128), ("qidx", 128), ("kidx", 128), ("qb", 512),
                  ("kvcmp", 256), ("kslc", 256), ("vslc", 128), ("kwin", 256), ("vwin", 128),
                  ("misc", 128), ("uc", 256)):
    _COLS[_name] = (_off, _off + _w)
    _off += _w
N_PROJ = _off
GATE_LANE0 = N_IDX_HEADS


def _rms(v, gain=None):
    y = v * lax.rsqrt(jnp.mean(v * v, axis=-1, keepdims=True) + EPS)
    return y if gain is None else y * gain


def _rope128(v, c, sa, sb, half):
    return v * c + pltpu.roll(v, LANES - half, 1) * sa + pltpu.roll(v, half, 1) * sb


def _nt_dot(a, b):
    return lax.dot_general(a, b, (((1,), (1,)), ((), ())), preferred_element_type=f32)


def _proj_kernel(x_ref, g_ref, w_ref, gkv_ref, wkv_ref, rh_ref, ri_ref,
                 qa_ref, ka_ref, vat_ref, qidx_ref, kidx_ref, qbn_ref, qbr_ref,
                 kcmp_ref, vcmp_ref, kslc_ref, vslct_ref, kwin_ref, vwint_ref, misc_ref, uc_ref):
    h = _rms(x_ref[...], g_ref[...])
    p = jnp.dot(h.astype(bf16), w_ref[...], preferred_element_type=f32)
    hc, hsa, hsb = rh_ref[:, 0:128], rh_ref[:, 128:256], rh_ref[:, 256:384]
    ic, isa, isb = ri_ref[:, 0:128], ri_ref[:, 128:256], ri_ref[:, 256:384]
    hh, ih = ROT_DIM_HEAD // 2, ROT_DIM_IDX // 2

    def col(name, j=0):
        a = _COLS[name][0] + j * LANES
        return p[:, a:a + LANES]

    for j in range(2):
        qa_ref[:, j * LANES:(j + 1) * LANES] = (_rope128(col("qa", j), hc, hsa, hsb, hh)
                                                * (SCALE * LOG2E)).astype(bf16)
    kv = jnp.dot(_rms(col("ckv"), gkv_ref[...]).astype(bf16), wkv_ref[...], preferred_element_type=f32)
    ka_ref[...] = _rope128(kv[:, 0:LANES], hc, hsa, hsb, hh).astype(bf16)
    for cc in range(vat_ref.shape[0]):
        tok = slice(cc * KEY_CHUNK, (cc + 1) * KEY_CHUNK)
        vat_ref[cc] = kv[tok, LANES:2 * LANES].T[0:HEAD_DIM, :].astype(bf16)
        vslct_ref[cc] = col("vslc")[tok, :].T.astype(bf16)
        vwint_ref[cc] = col("vwin")[tok, :].T.astype(bf16)
    qidx_ref[...] = _rope128(col("qidx"), ic, isa, isb, ih).astype(bf16)
    kidx_ref[...] = _rope128(col("kidx"), ic, isa, isb, ih).astype(bf16)
    for j in range(4):
        q = col("qb", j)
        qbn_ref[:, j * LANES:(j + 1) * LANES] = (q * SCALE).astype(bf16)
        qbr_ref[:, j * LANES:(j + 1) * LANES] = (_rope128(q, hc, hsa, hsb, hh) * (SCALE * LOG2E)).astype(bf16)
    for j in range(2):
        sl = slice(j * LANES, (j + 1) * LANES)
        (kcmp_ref, vcmp_ref)[j][...] = col("kvcmp", j)
        kslc_ref[:, sl] = _rope128(col("kslc", j), hc, hsa, hsb, hh).astype(bf16)
        kwin_ref[:, sl] = _rope128(col("kwin", j), hc, hsa, hsb, hh).astype(bf16)
        uc_ref[:, sl] = col("uc", j)
    misc_ref[...] = col("misc")


def _proj_call(x, g, w, gkv, wkv, rope_h, rope_i):
    T, D = x.shape
    TM = TOKEN_TILE
    row = lambda w_: pl.BlockSpec((TM, w_), lambda i: (i, 0))
    full = lambda a: pl.BlockSpec(a.shape, lambda i: (0, 0))
    outs = [("qa", 256, bf16), ("ka", 128, bf16), ("vat", -HEAD_DIM, bf16), ("qidx", 128, bf16),
            ("kidx", 128, bf16), ("qbn", 512, bf16), ("qbr", 512, bf16), ("kcmp", 128, f32), ("vcmp", 128, f32),
            ("kslc", 256, bf16), ("vslct", -LANES, bf16), ("kwin", 256, bf16), ("vwint", -LANES, bf16),
            ("misc", 128, f32), ("uc", 256, f32)]
    tspec = lambda d: pl.BlockSpec((TM // KEY_CHUNK, d, KEY_CHUNK), lambda i: (i, 0, 0))
    tshape = lambda d: jax.ShapeDtypeStruct((T // KEY_CHUNK, d, KEY_CHUNK), bf16)
    res = pl.pallas_call(
        _proj_kernel,
        grid=(T // TM,),
        in_specs=[row(D), full(g), full(w), full(gkv), full(wkv), row(384), row(384)],
        out_specs=[tspec(-w_) if w_ < 0 else row(w_) for _, w_, _ in outs],
        out_shape=[tshape(-w_) if w_ < 0 else jax.ShapeDtypeStruct((T, w_), dt) for _, w_, dt in outs],
        compiler_params=pltpu.CompilerParams(dimension_semantics=("arbitrary",),
                                             vmem_limit_bytes=VMEM_LIMIT),
        name="proj",
    )(x, g, w, gkv, wkv, rope_h, rope_i)
    return dict(zip([n for n, _, _ in outs], res))


KEY_OF_NEG_MAX = -(2 ** 31 - 2 ** 23)
HALF_BITS = 16
COUNT_SLAB = 32
COUNT16_SLAB = 64


def _column_max(s):
    m = s[0:BF16_ROWS, :]
    for r in range(1, s.shape[0] // BF16_ROWS):
        m = jnp.maximum(m, s[r * BF16_ROWS:(r + 1) * BF16_ROWS, :])
    return jnp.max(m.astype(f32), axis=0, keepdims=True)


def _attend_chunks(prepare, block_operands, values_t, s_ref, acc_ref, c_lo, c_hi, extra):
    n_heads = s_ref.shape[0]
    TQ = s_ref.shape[2]
    blocks_per_chunk = KEY_CHUNK // LANES
    acc_ref[...] = jnp.zeros_like(acc_ref)

    def step(c0, n_chunks, st):
        maxes, extra = st
        n_blocks = n_chunks * blocks_per_chunk
        rows = n_blocks * LANES
        extra = prepare(c0 * blocks_per_chunk, n_blocks, extra)
        ones = jnp.ones((acc_ref.shape[1] - HEAD_DIM, rows), bf16)
        new_maxes = [None] * n_heads

        def logits(h):
            blocks = [block_operands(h, c0 * blocks_per_chunk, j) for j in range(n_blocks)]
            raw = [_nt_dot(kb, q) for kb, q, _ in blocks]
            masked = [r.astype(bf16) + bias for r, (_, _, bias) in zip(raw, blocks)]
            for j, s in enumerate(masked):
                s_ref[h, j * LANES:(j + 1) * LANES, :] = s
            m_new = maxes[h]
            for s in masked:
                m_new = jnp.maximum(m_new, _column_max(s))
            new_maxes[h] = m_new

        def weigh(h):
            vt = jnp.concatenate([values_t(h, c0 + j) for j in range(n_chunks)], axis=1)
            e = jnp.exp2(s_ref[h, 0:rows, :] - new_maxes[h].astype(bf16))
            pv = jnp.dot(jnp.concatenate([vt, ones], axis=0), e, preferred_element_type=f32)
            acc_ref[h] = jnp.exp2(maxes[h] - new_maxes[h]) * acc_ref[h] + pv

        for h in range(n_heads + LOGITS_LEAD):
            if h < n_heads:
                logits(h)
            if h >= LOGITS_LEAD:
                weigh(h - LOGITS_LEAD)
        return tuple(new_maxes), extra

    st = ((jnp.full((1, TQ), -3.0e38, f32),) * n_heads, extra)
    p_lo = (c_lo + SPAN_CHUNKS - 1) // SPAN_CHUNKS
    p_hi = jnp.maximum(c_hi // SPAN_CHUNKS, p_lo)
    assert SPAN_CHUNKS == 2
    st = lax.cond(c_lo < p_lo * SPAN_CHUNKS, lambda s: step(c_lo, 1, s), lambda s: s, st)
    st = lax.fori_loop(p_lo, p_hi, lambda p, s: step(p * SPAN_CHUNKS, SPAN_CHUNKS, s), st)
    st = lax.cond(c_hi > p_hi * SPAN_CHUNKS, lambda s: step(c_hi - 1, 1, s), lambda s: s, st)
    return [acc_ref[h, 0:HEAD_DIM, :] / acc_ref[h, HEAD_DIM:HEAD_DIM + 1, :] for h in range(n_heads)]


def _ordered_key(v):
    bits = lax.bitcast_convert_type(v, jnp.int32)
    return bits ^ ((bits >> 31) & 0x7FFFFFFF)


def _dsa_kernel(qa_ref, qidx_ref, misc_ref, kidx_ref, ka_ref, vat_ref, o_ref,
                key_ref, s_ref, acc_ref, qm_ref, bias_ref, k16_ref, *, n_keep):
    TQ = qa_ref.shape[0]
    CH = KEY_CHUNK
    q0 = pl.program_id(1) * TQ
    tpos = q0 + lax.broadcasted_iota(jnp.int32, (1, TQ), 1)
    lane = lax.broadcasted_iota(jnp.int32, (1, LANES), 1)
    k = float(n_keep)

    SUB = LANES
    span = SPAN_CHUNKS * CH
    subs_per_span = span // SUB
    n_span = (q0 + TQ + span - 1) // span

    def sub_rows(i):
        return pl.ds(pl.multiple_of(i * SUB, SUB), SUB)

    w_t = misc_ref[...].T[0:8, :]

    def index_q(h):
        qidx = qidx_ref[...]
        return jnp.where(lane // IDX_DIM == h, qidx, jnp.zeros_like(qidx))

    def score_span(p, carry):
        for i in [p * subs_per_span + j for j in range(subs_per_span)]:
            kb = kidx_ref[sub_rows(i), :]
            sc = jnp.zeros((SUB, TQ), f32)
            for h in range(N_IDX_HEADS):
                sc = sc + jnp.maximum(_nt_dot(kb, index_q(h)), 0.0) * w_t[h:h + 1, :]
            kpos = i * SUB + lax.broadcasted_iota(jnp.int32, (SUB, 1), 0)
            sc = jnp.where(sc == 0.0, 0.0, sc)
            key = _ordered_key(jnp.where(kpos <= tpos, sc, -jnp.inf))
            key_ref[sub_rows(i), :] = key
            k16_ref[sub_rows(i), :] = (key >> HALF_BITS).astype(jnp.int16)
        return carry

    lax.fori_loop(0, n_span, score_span, 0)

    need = (tpos + 1) > n_keep
    half = 1 << (HALF_BITS - 1)

    def search(spans):
        def count_ge(v):
            vs = jnp.broadcast_to(v, (COUNT_SLAB, TQ))
            acc = jnp.zeros((COUNT_SLAB, TQ), jnp.int32)
            for r in range(spans * span // COUNT_SLAB):
                acc = acc + jnp.where(key_ref[r * COUNT_SLAB:(r + 1) * COUNT_SLAB, :] >= vs, 1, 0)
            return jnp.sum(acc.astype(f32), axis=0, keepdims=True)

        def count16_ge(v):
            vs = jnp.broadcast_to(v.astype(jnp.int16), (COUNT16_SLAB, TQ))
            one, zero = jnp.int16(1), jnp.int16(0)
            acc = jnp.zeros((COUNT16_SLAB, TQ), jnp.int16)
            for r in range(spans * span // COUNT16_SLAB):
                acc = acc + jnp.where(k16_ref[r * COUNT16_SLAB:(r + 1) * COUNT16_SLAB, :] >= vs, one, zero)
            return jnp.sum(acc.astype(jnp.int32).astype(f32), axis=0, keepdims=True)

        def bisect16(_, st):
            lo, hi, at, hit, active = st
            act = active > 0
            mid = (lo + hi) >> 1
            collapsed = (hi - lo) <= 1
            c = count16_ge(mid)
            exact = act & ~collapsed & (c == k)
            at = jnp.where(exact, mid, at)
            hit = jnp.where(exact, 1, hit)
            go = act & ~collapsed & ~exact
            lo = jnp.where(go & (c > k), mid, lo)
            hi = jnp.where(go & (c < k), mid, hi)
            return lo, hi, at, hit, jnp.where(go, 1, 0)

        lo0 = jnp.full((1, TQ), -half, jnp.int32)
        hi0 = jnp.full((1, TQ), half, jnp.int32)
        zero = jnp.zeros((1, TQ), jnp.int32)
        st = lax.fori_loop(0, HALF_BITS, bisect16, (lo0, hi0, zero, zero, jnp.where(need, 1, 0)))
        prefix, at_hi, hit_hi = st[0], st[2], st[3] > 0

        for i in range(spans * subs_per_span):
            key = key_ref[i * SUB:(i + 1) * SUB, :]
            hi_half = key >> HALF_BITS
            low = (key & ((1 << HALF_BITS) - 1)) - half
            low = jnp.where(hi_half > prefix, half - 1, jnp.where(hi_half < prefix, -half, low))
            k16_ref[i * SUB:(i + 1) * SUB, :] = low.astype(jnp.int16)

        st = lax.fori_loop(0, HALF_BITS, bisect16, (lo0, hi0, zero, zero, jnp.where(need & ~hit_hi, 1, 0)))
        low_thr = jnp.where(st[3] > 0, st[2], st[0]) + half
        thr = jnp.where(hit_hi, at_hi << HALF_BITS, (prefix << HALF_BITS) | low_thr)
        thr = jnp.where(need, thr, KEY_OF_NEG_MAX)
        return thr, k - count_ge(thr + 1)

    max_spans = key_ref.shape[0] // span
    thr, needed = lax.switch(n_span - 1, [functools.partial(search, n) for n in range(1, max_spans + 1)])

    ri = lax.broadcasted_iota(jnp.int32, (SUB, SUB), 0)
    rj = lax.broadcasted_iota(jnp.int32, (SUB, SUB), 1)
    strict_lower = jnp.where(rj < ri, 1.0, 0.0).astype(bf16)
    for h in range(N_HEADS_A):
        qc = qa_ref[:, (h // 2) * LANES:(h // 2 + 1) * LANES]
        qm_ref[h] = jnp.where(lane // HEAD_DIM == h % 2, qc, jnp.zeros_like(qc))

    def prepare(i0, n, ties_before):
        for j in range(n):
            keys = key_ref[sub_rows(i0 + j), :]
            tie = keys == thr
            tie_f = jnp.where(tie, 1.0, 0.0)
            rank = jnp.dot(strict_lower, tie_f.astype(bf16), preferred_element_type=f32) + ties_before
            sel = (keys > thr) | (tie & (rank < needed))
            bias_ref[j * SUB:(j + 1) * SUB, :] = jnp.where(sel, 0.0, NEG).astype(bf16)
            ties_before = ties_before + jnp.sum(tie_f, axis=0, keepdims=True)
        return ties_before

    def block_operands(h, i0, j):
        return ka_ref[sub_rows(i0 + j), :], qm_ref[h], bias_ref[j * SUB:(j + 1) * SUB, :]

    outs = _attend_chunks(prepare, block_operands, lambda h, c: vat_ref[c], s_ref, acc_ref, 0, (q0 + TQ) // CH,
                          jnp.zeros((1, TQ), f32))
    o_ref[...] = jnp.concatenate(outs, axis=0).T


def _dsa_call(pr, B, S):
    TQ = min(QUERY_TILE, S)
    n_keep = min(TOPK_MAX, S // 4)
    n_chunks = S // KEY_CHUNK
    qspec = lambda w_: pl.BlockSpec((TQ, w_), lambda b, i: (b * (S // TQ) + i, 0))
    kspec = lambda w_: pl.BlockSpec((S, w_), lambda b, i: (b, 0))
    vspec = pl.BlockSpec((n_chunks, HEAD_DIM, KEY_CHUNK), lambda b, i: (b, 0, 0))
    return pl.pallas_call(
        functools.partial(_dsa_kernel, n_keep=n_keep),
        grid=(B, S // TQ),
        in_specs=[qspec(256), qspec(128), qspec(128), kspec(128), kspec(128), vspec],
        out_specs=qspec(256),
        out_shape=jax.ShapeDtypeStruct((B * S, 256), f32),
        scratch_shapes=[pltpu.VMEM((S, TQ), jnp.int32),
                        pltpu.VMEM((N_HEADS_A, SPAN_CHUNKS * KEY_CHUNK, TQ), bf16),
                        pltpu.VMEM((N_HEADS_A, HEAD_DIM + DENOM_ROWS, TQ), f32),
                        pltpu.VMEM((N_HEADS_A, TQ, LANES), bf16),
                        pltpu.VMEM((SPAN_CHUNKS * KEY_CHUNK, TQ), bf16),
                        pltpu.VMEM((S, TQ), jnp.int16)],
        compiler_params=pltpu.CompilerParams(dimension_semantics=("arbitrary", "arbitrary"),
                                             vmem_limit_bytes=VMEM_LIMIT),
        name="dsa",
    )(pr["qa"], pr["qidx"], pr["misc"], pr["kidx"], pr["ka"], pr["vat"])


def _compress_kernel(k_ref, v_ref, pe_ref, w_ref, kc_ref, vct_ref):
    nch = k_ref.shape[0] // CMP_STRIDE
    rowid = lax.broadcasted_iota(jnp.int32, (nch, 1), 0)
    for kv, x_ref in enumerate((k_ref, v_ref)):
        top = jnp.zeros((nch, w_ref.shape[3]), f32)
        bot = jnp.zeros((nch, w_ref.shape[3]), f32)
        for l in range(CMP_STRIDE):
            x = x_ref[pl.ds(l, nch, stride=CMP_STRIDE), :]
            top = top + jnp.dot((x + pe_ref[kv, l:l + 1, :]).astype(bf16), w_ref[kv, l],
                                preferred_element_type=f32)
            bot = bot + jnp.dot((x + pe_ref[kv, CMP_STRIDE + l:CMP_STRIDE + l + 1, :]).astype(bf16),
                                w_ref[kv, CMP_STRIDE + l], preferred_element_type=f32)
        blk = top + pltpu.roll(bot, nch - 1, 0)
        blk = jnp.where(rowid < nch - 1, blk, 0.0)
        if kv == 0:
            kc_ref[...] = blk.astype(bf16)
        else:
            for g in range(N_KV_GROUPS_B):
                vct_ref[g] = blk[:, g * LANES:(g + 1) * LANES].T[0:HEAD_DIM, :].astype(bf16)


def _compress_call(kcmp, vcmp, pe_k, pe_v, w_k, w_v, B, S):
    nch = S // CMP_STRIDE
    G, dh = N_KV_GROUPS_B, HEAD_DIM
    pe = jnp.stack([jnp.concatenate([p] * G, axis=1) for p in (pe_k, pe_v)])
    w = jnp.zeros((2, CMP_BLOCK, G * dh, G * LANES), f32)
    for kv, ws in enumerate((w_k, w_v)):
        for g in range(G):
            w = w.at[kv, :, g * dh:(g + 1) * dh, g * LANES:(g + 1) * LANES].set(
                jnp.concatenate([ws, ws], axis=-1))
    w = w.astype(bf16)
    xspec = pl.BlockSpec((S, G * dh), lambda b: (b, 0))
    return pl.pallas_call(
        _compress_kernel,
        grid=(B,),
        in_specs=[xspec, xspec,
                  pl.BlockSpec(pe.shape, lambda b: (0, 0, 0)),
                  pl.BlockSpec(w.shape, lambda b: (0, 0, 0, 0))],
        out_specs=[pl.BlockSpec((nch, G * LANES), lambda b: (b, 0)),
                   pl.BlockSpec((G, dh, nch), lambda b: (b, 0, 0))],
        out_shape=[jax.ShapeDtypeStruct((B * nch, G * LANES), bf16),
                   jax.ShapeDtypeStruct((B * G, dh, nch), bf16)],
        compiler_params=pltpu.CompilerParams(dimension_semantics=("arbitrary",),
                                             vmem_limit_bytes=VMEM_LIMIT),
        name="compress",
    )(kcmp, vcmp, pe, w)


def _split3(v):
    a = v.astype(bf16)
    r = v - a.astype(f32)
    b = r.astype(bf16)
    c = (r - b.astype(f32)).astype(bf16)
    return a, b, c


def _nsa_kernel(qn_ref, qr_ref, misc_ref, kc_ref, vct_ref, kslc_ref, vslct_ref, kwin_ref, vwint_ref,
                o_ref, s_ref, acc_ref, blk_ref, out_ref, qm_ref, bias_ref, *, n_sel):
    TQ = qn_ref.shape[0]
    CH = KEY_CHUNK
    SUB = LANES
    S = kslc_ref.shape[0]
    nch = kc_ref.shape[0]
    n_cmp = nch - 1
    n_blk = S // SLC_BLOCK
    G = N_KV_GROUPS_B
    R = N_HEADS_B // G
    q0 = pl.program_id(1) * TQ
    n_ch = (q0 + TQ) // CH
    first_win = jnp.maximum(q0 - WINDOW, 0) // CH
    tpos = q0 + lax.broadcasted_iota(jnp.int32, (1, TQ), 1)
    lane = lax.broadcasted_iota(jnp.int32, (1, LANES), 1)
    gate = jax.nn.sigmoid(misc_ref[...].T[0:32, :])

    def sub_rows(i):
        return pl.ds(pl.multiple_of(i * SUB, SUB), SUB)

    def gate_of(h, branch):
        a = GATE_LANE0 + 3 * h + branch
        return gate[a:a + 1, :]

    def head_q(ref, h):
        qc = ref[:, (h // 2) * LANES:(h // 2 + 1) * LANES]
        return jnp.where(lane // HEAD_DIM == h % 2, qc, jnp.zeros_like(qc))

    def attend(k_ref, vt_ref, biases_of, c_lo, c_hi):
        def prepare(i0, n, extra):
            for j in range(n):
                biases = biases_of(i0 + j)
                for g in range(G):
                    bias_ref[g, j * SUB:(j + 1) * SUB, :] = biases[g]
            return extra

        def block_operands(h, i0, j):
            g = h // R
            return (k_ref[sub_rows(i0 + j), g * LANES:(g + 1) * LANES], qm_ref[h],
                    bias_ref[g, j * SUB:(j + 1) * SUB, :])

        def values_t(h, c):
            g = h // R
            return vt_ref[c, g * HEAD_DIM:(g + 1) * HEAD_DIM, :]

        return _attend_chunks(prepare, block_operands, values_t, s_ref, acc_ref, c_lo, c_hi, 0)

    cpos = lax.broadcasted_iota(jnp.int32, (nch, 1), 0)
    cmask = (cpos * CMP_STRIDE + CMP_BLOCK - 1 <= tpos) & (cpos < n_cmp)
    oj = lax.broadcasted_iota(jnp.int32, (n_blk, nch), 0)
    oc = lax.broadcasted_iota(jnp.int32, (n_blk, nch), 1)
    overlap_t = ((oc * CMP_STRIDE <= oj * SLC_BLOCK + SLC_BLOCK - 1)
                 & (oc * CMP_STRIDE + CMP_BLOCK - 1 >= oj * SLC_BLOCK) & (oc < n_cmp))
    overlap_t = jnp.where(overlap_t, 1.0, 0.0).astype(bf16)
    bj = lax.broadcasted_iota(jnp.int32, (n_blk, 1), 0)
    cur = tpos // SLC_BLOCK
    forced = (bj == 0) | (bj == cur) | (bj == cur - 1)
    admissible = bj * SLC_BLOCK <= tpos

    for h in range(N_HEADS_B):
        qm_ref[h] = head_q(qr_ref, h)

    def window_biases(i):
        kpos = i * SUB + lax.broadcasted_iota(jnp.int32, (SUB, 1), 0)
        return [jnp.where((kpos <= tpos) & (kpos > tpos - WINDOW), 0.0, NEG).astype(bf16)] * G

    def selected_biases(i):
        per_sub = SUB // SLC_BLOCK
        kpos = i * SUB + lax.broadcasted_iota(jnp.int32, (SUB, 1), 0)
        biases = []
        for g in range(G):
            parts = []
            for jj in range(per_sub):
                b8 = blk_ref[g * n_blk + i * per_sub + jj]
                parts.extend([b8] * (SLC_BLOCK // b8.shape[0]))
            biases.append(jnp.where(kpos <= tpos, jnp.concatenate(parts, axis=0), NEG).astype(bf16))
        return biases

    all_heads = range(N_HEADS_B)
    lcs = [jnp.where(cmask, _nt_dot(kc_ref[:, (h // R) * LANES:(h // R + 1) * LANES], head_q(qn_ref, h)), NEG)
           for h in all_heads]
    cmaxes = [jnp.max(lc, axis=0, keepdims=True) for lc in lcs]
    es = [jnp.where(cmask, jnp.exp(lc - m), 0.0) for lc, m in zip(lcs, cmaxes)]
    sums = [jnp.sum(e, axis=0, keepdims=True) for e in es]
    ps = [e * (1.0 / jnp.where(l > 0.0, l, 1.0)) for e, l in zip(es, sums)]
    for h in all_heads:
        out_ref[h] = gate_of(h, 0) * jnp.dot(vct_ref[h // R], ps[h].astype(bf16), preferred_element_type=f32)
    psums = [sum(ps[g * R:(g + 1) * R]) for g in range(G)]
    s_blks = [sum(jnp.dot(overlap_t, piece, preferred_element_type=f32) for piece in _split3(psum))
              for psum in psums]
    s_blks = [jnp.where(forced, jnp.inf, jnp.where(admissible, s_blk, -jnp.inf)) for s_blk in s_blks]
    ranks = [jnp.zeros((n_blk, TQ), f32) for _ in range(G)]
    for i in range(n_blk):
        for g in range(G):
            row = s_blks[g][i:i + 1, :]
            beats = (row > s_blks[g]) | ((row == s_blks[g]) & (bj > i))
            ranks[g] = ranks[g] + jnp.where(beats, 1.0, 0.0)
    for g in range(G):
        chosen_bias = jnp.where(ranks[g] < float(n_sel), 0.0, NEG)
        for j in range(n_blk):
            blk_ref[g * n_blk + j] = jnp.broadcast_to(chosen_bias[j:j + 1, :], blk_ref.shape[1:])

    for branch, k_ref, vt_ref, biases_of, c_lo in ((1, kslc_ref, vslct_ref, selected_biases, 0),
                                                   (2, kwin_ref, vwint_ref, window_biases, first_win)):
        o_branch = attend(k_ref, vt_ref, biases_of, c_lo, n_ch)
        for h in range(N_HEADS_B):
            out_ref[h] += gate_of(h, branch) * o_branch[h]
    o_ref[...] = jnp.concatenate([out_ref[h] for h in range(N_HEADS_B)], axis=0).T


def _nsa_call(pr, kc, vct, B, S):
    TQ = min(QUERY_TILE, S)
    n_sel = min(N_SLC, S // SLC_BLOCK)
    nch = S // CMP_STRIDE
    n_chunks = S // KEY_CHUNK
    G, R = N_KV_GROUPS_B, N_HEADS_B // N_KV_GROUPS_B
    qspec = lambda w_: pl.BlockSpec((TQ, w_), lambda b, i: (b * (S // TQ) + i, 0))
    kspec = lambda w_: pl.BlockSpec((S, w_), lambda b, i: (b, 0))
    vspec = pl.BlockSpec((n_chunks, G * HEAD_DIM, KEY_CHUNK), lambda b, i: (b, 0, 0))
    return pl.pallas_call(
        functools.partial(_nsa_kernel, n_sel=n_sel),
        grid=(B, S // TQ),
        in_specs=[qspec(512), qspec(512), qspec(128),
                  pl.BlockSpec((nch, G * LANES), lambda b, i: (b, 0)),
                  pl.BlockSpec((G, HEAD_DIM, nch), lambda b, i: (b, 0, 0)),
                  kspec(256), vspec, kspec(256), vspec],
        out_specs=qspec(512),
        out_shape=jax.ShapeDtypeStruct((B * S, 512), f32),
        scratch_shapes=[pltpu.VMEM((N_HEADS_B, SPAN_CHUNKS * KEY_CHUNK, TQ), bf16),
                        pltpu.VMEM((N_HEADS_B, HEAD_DIM + DENOM_ROWS, TQ), f32),
                        pltpu.VMEM((G * (S // SLC_BLOCK), 8, TQ), f32),
                        pltpu.VMEM((N_HEADS_B, HEAD_DIM, TQ), f32),
                        pltpu.VMEM((N_HEADS_B, TQ, LANES), bf16),
                        pltpu.VMEM((G, SPAN_CHUNKS * KEY_CHUNK, TQ), bf16)],
        compiler_params=pltpu.CompilerParams(dimension_semantics=("arbitrary", "arbitrary"),
                                             vmem_limit_bytes=VMEM_LIMIT),
        name="nsa",
    )(pr["qbn"], pr["qbr"], pr["misc"], kc, vct, pr["kslc"], pr["vslct"], pr["kwin"], pr["vwint"])


def _pool_kernel(u_ref, w_ref, o_ref):
    u = u_ref[...]
    S, C = u.shape
    row = lax.broadcasted_iota(jnp.int32, (S, 1), 0)
    grp = lax.broadcasted_iota(jnp.int32, (1, C), 1) // POOL_GROUP_DIM
    run, width = u, 1
    win_sum = jnp.zeros_like(u)
    win_len = jnp.zeros((1, C), f32)
    for gi, w in enumerate(POOL_WINDOWS):
        while width < w:
            run = run + jnp.where(row >= width, pltpu.roll(run, width, 0), 0.0)
            width *= 2
        assert width == w, "pooling windows must be increasing powers of two"
        win_sum = jnp.where(grp == gi, run, win_sum)
        win_len = jnp.where(grp == gi, float(w), win_len)
    pooled = win_sum / jnp.minimum((row + 1).astype(f32), win_len) - u
    o_ref[...] = jnp.dot(pooled.astype(bf16), w_ref[...], preferred_element_type=f32)


def _pool_call(uc, w_pool, B, S):
    C = N_POOL_GROUPS * POOL_GROUP_DIM
    wbd = jnp.zeros((C, C), f32)
    for gi in range(N_POOL_GROUPS):
        sl = slice(gi * POOL_GROUP_DIM, (gi + 1) * POOL_GROUP_DIM)
        wbd = wbd.at[sl, sl].set(w_pool[gi])
    spec = pl.BlockSpec((S, C), lambda b: (b, 0))
    return pl.pallas_call(
        _pool_kernel,
        grid=(B,),
        in_specs=[spec, pl.BlockSpec((C, C), lambda b: (0, 0))],
        out_specs=spec,
        out_shape=jax.ShapeDtypeStruct((B * S, C), f32),
        compiler_params=pltpu.CompilerParams(dimension_semantics=("arbitrary",),
                                             vmem_limit_bytes=VMEM_LIMIT),
        name="pool",
    )(uc, wbd.astype(bf16))


def _merge_mlp_kernel(x_ref, oa_ref, ob_ref, oc_ref, ga_ref, gb_ref, ps_ref, wo_ref, gpm_ref,
                      gmlp_ref, wup_ref, wdn_ref, gpost_ref, o_ref):
    mixed = jnp.concatenate([_rms(oa_ref[...], ga_ref[...]), _rms(ob_ref[...], gb_ref[...]),
                             _rms(oc_ref[...]) * ps_ref[...]], axis=1).astype(bf16)
    x = x_ref[...] + _rms(jnp.dot(mixed, wo_ref[...], preferred_element_type=f32), gpm_ref[...])
    h = _rms(x, gmlp_ref[...]).astype(bf16)
    up = jnp.maximum(jnp.dot(h, wup_ref[...], preferred_element_type=f32), 0.0)
    f = jnp.dot((up * up).astype(bf16), wdn_ref[...], preferred_element_type=f32)
    o_ref[...] = x + _rms(f, gpost_ref[...])


def _merge_mlp_call(x, oa, ob, oc, ga, gb, ps, wo, gpm, gmlp, wup, wdn, gpost):
    T, D = x.shape
    TM = TOKEN_TILE
    row = lambda a: pl.BlockSpec((TM, a.shape[1]), lambda i: (i, 0))
    full = lambda a: pl.BlockSpec(a.shape, lambda i: (0, 0), pipeline_mode=pl.Buffered(1))
    args = (x, oa, ob, oc, ga, gb, ps, wo, gpm, gmlp, wup, wdn, gpost)
    return pl.pallas_call(
        _merge_mlp_kernel,
        grid=(T // TM,),
        in_specs=[row(a) for a in args[:4]] + [full(a) for a in args[4:]],
        out_specs=pl.BlockSpec((TM, D), lambda i: (i, 0)),
        out_shape=jax.ShapeDtypeStruct((T, D), f32),
        compiler_params=pltpu.CompilerParams(dimension_semantics=("arbitrary",),
                                             vmem_limit_bytes=VMEM_LIMIT),
        name="merge_mlp",
    )(*args)


def _rope_lane_tables(positions, rot_dim, period):
    half = rot_dim // 2
    inv_freq = ROPE_THETA ** (-jnp.arange(0, rot_dim, 2, dtype=f32) / rot_dim)
    ang = positions.astype(f32).reshape(-1, 1) * inv_freq
    src = jnp.concatenate([jnp.cos(ang), jnp.sin(ang), jnp.ones((ang.shape[0], 1), f32)], axis=1)
    lane = jnp.arange(LANES) % period
    freq = jnp.arange(half)
    first, second = lane[None, :] == freq[:, None], lane[None, :] == freq[:, None] + half
    zeros = jnp.zeros((half, LANES), f32)
    c = jnp.concatenate([(first | second).astype(f32), zeros, (lane >= rot_dim).astype(f32)[None, :]])
    sa = jnp.concatenate([zeros, -first.astype(f32), jnp.zeros((1, LANES), f32)])
    sb = jnp.concatenate([zeros, second.astype(f32), jnp.zeros((1, LANES), f32)])
    select = jnp.concatenate([c, sa, sb], axis=1)
    return jnp.dot(src, select, precision=lax.Precision.HIGHEST)


def _dup_halves(w):
    D = w.shape[0]
    g = w.reshape(D, -1, 1, HEAD_DIM)
    return jnp.broadcast_to(g, (D, g.shape[1], 2, HEAD_DIM)).reshape(D, -1)


def _layout_w_in(w_in):
    D = w_in.shape[0]
    widths = (256, 128, 128, 32, 4, 512, 128, 128, 128, 128, 128, 128, 24, 256)
    offs = [0]
    for w_ in widths:
        offs.append(offs[-1] + w_)
    (q_a, ckv, q_idx, k_idx, w_idx, q_b, k_cmp, v_cmp, k_slc, v_slc, k_win, v_win, gates, u_c) = [
        w_in[:, offs[i]:offs[i + 1]] for i in range(len(widths))]
    misc = jnp.concatenate([w_idx, gates, jnp.zeros((D, LANES - 28), w_in.dtype)], axis=1)
    cols = [q_a, ckv, q_idx, jnp.tile(k_idx, (1, LANES // IDX_DIM)), q_b, k_cmp, v_cmp,
            _dup_halves(k_slc), v_slc, _dup_halves(k_win), v_win, misc, u_c]
    w = jnp.concatenate(cols, axis=1)
    assert w.shape[1] == N_PROJ
    return w.astype(bf16)


def kernel(x, positions, w_in, g_kv_a, w_kv_up_a, w_cmp_k, w_cmp_v, pe_cmp_k, pe_cmp_v, w_pool,
           pool_scale, g_out_a, g_out_b, w_out, g_pre_mix, g_post_mix, g_pre_mlp, g_post_mlp,
           w_up, w_down):
    B, S, D = x.shape
    depth = w_in.shape[0]
    assert S % (SPAN_CHUNKS * KEY_CHUNK) == 0 and QUERY_TILE == KEY_CHUNK
    assert (B * S) % TOKEN_TILE == 0
    rope_h = _rope_lane_tables(positions, ROT_DIM_HEAD, HEAD_DIM)
    rope_i = _rope_lane_tables(positions, ROT_DIM_IDX, IDX_DIM)
    row = lambda v: v.reshape(1, -1)
    xt = x.reshape(B * S, D)
    for l in range(depth):
        wkv = jnp.concatenate([_dup_halves(w_kv_up_a[l][:, :HEAD_DIM]), w_kv_up_a[l][:, HEAD_DIM:],
                               jnp.zeros((KV_RANK_A, LANES - HEAD_DIM), f32)], axis=1).astype(bf16)
        pr = _proj_call(xt, row(g_pre_mix[l]), _layout_w_in(w_in[l]), row(g_kv_a[l]), wkv, rope_h, rope_i)
        o_a = _dsa_call(pr, B, S)
        kc, vc = _compress_call(pr["kcmp"], pr["vcmp"], pe_cmp_k[l], pe_cmp_v[l], w_cmp_k[l], w_cmp_v[l], B, S)
        o_b = _nsa_call(pr, kc, vc, B, S)
        o_c = _pool_call(pr["uc"], w_pool[l], B, S)
        xt = _merge_mlp_call(xt, o_a, o_b, o_c, row(g_out_a[l]), row(g_out_b[l]), row(pool_scale[l]),
                             w_out[l].astype(bf16), row(g_post_mix[l]), row(g_pre_mlp[l]),
                             w_up[l].astype(bf16), w_down[l].astype(bf16), row(g_post_mlp[l]))
    return xt.reshape(B, S, D)
```

```python
import functools

import jax
import jax.numpy as jnp
from jax import lax
from jax.experimental import pallas as pl
from jax.experimental.pallas import tpu as pltpu

f32 = jnp.float32
bf16 = jnp.bfloat16

HEAD_DIM = 64
N_HEADS_A = 4
N_HEADS_B = 8
N_KV_GROUPS_B = 2
N_POOL_GROUPS = 4
POOL_GROUP_DIM = 64
POOL_WINDOWS = (2, 4, 8, 16)
KV_RANK_A = 128
N_IDX_HEADS = 4
IDX_DIM = 32
TOPK_MAX = 256
CMP_BLOCK = 32
CMP_STRIDE = 16
SLC_BLOCK = 64
N_SLC = 16
WINDOW = 512
ROPE_THETA = 500000.0
ROT_DIM_HEAD = HEAD_DIM // 4
ROT_DIM_IDX = IDX_DIM // 4
EPS = 1e-6
NEG = -1e30
SCALE = HEAD_DIM ** -0.5

LANES = 128
TOKEN_TILE = 512
QUERY_TILE = 256
KEY_CHUNK = 256
BF16_ROWS = 16
DENOM_ROWS = BF16_ROWS
LOG2E = 1.4426950408889634
LOGITS_LEAD = 3
SPAN_CHUNKS = 2
VMEM_LIMIT = 56 * 1024 * 1024

_COLS = {}
_off = 0
for _name, _w in (("qa", 256), ("ckv", 128), ("qidx", 128), ("kidx", 128), ("qb", 512),
                  ("kvcmp", 256), ("kslc", 256), ("vslc", 128), ("kwin", 256), ("vwin", 128),
                  ("misc", 128), ("uc", 256)):
    _COLS[_name] = (_off, _off + _w)
    _off += _w
N_PROJ = _off
GATE_LANE0 = N_IDX_HEADS


def _rms(v, gain=None):
    y = v * lax.rsqrt(jnp.mean(v * v, axis=-1, keepdims=True) + EPS)
    return y if gain is None else y * gain


def _rope128(v, c, sa, sb, half):
    return v * c + pltpu.roll(v, LANES - half, 1) * sa + pltpu.roll(v, half, 1) * sb


def _nt_dot(a, b):
    return lax.dot_general(a, b, (((1,), (1,)), ((), ())), preferred_element_type=f32)


def _proj_kernel(x_ref, g_ref, w_ref, gkv_ref, wkv_ref, rh_ref, ri_ref,
                 qa_ref, ka_ref, vat_ref, qidx_ref, kidx_ref, qbn_ref, qbr_ref,
                 kcmp_ref, vcmp_ref, kslc_ref, vslct_ref, kwin_ref, vwint_ref, misc_ref, uc_ref):
    h = _rms(x_ref[...], g_ref[...])
    p = jnp.dot(h.astype(bf16), w_ref[...], preferred_element_type=f32)
    hc, hsa, hsb = rh_ref[:, 0:128], rh_ref[:, 128:256], rh_ref[:, 256:384]
    ic, isa, isb = ri_ref[:, 0:128], ri_ref[:, 128:256], ri_ref[:, 256:384]
    hh, ih = ROT_DIM_HEAD // 2, ROT_DIM_IDX // 2

    def col(name, j=0):
        a = _COLS[name][0] + j * LANES
        return p[:, a:a + LANES]

    for j in range(2):
        qa_ref[:, j * LANES:(j + 1) * LANES] = (_rope128(col("qa", j), hc, hsa, hsb, hh)
                                                * (SCALE * LOG2E)).astype(bf16)
    kv = jnp.dot(_rms(col("ckv"), gkv_ref[...]).astype(bf16), wkv_ref[...], preferred_element_type=f32)
    ka_ref[...] = _rope128(kv[:, 0:LANES], hc, hsa, hsb, hh).astype(bf16)
    for cc in range(vat_ref.shape[0]):
        tok = slice(cc * KEY_CHUNK, (cc + 1) * KEY_CHUNK)
        vat_ref[cc] = kv[tok, LANES:2 * LANES].T[0:HEAD_DIM, :].astype(bf16)
        vslct_ref[cc] = col("vslc")[tok, :].T.astype(bf16)
        vwint_ref[cc] = col("vwin")[tok, :].T.astype(bf16)
    qidx_ref[...] = _rope128(col("qidx"), ic, isa, isb, ih).astype(bf16)
    kidx_ref[...] = _rope128(col("kidx"), ic, isa, isb, ih).astype(bf16)
    for j in range(4):
        q = col("qb", j)
        qbn_ref[:, j * LANES:(j + 1) * LANES] = (q * SCALE).astype(bf16)
        qbr_ref[:, j * LANES:(j + 1) * LANES] = (_rope128(q, hc, hsa, hsb, hh) * (SCALE * LOG2E)).astype(bf16)
    for j in range(2):
        sl = slice(j * LANES, (j + 1) * LANES)
        (kcmp_ref, vcmp_ref)[j][...] = col("kvcmp", j)
        kslc_ref[:, sl] = _rope128(col("kslc", j), hc, hsa, hsb, hh).astype(bf16)
        kwin_ref[:, sl] = _rope128(col("kwin", j), hc, hsa, hsb, hh).astype(bf16)
        uc_ref[:, sl] = col("uc", j)
    misc_ref[...] = col("misc")


def _proj_call(x, g, w, gkv, wkv, rope_h, rope_i):
    T, D = x.shape
    TM = TOKEN_TILE
    row = lambda w_: pl.BlockSpec((TM, w_), lambda i: (i, 0))
    full = lambda a: pl.BlockSpec(a.shape, lambda i: (0, 0))
    outs = [("qa", 256, bf16), ("ka", 128, bf16), ("vat", -HEAD_DIM, bf16), ("qidx", 128, bf16),
            ("kidx", 128, bf16), ("qbn", 512, bf16), ("qbr", 512, bf16), ("kcmp", 128, f32), ("vcmp", 128, f32),
            ("kslc", 256, bf16), ("vslct", -LANES, bf16), ("kwin", 256, bf16), ("vwint", -LANES, bf16),
            ("misc", 128, f32), ("uc", 256, f32)]
    tspec = lambda d: pl.BlockSpec((TM // KEY_CHUNK, d, KEY_CHUNK), lambda i: (i, 0, 0))
    tshape = lambda d: jax.ShapeDtypeStruct((T // KEY_CHUNK, d, KEY_CHUNK), bf16)
    res = pl.pallas_call(
        _proj_kernel,
        grid=(T // TM,),
        in_specs=[row(D), full(g), full(w), full(gkv), full(wkv), row(384), row(384)],
        out_specs=[tspec(-w_) if w_ < 0 else row(w_) for _, w_, _ in outs],
        out_shape=[tshape(-w_) if w_ < 0 else jax.ShapeDtypeStruct((T, w_), dt) for _, w_, dt in outs],
        compiler_params=pltpu.CompilerParams(dimension_semantics=("arbitrary",),
                                             vmem_limit_bytes=VMEM_LIMIT),
        name="proj",
    )(x, g, w, gkv, wkv, rope_h, rope_i)
    return dict(zip([n for n, _, _ in outs], res))


KEY_OF_NEG_MAX = -(2 ** 31 - 2 ** 23)
HALF_BITS = 16
COUNT_SLAB = 32
COUNT16_SLAB = 64


def _column_max(s):
    m = s[0:BF16_ROWS, :]
    for r in range(1, s.shape[0] // BF16_ROWS):
        m = jnp.maximum(m, s[r * BF16_ROWS:(r + 1) * BF16_ROWS, :])
    return jnp.max(m.astype(f32), axis=0, keepdims=True)


def _attend_chunks(prepare, block_operands, values_t, s_ref, acc_ref, c_lo, c_hi, extra):
    n_heads = s_ref.shape[0]
    TQ = s_ref.shape[2]
    blocks_per_chunk = KEY_CHUNK // LANES
    acc_ref[...] = jnp.zeros_like(acc_ref)

    def step(c0, n_chunks, st):
        maxes, extra = st
        n_blocks = n_chunks * blocks_per_chunk
        rows = n_blocks * LANES
        extra = prepare(c0 * blocks_per_chunk, n_blocks, extra)
        ones = jnp.ones((acc_ref.shape[1] - HEAD_DIM, rows), bf16)
        new_maxes = [None] * n_heads

        def logits(h):
            blocks = [block_operands(h, c0 * blocks_per_chunk, j) for j in range(n_blocks)]
            raw = [_nt_dot(kb, q) for kb, q, _ in blocks]
            masked = [r.astype(bf16) + bias for r, (_, _, bias) in zip(raw, blocks)]
            for j, s in enumerate(masked):
                s_ref[h, j * LANES:(j + 1) * LANES, :] = s
            m_new = maxes[h]
            for s in masked:
                m_new = jnp.maximum(m_new, _column_max(s))
            new_maxes[h] = m_new

        def weigh(h):
            vt = jnp.concatenate([values_t(h, c0 + j) for j in range(n_chunks)], axis=1)
            e = jnp.exp2(s_ref[h, 0:rows, :] - new_maxes[h].astype(bf16))
            pv = jnp.dot(jnp.concatenate([vt, ones], axis=0), e, preferred_element_type=f32)
            acc_ref[h] = jnp.exp2(maxes[h] - new_maxes[h]) * acc_ref[h] + pv

        for h in range(n_heads + LOGITS_LEAD):
            if h < n_heads:
                logits(h)
            if h >= LOGITS_LEAD:
                weigh(h - LOGITS_LEAD)
        return tuple(new_maxes), extra

    st = ((jnp.full((1, TQ), -3.0e38, f32),) * n_heads, extra)
    p_lo = (c_lo + SPAN_CHUNKS - 1) // SPAN_CHUNKS
    p_hi = jnp.maximum(c_hi // SPAN_CHUNKS, p_lo)
    assert SPAN_CHUNKS == 2
    st = lax.cond(c_lo < p_lo * SPAN_CHUNKS, lambda s: step(c_lo, 1, s), lambda s: s, st)
    st = lax.fori_loop(p_lo, p_hi, lambda p, s: step(p * SPAN_CHUNKS, SPAN_CHUNKS, s), st)
    st = lax.cond(c_hi > p_hi * SPAN_CHUNKS, lambda s: step(c_hi - 1, 1, s), lambda s: s, st)
    return [acc_ref[h, 0:HEAD_DIM, :] / acc_ref[h, HEAD_DIM:HEAD_DIM + 1, :] for h in range(n_heads)]


def _ordered_key(v):
    bits = lax.bitcast_convert_type(v, jnp.int32)
    return bits ^ ((bits >> 31) & 0x7FFFFFFF)


def _dsa_kernel(qa_ref, qidx_ref, misc_ref, kidx_ref, ka_ref, vat_ref, o_ref,
                key_ref, s_ref, acc_ref, qm_ref, bias_ref, k16_ref, *, n_keep):
    TQ = qa_ref.shape[0]
    CH = KEY_CHUNK
    q0 = pl.program_id(1) * TQ
    tpos = q0 + lax.broadcasted_iota(jnp.int32, (1, TQ), 1)
    lane = lax.broadcasted_iota(jnp.int32, (1, LANES), 1)
    k = float(n_keep)

    SUB = LANES
    subs_per_chunk = CH // SUB
    n_ch = (q0 + TQ) // CH

    def sub_rows(i):
        return pl.ds(pl.multiple_of(i * SUB, SUB), SUB)

    w_t = misc_ref[...].T[0:8, :]

    def index_q(h):
        qidx = qidx_ref[...]
        return jnp.where(lane // IDX_DIM == h, qidx, jnp.zeros_like(qidx))

    def score_blocks(i0, n):
        for i in [i0 + j for j in range(n)]:
            kb = kidx_ref[sub_rows(i), :]
            sc = jnp.zeros((SUB, TQ), f32)
            for h in range(N_IDX_HEADS):
                sc = sc + jnp.maximum(_nt_dot(kb, index_q(h)), 0.0) * w_t[h:h + 1, :]
            kpos = i * SUB + lax.broadcasted_iota(jnp.int32, (SUB, 1), 0)
            sc = jnp.where(sc == 0.0, 0.0, sc)
            key = _ordered_key(jnp.where(kpos <= tpos, sc, -jnp.inf))
            key_ref[sub_rows(i), :] = key
            k16_ref[sub_rows(i), :] = (key >> HALF_BITS).astype(jnp.int16)

    def score_span(p, carry):
        score_blocks(p * SPAN_CHUNKS * subs_per_chunk, SPAN_CHUNKS * subs_per_chunk)
        return carry

    lax.fori_loop(0, n_ch // SPAN_CHUNKS, score_span, 0)

    @pl.when(n_ch % SPAN_CHUNKS == 1)
    def _():
        score_blocks((n_ch - 1) * subs_per_chunk, subs_per_chunk)

    need = (tpos + 1) > n_keep
    half = 1 << (HALF_BITS - 1)

    def search(chunks):
        if chunks * CH <= n_keep:
            return jnp.full((1, TQ), KEY_OF_NEG_MAX, jnp.int32), jnp.full((1, TQ), k, f32)

        def count_ge(v):
            vs = jnp.broadcast_to(v, (COUNT_SLAB, TQ))
            acc = jnp.zeros((COUNT_SLAB, TQ), jnp.int32)
            for r in range(chunks * CH // COUNT_SLAB):
                acc = acc + jnp.where(key_ref[r * COUNT_SLAB:(r + 1) * COUNT_SLAB, :] >= vs, 1, 0)
            return jnp.sum(acc.astype(f32), axis=0, keepdims=True)

        def count16_ge(v):
            vs = jnp.broadcast_to(v.astype(jnp.int16), (COUNT16_SLAB, TQ))
            one, zero = jnp.int16(1), jnp.int16(0)
            acc = jnp.zeros((COUNT16_SLAB, TQ), jnp.int16)
            for r in range(chunks * CH // COUNT16_SLAB):
                acc = acc + jnp.where(k16_ref[r * COUNT16_SLAB:(r + 1) * COUNT16_SLAB, :] >= vs, one, zero)
            return jnp.sum(acc.astype(jnp.int32).astype(f32), axis=0, keepdims=True)

        def bisect16(_, st):
            lo, hi, at, hit, active = st
            act = active > 0
            mid = (lo + hi) >> 1
            collapsed = (hi - lo) <= 1
            c = count16_ge(mid)
            exact = act & ~collapsed & (c == k)
            at = jnp.where(exact, mid, at)
            hit = jnp.where(exact, 1, hit)
            go = act & ~collapsed & ~exact
            lo = jnp.where(go & (c > k), mid, lo)
            hi = jnp.where(go & (c < k), mid, hi)
            return lo, hi, at, hit, jnp.where(go, 1, 0)

        lo0 = jnp.full((1, TQ), -half, jnp.int32)
        hi0 = jnp.full((1, TQ), half, jnp.int32)
        zero = jnp.zeros((1, TQ), jnp.int32)
        st = lax.fori_loop(0, HALF_BITS, bisect16, (lo0, hi0, zero, zero, jnp.where(need, 1, 0)))
        prefix, at_hi, hit_hi = st[0], st[2], st[3] > 0

        for i in range(chunks * subs_per_chunk):
            key = key_ref[i * SUB:(i + 1) * SUB, :]
            hi_half = key >> HALF_BITS
            low = (key & ((1 << HALF_BITS) - 1)) - half
            low = jnp.where(hi_half > prefix, half - 1, jnp.where(hi_half < prefix, -half, low))
            k16_ref[i * SUB:(i + 1) * SUB, :] = low.astype(jnp.int16)

        st = lax.fori_loop(0, HALF_BITS, bisect16, (lo0, hi0, zero, zero, jnp.where(need & ~hit_hi, 1, 0)))
        low_thr = jnp.where(st[3] > 0, st[2], st[0]) + half
        thr = jnp.where(hit_hi, at_hi << HALF_BITS, (prefix << HALF_BITS) | low_thr)
        thr = jnp.where(need, thr, KEY_OF_NEG_MAX)
        return thr, k - count_ge(thr + 1)

    max_chunks = key_ref.shape[0] // CH
    thr, needed = lax.switch(n_ch - 1, [functools.partial(search, n) for n in range(1, max_chunks + 1)])

    ri = lax.broadcasted_iota(jnp.int32, (SUB, SUB), 0)
    rj = lax.broadcasted_iota(jnp.int32, (SUB, SUB), 1)
    strict_lower = jnp.where(rj < ri, 1.0, 0.0).astype(bf16)
    for h in range(N_HEADS_A):
        qc = qa_ref[:, (h // 2) * LANES:(h // 2 + 1) * LANES]
        qm_ref[h] = jnp.where(lane // HEAD_DIM == h % 2, qc, jnp.zeros_like(qc))

    def prepare(i0, n, ties_before):
        for j in range(n):
            keys = key_ref[sub_rows(i0 + j), :]
            tie = keys == thr
            tie_f = jnp.where(tie, 1.0, 0.0)
            rank = jnp.dot(strict_lower, tie_f.astype(bf16), preferred_element_type=f32) + ties_before
            sel = (keys > thr) | (tie & (rank < needed))
            bias_ref[j * SUB:(j + 1) * SUB, :] = jnp.where(sel, 0.0, NEG).astype(bf16)
            ties_before = ties_before + jnp.sum(tie_f, axis=0, keepdims=True)
        return ties_before

    def block_operands(h, i0, j):
        return ka_ref[sub_rows(i0 + j), :], qm_ref[h], bias_ref[j * SUB:(j + 1) * SUB, :]

    outs = _attend_chunks(prepare, block_operands, lambda h, c: vat_ref[c], s_ref, acc_ref, 0, (q0 + TQ) // CH,
                          jnp.zeros((1, TQ), f32))
    o_ref[...] = jnp.concatenate(outs, axis=0).T


def _dsa_call(pr, B, S):
    TQ = min(QUERY_TILE, S)
    n_keep = min(TOPK_MAX, S // 4)
    n_chunks = S // KEY_CHUNK
    qspec = lambda w_: pl.BlockSpec((TQ, w_), lambda b, i: (b * (S // TQ) + i, 0))
    kspec = lambda w_: pl.BlockSpec((S, w_), lambda b, i: (b, 0))
    vspec = pl.BlockSpec((n_chunks, HEAD_DIM, KEY_CHUNK), lambda b, i: (b, 0, 0))
    return pl.pallas_call(
        functools.partial(_dsa_kernel, n_keep=n_keep),
        grid=(B, S // TQ),
        in_specs=[qspec(256), qspec(128), qspec(128), kspec(128), kspec(128), vspec],
        out_specs=qspec(256),
        out_shape=jax.ShapeDtypeStruct((B * S, 256), f32),
        scratch_shapes=[pltpu.VMEM((S, TQ), jnp.int32),
                        pltpu.VMEM((N_HEADS_A, SPAN_CHUNKS * KEY_CHUNK, TQ), bf16),
                        pltpu.VMEM((N_HEADS_A, HEAD_DIM + DENOM_ROWS, TQ), f32),
                        pltpu.VMEM((N_HEADS_A, TQ, LANES), bf16),
                        pltpu.VMEM((SPAN_CHUNKS * KEY_CHUNK, TQ), bf16),
                        pltpu.VMEM((S, TQ), jnp.int16)],
        compiler_params=pltpu.CompilerParams(dimension_semantics=("arbitrary", "arbitrary"),
                                             vmem_limit_bytes=VMEM_LIMIT),
        name="dsa",
    )(pr["qa"], pr["qidx"], pr["misc"], pr["kidx"], pr["ka"], pr["vat"])


def _compress_kernel(k_ref, v_ref, pe_ref, w_ref, kc_ref, vct_ref):
    nch = k_ref.shape[0] // CMP_STRIDE
    rowid = lax.broadcasted_iota(jnp.int32, (nch, 1), 0)
    for kv, x_ref in enumerate((k_ref, v_ref)):
        top = jnp.zeros((nch, w_ref.shape[3]), f32)
        bot = jnp.zeros((nch, w_ref.shape[3]), f32)
        for l in range(CMP_STRIDE):
            x = x_ref[pl.ds(l, nch, stride=CMP_STRIDE), :]
            top = top + jnp.dot((x + pe_ref[kv, l:l + 1, :]).astype(bf16), w_ref[kv, l],
                                preferred_element_type=f32)
            bot = bot + jnp.dot((x + pe_ref[kv, CMP_STRIDE + l:CMP_STRIDE + l + 1, :]).astype(bf16),
                                w_ref[kv, CMP_STRIDE + l], preferred_element_type=f32)
        blk = top + pltpu.roll(bot, nch - 1, 0)
        blk = jnp.where(rowid < nch - 1, blk, 0.0)
        if kv == 0:
            kc_ref[...] = blk.astype(bf16)
        else:
            for g in range(N_KV_GROUPS_B):
                vct_ref[g] = blk[:, g * LANES:(g + 1) * LANES].T[0:HEAD_DIM, :].astype(bf16)


def _compress_call(kcmp, vcmp, pe_k, pe_v, w_k, w_v, B, S):
    nch = S // CMP_STRIDE
    G, dh = N_KV_GROUPS_B, HEAD_DIM
    pe = jnp.stack([jnp.concatenate([p] * G, axis=1) for p in (pe_k, pe_v)])
    w = jnp.zeros((2, CMP_BLOCK, G * dh, G * LANES), f32)
    for kv, ws in enumerate((w_k, w_v)):
        for g in range(G):
            w = w.at[kv, :, g * dh:(g + 1) * dh, g * LANES:(g + 1) * LANES].set(
                jnp.concatenate([ws, ws], axis=-1))
    w = w.astype(bf16)
    xspec = pl.BlockSpec((S, G * dh), lambda b: (b, 0))
    return pl.pallas_call(
        _compress_kernel,
        grid=(B,),
        in_specs=[xspec, xspec,
                  pl.BlockSpec(pe.shape, lambda b: (0, 0, 0)),
                  pl.BlockSpec(w.shape, lambda b: (0, 0, 0, 0))],
        out_specs=[pl.BlockSpec((nch, G * LANES), lambda b: (b, 0)),
                   pl.BlockSpec((G, dh, nch), lambda b: (b, 0, 0))],
        out_shape=[jax.ShapeDtypeStruct((B * nch, G * LANES), bf16),
                   jax.ShapeDtypeStruct((B * G, dh, nch), bf16)],
        compiler_params=pltpu.CompilerParams(dimension_semantics=("arbitrary",),
                                             vmem_limit_bytes=VMEM_LIMIT),
        name="compress",
    )(kcmp, vcmp, pe, w)


def _split3(v):
    a = v.astype(bf16)
    r = v - a.astype(f32)
    b = r.astype(bf16)
    c = (r - b.astype(f32)).astype(bf16)
    return a, b, c


def _nsa_kernel(qn_ref, qr_ref, misc_ref, kc_ref, vct_ref, kslc_ref, vslct_ref, kwin_ref, vwint_ref,
                o_ref, s_ref, acc_ref, blk_ref, out_ref, qm_ref, bias_ref, *, n_sel):
    TQ = qn_ref.shape[0]
    CH = KEY_CHUNK
    SUB = LANES
    S = kslc_ref.shape[0]
    nch = kc_ref.shape[0]
    n_cmp = nch - 1
    n_blk = S // SLC_BLOCK
    G = N_KV_GROUPS_B
    R = N_HEADS_B // G
    q0 = pl.program_id(1) * TQ
    n_ch = (q0 + TQ) // CH
    first_win = jnp.maximum(q0 - WINDOW, 0) // CH
    tpos = q0 + lax.broadcasted_iota(jnp.int32, (1, TQ), 1)
    lane = lax.broadcasted_iota(jnp.int32, (1, LANES), 1)
    gate = jax.nn.sigmoid(misc_ref[...].T[0:32, :])

    def sub_rows(i):
        return pl.ds(pl.multiple_of(i * SUB, SUB), SUB)

    def gate_of(h, branch):
        a = GATE_LANE0 + 3 * h + branch
        return gate[a:a + 1, :]

    def head_q(ref, h):
        qc = ref[:, (h // 2) * LANES:(h // 2 + 1) * LANES]
        return jnp.where(lane // HEAD_DIM == h % 2, qc, jnp.zeros_like(qc))

    def attend(k_ref, vt_ref, biases_of, c_lo, c_hi):
        def prepare(i0, n, extra):
            for j in range(n):
                biases = biases_of(i0 + j)
                for g in range(G):
                    bias_ref[g, j * SUB:(j + 1) * SUB, :] = biases[g]
            return extra

        def block_operands(h, i0, j):
            g = h // R
            return (k_ref[sub_rows(i0 + j), g * LANES:(g + 1) * LANES], qm_ref[h],
                    bias_ref[g, j * SUB:(j + 1) * SUB, :])

        def values_t(h, c):
            g = h // R
            return vt_ref[c, g * HEAD_DIM:(g + 1) * HEAD_DIM, :]

        return _attend_chunks(prepare, block_operands, values_t, s_ref, acc_ref, c_lo, c_hi, 0)

    cpos = lax.broadcasted_iota(jnp.int32, (nch, 1), 0)
    cmask = (cpos * CMP_STRIDE + CMP_BLOCK - 1 <= tpos) & (cpos < n_cmp)
    oj = lax.broadcasted_iota(jnp.int32, (n_blk, nch), 0)
    oc = lax.broadcasted_iota(jnp.int32, (n_blk, nch), 1)
    overlap_t = ((oc * CMP_STRIDE <= oj * SLC_BLOCK + SLC_BLOCK - 1)
                 & (oc * CMP_STRIDE + CMP_BLOCK - 1 >= oj * SLC_BLOCK) & (oc < n_cmp))
    overlap_t = jnp.where(overlap_t, 1.0, 0.0).astype(bf16)
    bj = lax.broadcasted_iota(jnp.int32, (n_blk, 1), 0)
    cur = tpos // SLC_BLOCK
    forced = (bj == 0) | (bj == cur) | (bj == cur - 1)
    admissible = bj * SLC_BLOCK <= tpos

    for h in range(N_HEADS_B):
        qm_ref[h] = head_q(qr_ref, h)

    def window_biases(i):
        kpos = i * SUB + lax.broadcasted_iota(jnp.int32, (SUB, 1), 0)
        return [jnp.where((kpos <= tpos) & (kpos > tpos - WINDOW), 0.0, NEG).astype(bf16)] * G

    def selected_biases(i):
        per_sub = SUB // SLC_BLOCK
        kpos = i * SUB + lax.broadcasted_iota(jnp.int32, (SUB, 1), 0)
        biases = []
        for g in range(G):
            parts = []
            for jj in range(per_sub):
                b8 = blk_ref[g * n_blk + i * per_sub + jj]
                parts.extend([b8] * (SLC_BLOCK // b8.shape[0]))
            biases.append(jnp.where(kpos <= tpos, jnp.concatenate(parts, axis=0), NEG).astype(bf16))
        return biases

    all_heads = range(N_HEADS_B)
    lcs = [jnp.where(cmask, _nt_dot(kc_ref[:, (h // R) * LANES:(h // R + 1) * LANES], head_q(qn_ref, h)), NEG)
           for h in all_heads]
    cmaxes = [jnp.max(lc, axis=0, keepdims=True) for lc in lcs]
    es = [jnp.where(cmask, jnp.exp(lc - m), 0.0) for lc, m in zip(lcs, cmaxes)]
    sums = [jnp.sum(e, axis=0, keepdims=True) for e in es]
    ps = [e * (1.0 / jnp.where(l > 0.0, l, 1.0)) for e, l in zip(es, sums)]
    for h in all_heads:
        out_ref[h] = gate_of(h, 0) * jnp.dot(vct_ref[h // R], ps[h].astype(bf16), preferred_element_type=f32)
    few_blocks = n_ch * CH <= n_sel * SLC_BLOCK

    @pl.when(few_blocks)
    def _():
        blk_ref[...] = jnp.zeros_like(blk_ref)

    @pl.when(jnp.logical_not(few_blocks))
    def _():
        psums = [sum(ps[g * R:(g + 1) * R]) for g in range(G)]
        s_blks = [sum(jnp.dot(overlap_t, piece, preferred_element_type=f32) for piece in _split3(psum))
                  for psum in psums]
        s_blks = [jnp.where(forced, jnp.inf, jnp.where(admissible, s_blk, -jnp.inf)) for s_blk in s_blks]
        ranks = [jnp.zeros((n_blk, TQ), f32) for _ in range(G)]
        for i in range(n_blk):
            for g in range(G):
                row = s_blks[g][i:i + 1, :]
                beats = (row > s_blks[g]) | ((row == s_blks[g]) & (bj > i))
                ranks[g] = ranks[g] + jnp.where(beats, 1.0, 0.0)
        for g in range(G):
            chosen_bias = jnp.where(ranks[g] < float(n_sel), 0.0, NEG)
            for j in range(n_blk):
                blk_ref[g * n_blk + j] = jnp.broadcast_to(chosen_bias[j:j + 1, :], blk_ref.shape[1:])

    for branch, k_ref, vt_ref, biases_of, c_lo in ((1, kslc_ref, vslct_ref, selected_biases, 0),
                                                   (2, kwin_ref, vwint_ref, window_biases, first_win)):
        o_branch = attend(k_ref, vt_ref, biases_of, c_lo, n_ch)
        for h in range(N_HEADS_B):
            out_ref[h] += gate_of(h, branch) * o_branch[h]
    o_ref[...] = jnp.concatenate([out_ref[h] for h in range(N_HEADS_B)], axis=0).T


def _nsa_call(pr, kc, vct, B, S):
    TQ = min(QUERY_TILE, S)
    n_sel = min(N_SLC, S // SLC_BLOCK)
    nch = S // CMP_STRIDE
    n_chunks = S // KEY_CHUNK
    G, R = N_KV_GROUPS_B, N_HEADS_B // N_KV_GROUPS_B
    qspec = lambda w_: pl.BlockSpec((TQ, w_), lambda b, i: (b * (S // TQ) + i, 0))
    kspec = lambda w_: pl.BlockSpec((S, w_), lambda b, i: (b, 0))
    vspec = pl.BlockSpec((n_chunks, G * HEAD_DIM, KEY_CHUNK), lambda b, i: (b, 0, 0))
    return pl.pallas_call(
        functools.partial(_nsa_kernel, n_sel=n_sel),
        grid=(B, S // TQ),
        in_specs=[qspec(512), qspec(512), qspec(128),
                  pl.BlockSpec((nch, G * LANES), lambda b, i: (b, 0)),
                  pl.BlockSpec((G, HEAD_DIM, nch), lambda b, i: (b, 0, 0)),
                  kspec(256), vspec, kspec(256), vspec],
        out_specs=qspec(512),
        out_shape=jax.ShapeDtypeStruct((B * S, 512), f32),
        scratch_shapes=[pltpu.VMEM((N_HEADS_B, SPAN_CHUNKS * KEY_CHUNK, TQ), bf16),
                        pltpu.VMEM((N_HEADS_B, HEAD_DIM + DENOM_ROWS, TQ), f32),
                        pltpu.VMEM((G * (S // SLC_BLOCK), 8, TQ), f32),
                        pltpu.VMEM((N_HEADS_B, HEAD_DIM, TQ), f32),
                        pltpu.VMEM((N_HEADS_B, TQ, LANES), bf16),
                        pltpu.VMEM((G, SPAN_CHUNKS * KEY_CHUNK, TQ), bf16)],
        compiler_params=pltpu.CompilerParams(dimension_semantics=("arbitrary", "arbitrary"),
                                             vmem_limit_bytes=VMEM_LIMIT),
        name="nsa",
    )(pr["qbn"], pr["qbr"], pr["misc"], kc, vct, pr["kslc"], pr["vslct"], pr["kwin"], pr["vwint"])


def _pool_kernel(u_ref, w_ref, o_ref):
    u = u_ref[...]
    S, C = u.shape
    row = lax.broadcasted_iota(jnp.int32, (S, 1), 0)
    grp = lax.broadcasted_iota(jnp.int32, (1, C), 1) // POOL_GROUP_DIM
    run, width = u, 1
    win_sum = jnp.zeros_like(u)
    win_len = jnp.zeros((1, C), f32)
    for gi, w in enumerate(POOL_WINDOWS):
        while width < w:
            run = run + jnp.where(row >= width, pltpu.roll(run, width, 0), 0.0)
            width *= 2
        assert width == w, "pooling windows must be increasing powers of two"
        win_sum = jnp.where(grp == gi, run, win_sum)
        win_len = jnp.where(grp == gi, float(w), win_len)
    pooled = win_sum / jnp.minimum((row + 1).astype(f32), win_len) - u
    o_ref[...] = jnp.dot(pooled.astype(bf16), w_ref[...], preferred_element_type=f32)


def _pool_call(uc, w_pool, B, S):
    C = N_POOL_GROUPS * POOL_GROUP_DIM
    wbd = jnp.zeros((C, C), f32)
    for gi in range(N_POOL_GROUPS):
        sl = slice(gi * POOL_GROUP_DIM, (gi + 1) * POOL_GROUP_DIM)
        wbd = wbd.at[sl, sl].set(w_pool[gi])
    spec = pl.BlockSpec((S, C), lambda b: (b, 0))
    return pl.pallas_call(
        _pool_kernel,
        grid=(B,),
        in_specs=[spec, pl.BlockSpec((C, C), lambda b: (0, 0))],
        out_specs=spec,
        out_shape=jax.ShapeDtypeStruct((B * S, C), f32),
        compiler_params=pltpu.CompilerParams(dimension_semantics=("arbitrary",),
                                             vmem_limit_bytes=VMEM_LIMIT),
        name="pool",
    )(uc, wbd.astype(bf16))


def _merge_mlp_kernel(x_ref, oa_ref, ob_ref, oc_ref, ga_ref, gb_ref, ps_ref, wo_ref, gpm_ref,
                      gmlp_ref, wup_ref, wdn_ref, gpost_ref, o_ref):
    mixed = jnp.concatenate([_rms(oa_ref[...], ga_ref[...]), _rms(ob_ref[...], gb_ref[...]),
                             _rms(oc_ref[...]) * ps_ref[...]], axis=1).astype(bf16)
    x = x_ref[...] + _rms(jnp.dot(mixed, wo_ref[...], preferred_element_type=f32), gpm_ref[...])
    h = _rms(x, gmlp_ref[...]).astype(bf16)
    up = jnp.maximum(jnp.dot(h, wup_ref[...], preferred_element_type=f32), 0.0)
    f = jnp.dot((up * up).astype(bf16), wdn_ref[...], preferred_element_type=f32)
    o_ref[...] = x + _rms(f, gpost_ref[...])


def _merge_mlp_call(x, oa, ob, oc, ga, gb, ps, wo, gpm, gmlp, wup, wdn, gpost):
    T, D = x.shape
    TM = TOKEN_TILE
    row = lambda a: pl.BlockSpec((TM, a.shape[1]), lambda i: (i, 0))
    full = lambda a: pl.BlockSpec(a.shape, lambda i: (0, 0), pipeline_mode=pl.Buffered(1))
    args = (x, oa, ob, oc, ga, gb, ps, wo, gpm, gmlp, wup, wdn, gpost)
    return pl.pallas_call(
        _merge_mlp_kernel,
        grid=(T // TM,),
        in_specs=[row(a) for a in args[:4]] + [full(a) for a in args[4:]],
        out_specs=pl.BlockSpec((TM, D), lambda i: (i, 0)),
        out_shape=jax.ShapeDtypeStruct((T, D), f32),
        compiler_params=pltpu.CompilerParams(dimension_semantics=("arbitrary",),
                                             vmem_limit_bytes=VMEM_LIMIT),
        name="merge_mlp",
    )(*args)


def _rope_lane_tables(positions, rot_dim, period):
    half = rot_dim // 2
    inv_freq = ROPE_THETA ** (-jnp.arange(0, rot_dim, 2, dtype=f32) / rot_dim)
    ang = positions.astype(f32).reshape(-1, 1) * inv_freq
    src = jnp.concatenate([jnp.cos(ang), jnp.sin(ang), jnp.ones((ang.shape[0], 1), f32)], axis=1)
    lane = jnp.arange(LANES) % period
    freq = jnp.arange(half)
    first, second = lane[None, :] == freq[:, None], lane[None, :] == freq[:, None] + half
    zeros = jnp.zeros((half, LANES), f32)
    c = jnp.concatenate([(first | second).astype(f32), zeros, (lane >= rot_dim).astype(f32)[None, :]])
    sa = jnp.concatenate([zeros, -first.astype(f32), jnp.zeros((1, LANES), f32)])
    sb = jnp.concatenate([zeros, second.astype(f32), jnp.zeros((1, LANES), f32)])
    select = jnp.concatenate([c, sa, sb], axis=1)
    return jnp.dot(src, select, precision=lax.Precision.HIGHEST)


def _dup_halves(w):
    D = w.shape[0]
    g = w.reshape(D, -1, 1, HEAD_DIM)
    return jnp.broadcast_to(g, (D, g.shape[1], 2, HEAD_DIM)).reshape(D, -1)


def _layout_w_in(w_in):
    D = w_in.shape[0]
    widths = (256, 128, 128, 32, 4, 512, 128, 128, 128, 128, 128, 128, 24, 256)
    offs = [0]
    for w_ in widths:
        offs.append(offs[-1] + w_)
    (q_a, ckv, q_idx, k_idx, w_idx, q_b, k_cmp, v_cmp, k_slc, v_slc, k_win, v_win, gates, u_c) = [
        w_in[:, offs[i]:offs[i + 1]] for i in range(len(widths))]
    misc = jnp.concatenate([w_idx, gates, jnp.zeros((D, LANES - 28), w_in.dtype)], axis=1)
    cols = [q_a, ckv, q_idx, jnp.tile(k_idx, (1, LANES // IDX_DIM)), q_b, k_cmp, v_cmp,
            _dup_halves(k_slc), v_slc, _dup_halves(k_win), v_win, misc, u_c]
    w = jnp.concatenate(cols, axis=1)
    assert w.shape[1] == N_PROJ
    return w.astype(bf16)


def kernel(x, positions, w_in, g_kv_a, w_kv_up_a, w_cmp_k, w_cmp_v, pe_cmp_k, pe_cmp_v, w_pool,
           pool_scale, g_out_a, g_out_b, w_out, g_pre_mix, g_post_mix, g_pre_mlp, g_post_mlp,
           w_up, w_down):
    B, S, D = x.shape
    depth = w_in.shape[0]
    assert S % (SPAN_CHUNKS * KEY_CHUNK) == 0 and QUERY_TILE == KEY_CHUNK
    assert (B * S) % TOKEN_TILE == 0
    rope_h = _rope_lane_tables(positions, ROT_DIM_HEAD, HEAD_DIM)
    rope_i = _rope_lane_tables(positions, ROT_DIM_IDX, IDX_DIM)
    row = lambda v: v.reshape(1, -1)
    xt = x.reshape(B * S, D)
    for l in range(depth):
        wkv = jnp.concatenate([_dup_halves(w_kv_up_a[l][:, :HEAD_DIM]), w_kv_up_a[l][:, HEAD_DIM:],
                               jnp.zeros((KV_RANK_A, LANES - HEAD_DIM), f32)], axis=1).astype(bf16)
        pr = _proj_call(xt, row(g_pre_mix[l]), _layout_w_in(w_in[l]), row(g_kv_a[l]), wkv, rope_h, rope_i)
        o_a = _dsa_call(pr, B, S)
        kc, vc = _compress_call(pr["kcmp"], pr["vcmp"], pe_cmp_k[l], pe_cmp_v[l], w_cmp_k[l], w_cmp_v[l], B, S)
        o_b = _nsa_call(pr, kc, vc, B, S)
        o_c = _pool_call(pr["uc"], w_pool[l], B, S)
        xt = _merge_mlp_call(xt, o_a, o_b, o_c, row(g_out_a[l]), row(g_out_b[l]), row(pool_scale[l]),
                             w_out[l].astype(bf16), row(g_post_mix[l]), row(g_pre_mlp[l]),
                             w_up[l].astype(bf16), w_down[l].astype(bf16), row(g_post_mlp[l]))
    return xt.reshape(B, S, D)
```
